```python
import math
import jax, jax.numpy as jnp
from jax import lax
import numpy as np

D_MODEL = 1024
BATCH = 8
SEQ = 2048
DEPTH = 1
DEC_BATCH = 32
DEC_SEQ = 8
PAST_LEN = 16384
PAGE_SIZE = 128

MIX_WIDTH = D_MODEL
ATTN_WIDTH = MIX_WIDTH // 2
CONV_CH = MIX_WIDTH - ATTN_WIDTH
N_HEADS = 8
HEAD_DIM = ATTN_WIDTH // N_HEADS
MOBA_BLOCK = 256
MOBA_TOPK = 3
Q_CHUNK = 32
CONV_K = 31
GN_GROUPS = 8
D_FF = ((8 * D_MODEL // 3 + 255) // 256) * 256
DEEPNORM_ALPHA = (2.0 * DEPTH) ** 0.25
DEEPNORM_BETA = (8.0 * DEPTH) ** -0.25
LN_EPS = 1e-5
NEG_INF = -1e30

kernel_name = 'hymba_moba_conformer_macaron_decoder_step'


def _layernorm(x, g, b):
    xf = x.astype(jnp.float32)
    mu = jnp.mean(xf, -1, keepdims=True)
    var = jnp.mean(jnp.square(xf - mu), -1, keepdims=True)
    y = (xf - mu) * lax.rsqrt(var + LN_EPS) * g.astype(jnp.float32) + b.astype(jnp.float32)
    return y.astype(x.dtype)


def _groupnorm_channels(x, g, b):
    shp = x.shape
    xf = x.astype(jnp.float32).reshape(shp[:-1] + (GN_GROUPS, shp[-1] // GN_GROUPS))
    mu = jnp.mean(xf, -1, keepdims=True)
    var = jnp.mean(jnp.square(xf - mu), -1, keepdims=True)
    y = ((xf - mu) * lax.rsqrt(var + LN_EPS)).reshape(shp)
    return (y * g.astype(jnp.float32) + b.astype(jnp.float32)).astype(x.dtype)


def _alibi_slopes():
    return 2.0 ** (-8.0 * jnp.arange(1, N_HEADS + 1, dtype=jnp.float32) / N_HEADS)


def _swiglu(h, wg, wu, wd):
    return (jax.nn.silu(h @ wg) * (h @ wu)) @ wd


def _split_heads(t):
    b, s, _ = t.shape
    return t.reshape(b, s, N_HEADS, HEAD_DIM).transpose(0, 2, 1, 3)


def _merge_heads(t):
    b, h, s, d = t.shape
    return t.transpose(0, 2, 1, 3).reshape(b, s, h * d)


def _project(h, w_in):
    u = h @ w_in
    q, k, v, ga, gb = jnp.split(u, [ATTN_WIDTH, 2 * ATTN_WIDTH, 3 * ATTN_WIDTH, 3 * ATTN_WIDTH + CONV_CH], axis=-1)
    return _split_heads(q), _split_heads(k), _split_heads(v), ga * jax.nn.sigmoid(gb)


def _conv_branch(buf, w_dw, b_dw, gn_g, gn_b):
    y = lax.conv_general_dilated(buf, w_dw[:, None, :], (1,), 'VALID',
                                 dimension_numbers=('NWC', 'WIO', 'NWC'),
                                 feature_group_count=CONV_CH) + b_dw
    return jax.nn.silu(_groupnorm_channels(y, gn_g, gn_b))


def _merge(attn, conv, beta_attn, beta_conv, w_out):
    return jnp.concatenate([_merge_heads(attn) * beta_attn, conv * beta_conv], axis=-1) @ w_out


def _moba_softmax(q, q_pos, sel_k, sel_v, sel_pos, sel_ok, own_k, own_v, own_pos, slopes):
    scale = HEAD_DIM ** -0.5
    m = slopes[None, :, None, None]
    rel_sel = (q_pos[:, None] - sel_pos).astype(jnp.float32)
    s_sel = jnp.einsum('bhqd,bhqkd->bhqk', q, sel_k, preferred_element_type=jnp.float32) * scale - m * rel_sel
    s_sel = jnp.where(sel_ok, s_sel, NEG_INF)
    rel_own = q_pos[:, None] - own_pos[None, :]
    s_own = jnp.einsum('bhqd,bhkd->bhqk', q, own_k, preferred_element_type=jnp.float32) * scale \
        - m * rel_own.astype(jnp.float32)
    s_own = jnp.where(rel_own >= 0, s_own, NEG_INF)
    p = jax.nn.softmax(jnp.concatenate([s_sel, s_own], axis=-1), axis=-1)
    ks = sel_k.shape[3]
    out = jnp.einsum('bhqk,bhqkd->bhqd', p[..., :ks].astype(sel_v.dtype), sel_v, preferred_element_type=jnp.float32) \
        + jnp.einsum('bhqk,bhkd->bhqd', p[..., ks:].astype(own_v.dtype), own_v, preferred_element_type=jnp.float32)
    return out.astype(q.dtype)


def _moba_prompt(q, k, v, slopes):
    b, h, s, dh = q.shape
    nb = -(-s // MOBA_BLOCK)
    pad = nb * MOBA_BLOCK - s
    kb = jnp.pad(k, ((0, 0), (0, 0), (0, pad), (0, 0))).reshape(b, h, nb, MOBA_BLOCK, dh)
    vb = jnp.pad(v, ((0, 0), (0, 0), (0, pad), (0, 0))).reshape(b, h, nb, MOBA_BLOCK, dh)
    kmean = jnp.mean(kb, axis=3, dtype=jnp.float32)
    k_eff = min(MOBA_TOPK, nb - 1)
    n_chunks = s // Q_CHUNK
    qc = q.reshape(b, h, n_chunks, Q_CHUNK, dh).transpose(2, 0, 1, 3, 4)
    bi = jnp.arange(b)[:, None, None, None]
    hi = jnp.arange(h)[None, :, None, None]
    blk_pos = jnp.arange(MOBA_BLOCK)

    def chunk(args):
        qi, ci = args
        start = ci * Q_CHUNK
        q_pos = start + jnp.arange(Q_CHUNK)
        qb = start // MOBA_BLOCK
        own_k = lax.dynamic_index_in_dim(kb, qb, axis=2, keepdims=False)
        own_v = lax.dynamic_index_in_dim(vb, qb, axis=2, keepdims=False)
        own_pos = qb * MOBA_BLOCK + blk_pos
        if k_eff > 0:
            gate = jnp.einsum('bhqd,bhnd->bhqn', qi.astype(jnp.float32), kmean)
            gate = jnp.where(jnp.arange(nb) < qb, gate, -jnp.inf)
            _, idx = lax.top_k(gate, k_eff)
            sel_k = kb[bi, hi, idx].reshape(b, h, Q_CHUNK, k_eff * MOBA_BLOCK, dh)
            sel_v = vb[bi, hi, idx].reshape(b, h, Q_CHUNK, k_eff * MOBA_BLOCK, dh)
            sel_pos = (idx[..., None] * MOBA_BLOCK + blk_pos).reshape(b, h, Q_CHUNK, -1)
            sel_ok = jnp.broadcast_to((idx < qb)[..., None], idx.shape + (MOBA_BLOCK,)).reshape(b, h, Q_CHUNK, -1)
        else:
            sel_k = jnp.zeros((b, h, Q_CHUNK, 0, dh), k.dtype)
            sel_v = jnp.zeros((b, h, Q_CHUNK, 0, dh), v.dtype)
            sel_pos = jnp.zeros((b, h, Q_CHUNK, 0), jnp.int32)
            sel_ok = jnp.zeros((b, h, Q_CHUNK, 0), bool)
        return _moba_softmax(qi, q_pos, sel_k, sel_v, sel_pos, sel_ok, own_k, own_v, own_pos, slopes)

    out = lax.map(chunk, (qc, jnp.arange(n_chunks)))
    return out.transpose(1, 2, 0, 3, 4).reshape(b, h, s, dh)


def _moba_sample(q, k, v, pool_k, pool_v, page_table, slopes):
    db, h, t, dh = q.shape
    ppb = MOBA_BLOCK // PAGE_SIZE
    nbp = PAST_LEN // MOBA_BLOCK
    rp = (PAST_LEN - nbp * MOBA_BLOCK) // PAGE_SIZE
    q_pos = PAST_LEN + jnp.arange(t)
    own_pages = page_table[:, nbp * ppb: nbp * ppb + rp]
    past_k = pool_k[own_pages].transpose(0, 2, 1, 3, 4).reshape(db, h, rp * PAGE_SIZE, dh)
    past_v = pool_v[own_pages].transpose(0, 2, 1, 3, 4).reshape(db, h, rp * PAGE_SIZE, dh)
    own_k = jnp.concatenate([past_k, k], axis=2)
    own_v = jnp.concatenate([past_v, v], axis=2)
    own_pos = jnp.concatenate([nbp * MOBA_BLOCK + jnp.arange(rp * PAGE_SIZE), q_pos])
    k_eff = min(MOBA_TOPK, nbp)
    if k_eff > 0:
        def seq_block_means(pt_row):
            rows = pool_k[pt_row[:nbp * ppb]].reshape(nbp, ppb, h, PAGE_SIZE, dh)
            return jnp.mean(rows, axis=(1, 3), dtype=jnp.float32)
        kmean = lax.map(seq_block_means, page_table).transpose(0, 2, 1, 3)
        gate = jnp.einsum('bhqd,bhnd->bhqn', q.astype(jnp.float32), kmean)
        _, idx = lax.top_k(gate, k_eff)
        logical = idx[..., None] * ppb + jnp.arange(ppb)
        phys = page_table[jnp.arange(db)[:, None, None, None, None], logical]
        hsel = jnp.arange(h)[None, :, None, None, None]
        sel_k = pool_k[phys, hsel].reshape(db, h, t, k_eff * MOBA_BLOCK, dh)
        sel_v = pool_v[phys, hsel].reshape(db, h, t, k_eff * MOBA_BLOCK, dh)
        sel_pos = (idx[..., None] * MOBA_BLOCK + jnp.arange(MOBA_BLOCK)).reshape(db, h, t, -1)
        sel_ok = jnp.ones(sel_pos.shape, bool)
    else:
        sel_k = jnp.zeros((db, h, t, 0, dh), k.dtype)
        sel_v = jnp.zeros((db, h, t, 0, dh), v.dtype)
        sel_pos = jnp.zeros((db, h, t, 0), jnp.int32)
        sel_ok = jnp.zeros((db, h, t, 0), bool)
    return _moba_softmax(q, q_pos, sel_k, sel_v, sel_pos, sel_ok, own_k, own_v, own_pos, slopes)


def _trunk_layer(x, c, mixer, w_ada, b_ada, f1g, f1u, f1d, f2g, f2u, f2d, ln_g, ln_b):
    ada = (jax.nn.silu(c) @ w_ada + b_ada).reshape(c.shape[0], 3, 3, D_MODEL)

    def mod(t, i):
        return t * (1.0 + ada[:, i, 1, None, :]) + ada[:, i, 0, None, :]

    def post(t, y, i):
        return _layernorm(DEEPNORM_ALPHA * t + ada[:, i, 2, None, :] * y, ln_g[i], ln_b[i])

    x = post(x, 0.5 * _swiglu(mod(x, 0), f1g, f1u, f1d), 0)
    m, st = mixer(mod(x, 1))
    x = post(x, m, 1)
    x = post(x, 0.5 * _swiglu(mod(x, 2), f2g, f2u, f2d), 2)
    return x, st


def setup_inputs(seed: int = 0) -> dict:
    key = jax.random.key(seed)
    ks = jax.random.split(key, 32)
    n_pages = PAST_LEN // PAGE_SIZE
    n_pool = (DEC_BATCH * n_pages * 5) // 4
    f32 = jnp.float32

    def nrm(k, shape, s):
        return jax.random.normal(k, shape, f32) * s

    w_in = nrm(ks[8], (DEPTH, D_MODEL, 3 * ATTN_WIDTH + 2 * CONV_CH), D_MODEL ** -0.5)
    w_in = w_in.at[:, :, 2 * ATTN_WIDTH:3 * ATTN_WIDTH].multiply(DEEPNORM_BETA)
    page_table = jax.random.permutation(ks[5], n_pool)[:DEC_BATCH * n_pages].reshape(DEC_BATCH, n_pages).astype(jnp.int32)
    return {
        'x_prompt': nrm(ks[0], (BATCH, SEQ, D_MODEL), 1.0),
        'x_sample': nrm(ks[1], (DEC_BATCH, DEC_SEQ, D_MODEL), 1.0),
        'cache_k': nrm(ks[2], (DEPTH, n_pool, N_HEADS, PAGE_SIZE, HEAD_DIM), 1.0),
        'cache_v': nrm(ks[3], (DEPTH, n_pool, N_HEADS, PAGE_SIZE, HEAD_DIM), 1.0),
        'state_conv': nrm(ks[4], (DEPTH, DEC_BATCH, CONV_K - 1, CONV_CH), 1.0),
        'page_table': page_table,
        'c_prompt': nrm(ks[6], (BATCH, D_MODEL), 1.0),
        'c_sample': nrm(ks[7], (DEC_BATCH, D_MODEL), 1.0),
        'w_ada': nrm(ks[9], (DEPTH, D_MODEL, 9 * D_MODEL), D_MODEL ** -0.5),
        'b_ada': nrm(ks[10], (DEPTH, 9 * D_MODEL), 0.02),
        'ffn1_wg': nrm(ks[11], (DEPTH, D_MODEL, D_FF), D_MODEL ** -0.5),
        'ffn1_wu': nrm(ks[12], (DEPTH, D_MODEL, D_FF), D_MODEL ** -0.5),
        'ffn1_wd': nrm(ks[13], (DEPTH, D_FF, D_MODEL), D_FF ** -0.5 * DEEPNORM_BETA),
        'w_in': w_in,
        'w_dw': nrm(ks[14], (DEPTH, CONV_K, CONV_CH), CONV_K ** -0.5),
        'b_dw': nrm(ks[15], (DEPTH, CONV_CH), 0.02),
        'gn_g': 1.0 + nrm(ks[16], (DEPTH, CONV_CH), 0.05),
        'gn_b': nrm(ks[17], (DEPTH, CONV_CH), 0.02),
        'beta_attn': 1.0 + nrm(ks[18], (DEPTH, ATTN_WIDTH), 0.05),
        'beta_conv': 1.0 + nrm(ks[19], (DEPTH, CONV_CH), 0.05),
        'w_out': nrm(ks[20], (DEPTH, MIX_WIDTH, D_MODEL), MIX_WIDTH ** -0.5 * DEEPNORM_BETA),
        'ffn2_wg': nrm(ks[21], (DEPTH, D_MODEL, D_FF), D_MODEL ** -0.5),
        'ffn2_wu': nrm(ks[22], (DEPTH, D_MODEL, D_FF), D_MODEL ** -0.5),
        'ffn2_wd': nrm(ks[23], (DEPTH, D_FF, D_MODEL), D_FF ** -0.5 * DEEPNORM_BETA),
        'ln_g': 1.0 + nrm(ks[24], (DEPTH, 3, D_MODEL), 0.05),
        'ln_b': nrm(ks[25], (DEPTH, 3, D_MODEL), 0.02),
    }


def reference(x_prompt, x_sample, cache_k, cache_v, state_conv, page_table, c_prompt, c_sample,
              w_ada, b_ada, ffn1_wg, ffn1_wu, ffn1_wd, w_in, w_dw, b_dw, gn_g, gn_b,
              beta_attn, beta_conv, w_out, ffn2_wg, ffn2_wu, ffn2_wd, ln_g, ln_b):
    slopes = _alibi_slopes()
    y_prompt, y_sample = x_prompt, x_sample
    kp_l, vp_l, cp_l, ks_l, vs_l, cs_l = [], [], [], [], [], []
    for l in range(DEPTH):
        def mixer_prompt(h, l=l):
            q, k, v, glu = _project(h, w_in[l])
            attn = _moba_prompt(q, k, v, slopes)
            buf = jnp.pad(glu, ((0, 0), (CONV_K - 1, 0), (0, 0)))
            conv = _conv_branch(buf, w_dw[l], b_dw[l], gn_g[l], gn_b[l])
            return _merge(attn, conv, beta_attn[l], beta_conv[l], w_out[l]), (k, v, glu[:, -(CONV_K - 1):])

        def mixer_sample(h, l=l):
            q, k, v, glu = _project(h, w_in[l])
            attn = _moba_sample(q, k, v, cache_k[l], cache_v[l], page_table, slopes)
            buf = jnp.concatenate([state_conv[l].astype(glu.dtype), glu], axis=1)
            conv = _conv_branch(buf, w_dw[l], b_dw[l], gn_g[l], gn_b[l])
            return _merge(attn, conv, beta_attn[l], beta_conv[l], w_out[l]), (k, v, buf[:, -(CONV_K - 1):])

        y_prompt, (kp, vp, cp) = _trunk_layer(y_prompt, c_prompt, mixer_prompt, w_ada[l], b_ada[l],
                                              ffn1_wg[l], ffn1_wu[l], ffn1_wd[l], ffn2_wg[l], ffn2_wu[l], ffn2_wd[l],
                                              ln_g[l], ln_b[l])
        y_sample, (ksm, vsm, csm) = _trunk_layer(y_sample, c_sample, mixer_sample, w_ada[l], b_ada[l],
                                                 ffn1_wg[l], ffn1_wu[l], ffn1_wd[l], ffn2_wg[l], ffn2_wu[l], ffn2_wd[l],
                                                 ln_g[l], ln_b[l])
        b, h, s, dh = kp.shape
        kp_l.append(kp.reshape(b, h, s // PAGE_SIZE, PAGE_SIZE, dh).transpose(0, 2, 1, 3, 4))
        vp_l.append(vp.reshape(b, h, s // PAGE_SIZE, PAGE_SIZE, dh).transpose(0, 2, 1, 3, 4))
        cp_l.append(cp)
        ks_l.append(ksm)
        vs_l.append(vsm)
        cs_l.append(csm)
    k_prompt_new = jnp.stack(kp_l, 0)
    v_prompt_new = jnp.stack(vp_l, 0)
    conv_prompt_new = jnp.stack(cp_l, 0)
    k_sample_new = jnp.stack(ks_l, 0)
    v_sample_new = jnp.stack(vs_l, 0)
    conv_sample_new = jnp.stack(cs_l, 0)
    return (y_prompt, y_sample, k_prompt_new, v_prompt_new, conv_prompt_new, k_sample_new, v_sample_new, conv_sample_new)
```

```python
import functools

import numpy as np
import jax
import jax.numpy as jnp
from jax import lax
from jax.experimental import pallas as pl
from jax.experimental.pallas import tpu as pltpu

F32 = jnp.float32
BF16 = jnp.bfloat16
HIGHEST = lax.Precision.HIGHEST

D_MODEL = 1024
D_FF = 2816
N_HEADS = 8
HEAD_DIM = 64
ATTN_WIDTH = N_HEADS * HEAD_DIM
CONV_CH = 512
W_IN_COLS = 3 * ATTN_WIDTH + 2 * CONV_CH
CONV_K = 31
HIST = CONV_K - 1
GN_GROUPS = 8
MOBA_BLOCK = 256
MOBA_TOPK = 3
PAGE_SIZE = 128
PAGES_PER_BLOCK = MOBA_BLOCK // PAGE_SIZE
LN_EPS = 1e-5
NEG_INF = -1e30
SCALE = HEAD_DIM ** -0.5

LANES = 128
HEADS_PER_GROUP = LANES // HEAD_DIM
VMEM_LIMIT = 56 * 1024 * 1024

FF_CHUNKS = ((0, 1024), (1024, 2048), (2048, D_FF))


def _silu(x):
    return x / (1.0 + jnp.exp(-x))


def _sigmoid(x):
    return 1.0 / (1.0 + jnp.exp(-x))


def _layernorm(t, g, b):
    mu = jnp.mean(t, axis=-1, keepdims=True)
    d = t - mu
    var = jnp.mean(d * d, axis=-1, keepdims=True)
    return d * lax.rsqrt(var + LN_EPS) * g + b


def _ffn(h, wg_ref, wu_ref, wd_ref):
    acc = None
    for lo, hi in FF_CHUNKS:
        g = jnp.dot(h, wg_ref[:, lo:hi], preferred_element_type=F32)
        u = jnp.dot(h, wu_ref[:, lo:hi], preferred_element_type=F32)
        a = (_silu(g) * u).astype(BF16)
        y = jnp.dot(a, wd_ref[lo:hi, :], preferred_element_type=F32)
        acc = y if acc is None else acc + y
    return acc


def _ada_kernel(c_ref, w_ref, b_ref, o_ref):
    a = _silu(c_ref[...])
    o_ref[...] = jnp.dot(a, w_ref[...], preferred_element_type=F32, precision=HIGHEST) + b_ref[...]


def _ada(c_all, w_ada, b_ada):
    n = c_all.shape[0]
    tn = 1024
    return pl.pallas_call(
        _ada_kernel,
        grid=(w_ada.shape[1] // tn,),
        in_specs=[
            pl.BlockSpec((n, D_MODEL), lambda j: (0, 0)),
            pl.BlockSpec((D_MODEL, tn), lambda j: (0, j)),
            pl.BlockSpec((1, tn), lambda j: (0, j)),
        ],
        out_specs=pl.BlockSpec((n, tn), lambda j: (0, j)),
        out_shape=jax.ShapeDtypeStruct((n, w_ada.shape[1]), F32),
        name="ada",
    )(c_all, w_ada, b_ada.reshape(1, -1))


def _stage_a_kernel(alpha, paged, x_ref, sh0_ref, sc0_ref, g0_ref, sh1_ref, sc1_ref, lng_ref, lnb_ref,
                    wg_ref, wu_ref, wd_ref, win_ref, x1_ref, *out_refs):
    nseq, rows, d = x_ref.shape
    m = nseq * rows
    x = x_ref[...]
    h0 = (x * (1.0 + sc0_ref[...]) + sh0_ref[...]).reshape(m, d).astype(BF16)
    y = _ffn(h0, wg_ref, wu_ref, wd_ref).reshape(nseq, rows, d)
    x1 = _layernorm(alpha * x + g0_ref[...] * (0.5 * y), lng_ref[...], lnb_ref[...])
    x1_ref[...] = x1
    h1 = (x1 * (1.0 + sc1_ref[...]) + sh1_ref[...]).reshape(m, d).astype(BF16)
    u = jnp.dot(h1, win_ref[...], preferred_element_type=F32)
    q = u[:, 0:ATTN_WIDTH]
    k = u[:, ATTN_WIDTH:2 * ATTN_WIDTH]
    v = u[:, 2 * ATTN_WIDTH:3 * ATTN_WIDTH]
    ga = u[:, 3 * ATTN_WIDTH:3 * ATTN_WIDTH + CONV_CH]
    gb = u[:, 3 * ATTN_WIDTH + CONV_CH:]
    glu = ga * _sigmoid(gb)
    if paged:
        qt_ref, k_ref, ktp_ref, vtp_ref, glu_ref = out_refs
        qt_ref[0] = q.T
        k_ref[...] = k
        kt = k.T
        vt = v.T
        for p in range(m // PAGE_SIZE):
            for h in range(N_HEADS):
                rs = slice(h * HEAD_DIM, (h + 1) * HEAD_DIM)
                cs = slice(p * PAGE_SIZE, (p + 1) * PAGE_SIZE)
                ktp_ref[0, p, h] = kt[rs, cs]
                vtp_ref[0, p, h] = vt[rs, cs]
    else:
        q_ref, k_ref, v_ref, glu_ref = out_refs
        q_ref[...] = q
        k_ref[...] = k
        v_ref[...] = v
    glu_ref[...] = glu


def _const_spec(shape):
    return pl.BlockSpec(shape, lambda i, j: (0,) * len(shape), pipeline_mode=pl.Buffered(1))


def _stage_a(x, ada, ln_g, ln_b, wg, wu, wd, win, *, alpha, nseq_blk, rows_blk, paged):
    n_seq, s, _ = x.shape
    tiles_per_seq = s // rows_blk
    assert nseq_blk == 1 or (nseq_blk == n_seq and tiles_per_seq == 1)
    m_blk = nseq_blk * rows_blk
    n_tok = n_seq * s
    x_spec = pl.BlockSpec((nseq_blk, rows_blk, D_MODEL), lambda i, j: (i, j, 0))
    a_spec = pl.BlockSpec((nseq_blk, 1, D_MODEL), lambda i, j: (i, 0, 0))
    flat_spec = pl.BlockSpec((m_blk, ATTN_WIDTH), lambda i, j: (i * tiles_per_seq + j, 0))
    flat_shape = jax.ShapeDtypeStruct((n_tok, ATTN_WIDTH), F32)
    if paged:
        assert nseq_blk == 1 and rows_blk % PAGE_SIZE == 0
        ppt = rows_blk // PAGE_SIZE
        page_spec = pl.BlockSpec((1, ppt, N_HEADS, HEAD_DIM, PAGE_SIZE), lambda i, j: (i, j, 0, 0, 0))
        page_shape = jax.ShapeDtypeStruct((n_seq, s // PAGE_SIZE, N_HEADS, HEAD_DIM, PAGE_SIZE), F32)
        out_specs = [x_spec, pl.BlockSpec((1, ATTN_WIDTH, rows_blk), lambda i, j: (i, 0, j)),
                     flat_spec, page_spec, page_spec, flat_spec]
        out_shape = [jax.ShapeDtypeStruct((n_seq, s, D_MODEL), F32),
                     jax.ShapeDtypeStruct((n_seq, ATTN_WIDTH, s), F32),
                     flat_shape, page_shape, page_shape, flat_shape]
    else:
        out_specs = [x_spec, flat_spec, flat_spec, flat_spec, flat_spec]
        out_shape = [jax.ShapeDtypeStruct((n_seq, s, D_MODEL), F32)] + [flat_shape] * 4
    return pl.pallas_call(
        functools.partial(_stage_a_kernel, alpha, paged),
        grid=(n_seq // nseq_blk, tiles_per_seq),
        in_specs=[x_spec, a_spec, a_spec, a_spec, a_spec, a_spec,
                  _const_spec((1, D_MODEL)), _const_spec((1, D_MODEL)),
                  _const_spec((D_MODEL, D_FF)), _const_spec((D_MODEL, D_FF)),
                  _const_spec((D_FF, D_MODEL)), _const_spec((D_MODEL, W_IN_COLS))],
        out_specs=out_specs,
        out_shape=out_shape,
        compiler_params=pltpu.CompilerParams(
            dimension_semantics=("arbitrary", "arbitrary"), vmem_limit_bytes=VMEM_LIMIT),
        name="stage_a",
    )(x, ada["sh0"], ada["sc0"], ada["g0"], ada["sh1"], ada["sc1"], ln_g, ln_b, wg, wu, wd, win)


def _stage_c_kernel(alpha, x1_ref, attn_ref, conv_ref, g1_ref, sh2_ref, sc2_ref, g2_ref,
                    lng1_ref, lnb1_ref, lng2_ref, lnb2_ref, ba_ref, bc_ref, wo_ref,
                    wg_ref, wu_ref, wd_ref, y_ref):
    nseq, rows, d = x1_ref.shape
    m = nseq * rows
    a = (attn_ref[...] * ba_ref[...]).astype(BF16)
    c = (conv_ref[...] * bc_ref[...]).astype(BF16)
    mix = (jnp.dot(a, wo_ref[0:ATTN_WIDTH, :], preferred_element_type=F32)
           + jnp.dot(c, wo_ref[ATTN_WIDTH:, :], preferred_element_type=F32)).reshape(nseq, rows, d)
    x2 = _layernorm(alpha * x1_ref[...] + g1_ref[...] * mix, lng1_ref[...], lnb1_ref[...])
    h2 = (x2 * (1.0 + sc2_ref[...]) + sh2_ref[...]).reshape(m, d).astype(BF16)
    y = _ffn(h2, wg_ref, wu_ref, wd_ref).reshape(nseq, rows, d)
    y_ref[...] = _layernorm(alpha * x2 + g2_ref[...] * (0.5 * y), lng2_ref[...], lnb2_ref[...])


def _stage_c(x1, attn, conv, ada, ln_g1, ln_b1, ln_g2, ln_b2, beta_a, beta_c, wo, wg, wu, wd,
             *, alpha, nseq_blk, rows_blk):
    n_seq, s, _ = x1.shape
    tiles_per_seq = s // rows_blk
    assert nseq_blk == 1 or (nseq_blk == n_seq and tiles_per_seq == 1)
    m_blk = nseq_blk * rows_blk
    x_spec = pl.BlockSpec((nseq_blk, rows_blk, D_MODEL), lambda i, j: (i, j, 0))
    a_spec = pl.BlockSpec((nseq_blk, 1, D_MODEL), lambda i, j: (i, 0, 0))
    flat = pl.BlockSpec((m_blk, ATTN_WIDTH), lambda i, j: (i * tiles_per_seq + j, 0))
    return pl.pallas_call(
        functools.partial(_stage_c_kernel, alpha),
        grid=(n_seq // nseq_blk, tiles_per_seq),
        in_specs=[x_spec, flat, flat, a_spec, a_spec, a_spec, a_spec,
                  _const_spec((1, D_MODEL)), _const_spec((1, D_MODEL)),
                  _const_spec((1, D_MODEL)), _const_spec((1, D_MODEL)),
                  _const_spec((1, ATTN_WIDTH)), _const_spec((1, CONV_CH)),
                  _const_spec((D_MODEL, D_MODEL)),
                  _const_spec((D_MODEL, D_FF)), _const_spec((D_MODEL, D_FF)),
                  _const_spec((D_FF, D_MODEL))],
        out_specs=x_spec,
        out_shape=jax.ShapeDtypeStruct((n_seq, s, D_MODEL), F32),
        compiler_params=pltpu.CompilerParams(
            dimension_semantics=("arbitrary", "arbitrary"), vmem_limit_bytes=VMEM_LIMIT),
        name="stage_c",
    )(x1, attn, conv, ada["g1"], ada["sh2"], ada["sc2"], ada["g2"],
      ln_g1, ln_b1, ln_g2, ln_b2, beta_a, beta_c, wo, wg, wu, wd)


def _conv_kernel(n_chunks, hist_ref, prev_ref, cur_ref, w_ref, bdw_ref, gng_ref, gnb_ref, gavg_ref,
                 o_ref, buf_ref):
    nb, ch, c = cur_ref.shape
    if n_chunks == 1:
        halo = hist_ref[...]
    else:
        halo = jnp.where(pl.program_id(1) == 0, hist_ref[...], prev_ref[:, ch - HIST:, :])
    buf_ref[:, 2:2 + HIST, :] = halo
    buf_ref[:, 2 + HIST:, :] = cur_ref[...]
    acc = jnp.zeros((nb, ch, c), F32)
    for j in range(CONV_K):
        acc = acc + buf_ref[:, 2 + j:2 + j + ch, :] * w_ref[j:j + 1, :]
    y = (acc + bdw_ref[...]).reshape(nb * ch, c)
    gavg = gavg_ref[...]

    def group_mean(t):
        hi = t.astype(BF16)
        lo = (t - hi.astype(F32)).astype(BF16)
        return (jnp.dot(hi, gavg, preferred_element_type=F32)
                + jnp.dot(lo, gavg, preferred_element_type=F32))

    mu = group_mean(y)
    dlt = y - mu
    var = group_mean(dlt * dlt)
    z = dlt * lax.rsqrt(var + LN_EPS) * gng_ref[...] + gnb_ref[...]
    o_ref[...] = _silu(z).reshape(nb, ch, c)


def _conv_branch(hist, glu3, w_dw, b_dw, gn_g, gn_b, *, nb, ch):
    n_seq, s, c = glu3.shape
    n_chunks = s // ch
    assert ch >= HIST or n_chunks == 1
    grp = np.arange(c) // (c // GN_GROUPS)
    gavg = jnp.asarray((grp[:, None] == grp[None, :]).astype(np.float32) / (c // GN_GROUPS), BF16)
    cur_spec = pl.BlockSpec((nb, ch, c), lambda i, j: (i, j, 0))
    prev_spec = pl.BlockSpec((nb, ch, c), lambda i, j: (i, jnp.maximum(j - 1, 0), 0))
    return pl.pallas_call(
        functools.partial(_conv_kernel, n_chunks),
        grid=(n_seq // nb, n_chunks),
        in_specs=[pl.BlockSpec((nb, HIST, c), lambda i, j: (i, 0, 0)), prev_spec, cur_spec,
                  pl.BlockSpec((CONV_K, c), lambda i, j: (0, 0)),
                  pl.BlockSpec((1, c), lambda i, j: (0, 0)),
                  pl.BlockSpec((1, c), lambda i, j: (0, 0)),
                  pl.BlockSpec((1, c), lambda i, j: (0, 0)),
                  pl.BlockSpec((c, c), lambda i, j: (0, 0))],
        out_specs=cur_spec,
        out_shape=jax.ShapeDtypeStruct((n_seq, s, c), F32),
        scratch_shapes=[pltpu.VMEM((nb, ch + HIST + 2, c), F32)],
        compiler_params=pltpu.CompilerParams(dimension_semantics=("arbitrary", "arbitrary")),
        name="conv_branch",
    )(hist, glu3, glu3, w_dw, b_dw.reshape(1, c), gn_g.reshape(1, c), gn_b.reshape(1, c), gavg)


def _moba_prompt_kernel(slopes_ref, qt_ref, k_ref, vtp_ref, o_ref, kmean_ref, sel_ref):
    hp = pl.program_id(1)
    qb = pl.program_id(2)
    blk = MOBA_BLOCK
    nb = k_ref.shape[1] // blk

    @pl.when(qb == 0)
    def _():
        for n in range(nb):
            kmean_ref[n:n + 1, :] = jnp.sum(k_ref[0, n * blk:(n + 1) * blk, :], axis=0,
                                            keepdims=True) * (1.0 / blk)

    qt = qt_ref[0] * SCALE
    row_head = lax.broadcasted_iota(jnp.int32, (LANES, blk), 0) // HEAD_DIM
    key_r = lax.broadcasted_iota(jnp.int32, (blk, blk), 0)
    qry_c = lax.broadcasted_iota(jnp.int32, (blk, blk), 1)
    blk_id = lax.broadcasted_iota(jnp.int32, (nb, blk), 0)
    past = blk_id < qb
    outs = []
    for hh in range(HEADS_PER_GROUP):
        slope = slopes_ref[hp * HEADS_PER_GROUP + hh]
        qh_t = jnp.where(row_head == hh, qt, 0.0)
        qh_t16 = qh_t.astype(BF16)

        gate = jnp.dot(kmean_ref[...], qh_t, preferred_element_type=F32, precision=HIGHEST)
        for n in range(nb):
            row = gate[n:n + 1, :]
            beats = ((gate > row) | ((gate == row) & (blk_id < n))) & past
            cnt = jnp.sum(jnp.where(beats, 1.0, 0.0), axis=0, keepdims=True)
            sel_ref[n:n + 1, :] = jnp.where((cnt < MOBA_TOPK) & (n < qb), 1.0, 0.0)

        def scores(n):
            kb = k_ref[0, pl.ds(pl.multiple_of(n * blk, blk), blk), :].astype(BF16)
            s = jnp.dot(kb, qh_t16, preferred_element_type=F32)
            return s + slope * (key_r + (n - qb) * blk).astype(F32)

        def pv(n, p):
            p16 = p.astype(BF16)
            acc = None
            for i in range(PAGES_PER_BLOCK):
                vt = vtp_ref[0, n * PAGES_PER_BLOCK + i, hh].astype(BF16)
                part = jnp.dot(vt, p16[i * PAGE_SIZE:(i + 1) * PAGE_SIZE, :],
                               preferred_element_type=F32)
                acc = part if acc is None else acc + part
            return acc

        s = jnp.where(key_r <= qry_c, scores(qb), NEG_INF)
        m0 = jnp.max(s, axis=0, keepdims=True)
        p = jnp.exp(s - m0)
        l0 = jnp.sum(p, axis=0, keepdims=True)
        acc0 = pv(qb, p)

        def body(n, carry):
            m_prev, l_prev, acc_prev = carry
            s = jnp.where(sel_ref[pl.ds(n, 1), :] > 0.5, scores(n), NEG_INF)
            m_new = jnp.maximum(m_prev, jnp.max(s, axis=0, keepdims=True))
            alpha = jnp.exp(m_prev - m_new)
            p = jnp.exp(s - m_new)
            l_new = alpha * l_prev + jnp.sum(p, axis=0, keepdims=True)
            return m_new, l_new, alpha * acc_prev + pv(n, p)

        _, l_fin, acc_fin = lax.fori_loop(0, qb, body, (m0, l0, acc0))
        outs.append(acc_fin / l_fin)
    o_ref[0] = jnp.concatenate(outs, axis=0).T


def _moba_prompt(qt3, k3, vtp, slopes):
    b, s, _ = k3.shape
    n_groups = ATTN_WIDTH // LANES
    n_blocks = s // MOBA_BLOCK
    return pl.pallas_call(
        _moba_prompt_kernel,
        grid=(b, n_groups, n_blocks),
        in_specs=[pl.BlockSpec(memory_space=pltpu.SMEM),
                  pl.BlockSpec((1, LANES, MOBA_BLOCK), lambda i, g, j: (i, g, j)),
                  pl.BlockSpec((1, s, LANES), lambda i, g, j: (i, 0, g)),
                  pl.BlockSpec((1, s // PAGE_SIZE, HEADS_PER_GROUP, HEAD_DIM, PAGE_SIZE),
                               lambda i, g, j: (i, 0, g, 0, 0))],
        out_specs=pl.BlockSpec((1, MOBA_BLOCK, LANES), lambda i, g, j: (i, j, g)),
        out_shape=jax.ShapeDtypeStruct((b, s, ATTN_WIDTH), F32),
        scratch_shapes=[pltpu.VMEM((n_blocks, LANES), F32),
                        pltpu.VMEM((n_blocks, MOBA_BLOCK), F32)],
        compiler_params=pltpu.CompilerParams(
            dimension_semantics=("arbitrary", "arbitrary", "arbitrary")),
        name="moba_prompt",
    )(slopes, qt3, k3, vtp)


PAGES_PER_STEP = 16
BLOCKS_PER_STEP = PAGES_PER_STEP // PAGES_PER_BLOCK


def _block_mean_kernel(pt_ref, *refs):
    page_refs, o_ref = refs[:-1], refs[-1]
    inv = 1.0 / MOBA_BLOCK
    lane = lax.broadcasted_iota(jnp.int32, (HEAD_DIM, BLOCKS_PER_STEP), 1)
    for h in range(N_HEADS):
        acc = jnp.zeros((HEAD_DIM, BLOCKS_PER_STEP), F32)
        for jb in range(BLOCKS_PER_STEP):
            tot = page_refs[jb * PAGES_PER_BLOCK][h]
            for i in range(1, PAGES_PER_BLOCK):
                tot = tot + page_refs[jb * PAGES_PER_BLOCK + i][h]
            acc = jnp.where(lane == jb, jnp.sum(tot, axis=1, keepdims=True) * inv, acc)
        o_ref[h] = acc


def _block_means(pool_kt, page_table_flat, n_seq, n_pages):
    steps = n_pages // PAGES_PER_STEP

    def page_spec(i):
        return pl.BlockSpec(
            (None, N_HEADS, HEAD_DIM, PAGE_SIZE),
            lambda b, c, pt: (pt[b * n_pages + c * PAGES_PER_STEP + i], 0, 0, 0))

    return pl.pallas_call(
        _block_mean_kernel,
        grid_spec=pltpu.PrefetchScalarGridSpec(
            num_scalar_prefetch=1,
            grid=(n_seq, steps),
            in_specs=[page_spec(i) for i in range(PAGES_PER_STEP)],
            out_specs=pl.BlockSpec((None, N_HEADS, None, HEAD_DIM, BLOCKS_PER_STEP),
                                   lambda b, c, pt: (b, 0, c, 0, 0)),
        ),
        out_shape=jax.ShapeDtypeStruct((n_seq, N_HEADS, steps, HEAD_DIM, BLOCKS_PER_STEP), F32),
        compiler_params=pltpu.CompilerParams(
            dimension_semantics=("arbitrary", "arbitrary"), vmem_limit_bytes=VMEM_LIMIT),
        name="block_means",
    )(page_table_flat, *([pool_kt] * PAGES_PER_STEP))


def _topk_kernel(q_ref, kmt_ref, o_ref):
    t, nbp = q_ref.shape[0], kmt_ref.shape[2]
    q = q_ref[...]
    lane = lax.broadcasted_iota(jnp.int32, (t, nbp), 1).astype(F32)
    out_lane = lax.broadcasted_iota(jnp.int32, (t, LANES), 1)
    for h in range(N_HEADS):
        g = jnp.dot(q[:, h * HEAD_DIM:(h + 1) * HEAD_DIM], kmt_ref[h],
                    preferred_element_type=F32, precision=HIGHEST)
        res = jnp.zeros((t, LANES), F32)
        for r in range(MOBA_TOPK):
            mx = jnp.max(g, axis=1, keepdims=True)
            idx = jnp.min(jnp.where(g == mx, lane, float(nbp)), axis=1, keepdims=True)
            res = jnp.where(out_lane == r, idx, res)
            g = jnp.where(lane == idx, -jnp.inf, g)
        o_ref[h] = res.astype(jnp.int32)


def _topk_blocks(q3, kmean_t):
    n_seq, t, _ = q3.shape
    nbp = kmean_t.shape[3]
    return pl.pallas_call(
        _topk_kernel,
        grid=(n_seq,),
        in_specs=[pl.BlockSpec((None, t, ATTN_WIDTH), lambda b: (b, 0, 0)),
                  pl.BlockSpec((None, N_HEADS, HEAD_DIM, nbp), lambda b: (b, 0, 0, 0))],
        out_specs=pl.BlockSpec((None, N_HEADS, t, LANES), lambda b: (b, 0, 0, 0)),
        out_shape=jax.ShapeDtypeStruct((n_seq, N_HEADS, t, LANES), jnp.int32),
        name="topk_blocks",
    )(q3, kmean_t)


def _moba_sample_kernel(past_len, n_k, idx_ref, pt_ref, slopes_ref, qt_ref, knt_ref, vnt_ref, *refs):
    k_refs, v_refs, o_ref = refs[:n_k], refs[n_k:2 * n_k], refs[-1]
    b = pl.program_id(0)
    h = pl.program_id(1)
    t = qt_ref.shape[1]
    slabs = MOBA_TOPK * PAGES_PER_BLOCK
    slope = slopes_ref[h]
    qt = qt_ref[...] * SCALE
    knt = knt_ref[...]
    vnt = vnt_ref[...]
    srow = lax.broadcasted_iota(jnp.int32, (slabs, PAGE_SIZE), 0)
    lane = lax.broadcasted_iota(jnp.int32, (slabs, PAGE_SIZE), 1)
    own_pos = lax.broadcasted_iota(jnp.int32, (1, t), 1)
    out_lane = lax.broadcasted_iota(jnp.int32, (HEAD_DIM, t), 1)
    out = jnp.zeros((HEAD_DIM, t), F32)
    for tq in range(t):
        q_col = qt[:, tq:tq + 1]
        base = ((b * N_HEADS + h) * t + tq) * MOBA_TOPK
        s_rows = []
        blk_of_row = jnp.zeros((slabs, PAGE_SIZE), jnp.int32)
        for j in range(MOBA_TOPK):
            blk_of_row = jnp.where(srow // PAGES_PER_BLOCK == j, idx_ref[base + j], blk_of_row)
            for i in range(PAGES_PER_BLOCK):
                kt = k_refs[(tq * MOBA_TOPK + j) * PAGES_PER_BLOCK + i][...]
                s_rows.append(jnp.sum(kt * q_col, axis=0, keepdims=True))
        s_sel = jnp.concatenate(s_rows, axis=0)
        k_pos = blk_of_row * MOBA_BLOCK + (srow % PAGES_PER_BLOCK) * PAGE_SIZE + lane
        s_sel = s_sel - slope * ((past_len + tq) - k_pos).astype(F32)
        rel_own = tq - own_pos
        s_own = jnp.sum(knt * q_col, axis=0, keepdims=True)
        s_own = jnp.where(rel_own >= 0, s_own - slope * rel_own.astype(F32), NEG_INF)
        m = jnp.maximum(jnp.max(jnp.max(s_sel, axis=1, keepdims=True), axis=0, keepdims=True),
                        jnp.max(s_own, axis=1, keepdims=True))
        p_sel = jnp.exp(s_sel - m)
        p_own = jnp.exp(s_own - m)
        l = (jnp.sum(jnp.sum(p_sel, axis=1, keepdims=True), axis=0, keepdims=True)
             + jnp.sum(p_own, axis=1, keepdims=True))
        acc = jnp.zeros((HEAD_DIM, PAGE_SIZE), F32)
        for r in range(slabs):
            acc = acc + v_refs[tq * slabs + r][...] * p_sel[r:r + 1, :]
        o_col = (jnp.sum(acc, axis=1, keepdims=True) + jnp.sum(vnt * p_own, axis=1, keepdims=True)) / l
        out = jnp.where(out_lane == tq, o_col, out)
    o_ref[...] = out


def _moba_sample(qt4, knt4, vnt4, pool_kt, pool_vt, idx_flat, page_table_flat, slopes, *, past_len, n_pages):
    n_seq, _, _, t = qt4.shape
    n_k = t * MOBA_TOPK * PAGES_PER_BLOCK

    def page_spec(n):
        tq, rem = divmod(n, MOBA_TOPK * PAGES_PER_BLOCK)
        j, i = divmod(rem, PAGES_PER_BLOCK)

        def index_map(b, h, idx, pt):
            blk_idx = idx[((b * N_HEADS + h) * t + tq) * MOBA_TOPK + j]
            return (pt[b * n_pages + blk_idx * PAGES_PER_BLOCK + i], h, 0, 0)

        return pl.BlockSpec((None, None, HEAD_DIM, PAGE_SIZE), index_map)

    head_spec = pl.BlockSpec((None, None, HEAD_DIM, t), lambda b, h, idx, pt: (b, h, 0, 0))
    return pl.pallas_call(
        functools.partial(_moba_sample_kernel, past_len, n_k),
        grid_spec=pltpu.PrefetchScalarGridSpec(
            num_scalar_prefetch=2,
            grid=(n_seq, N_HEADS),
            in_specs=[pl.BlockSpec(memory_space=pltpu.SMEM), head_spec, head_spec, head_spec]
            + [page_spec(n) for n in range(n_k)] * 2,
            out_specs=head_spec,
        ),
        out_shape=jax.ShapeDtypeStruct((n_seq, N_HEADS, HEAD_DIM, t), F32),
        compiler_params=pltpu.CompilerParams(dimension_semantics=("arbitrary", "arbitrary")),
        name="moba_sample",
    )(idx_flat, page_table_flat, slopes, qt4, knt4, vnt4, *([pool_kt] * n_k), *([pool_vt] * n_k))


def _split_ada(ada_rows):
    a = ada_rows.reshape(ada_rows.shape[0], 3, 3, 1, D_MODEL)
    return {"sh0": a[:, 0, 0], "sc0": a[:, 0, 1], "g0": a[:, 0, 2],
            "sh1": a[:, 1, 0], "sc1": a[:, 1, 1], "g1": a[:, 1, 2],
            "sh2": a[:, 2, 0], "sc2": a[:, 2, 1], "g2": a[:, 2, 2]}


def _heads(t2, n_seq, s):
    return t2.reshape(n_seq, s, N_HEADS, HEAD_DIM)


def kernel(x_prompt, x_sample, cache_k, cache_v, state_conv, page_table, c_prompt, c_sample,
           w_ada, b_ada, ffn1_wg, ffn1_wu, ffn1_wd, w_in, w_dw, b_dw, gn_g, gn_b,
           beta_attn, beta_conv, w_out, ffn2_wg, ffn2_wu, ffn2_wd, ln_g, ln_b):
    depth = w_ada.shape[0]
    alpha = (2.0 * depth) ** 0.25
    batch, seq, _ = x_prompt.shape
    dec_batch, dec_seq, _ = x_sample.shape
    n_pages = page_table.shape[1]
    past_len = n_pages * PAGE_SIZE
    assert past_len % MOBA_BLOCK == 0 and dec_seq <= MOBA_BLOCK
    slopes = jnp.asarray(2.0 ** (-8.0 * np.arange(1, N_HEADS + 1) / N_HEADS), F32)
    page_table_flat = page_table.reshape(-1)

    y_p, y_s = x_prompt, x_sample
    outs = {name: [] for name in ("kp", "vp", "cp", "ks", "vs", "cs")}
    for l in range(depth):
        ada = _ada(jnp.concatenate([c_prompt, c_sample], axis=0), w_ada[l], b_ada[l])
        ada_p, ada_s = _split_ada(ada[:batch]), _split_ada(ada[batch:])
        lng = [ln_g[l, i].reshape(1, D_MODEL) for i in range(3)]
        lnb = [ln_b[l, i].reshape(1, D_MODEL) for i in range(3)]
        w1 = (ffn1_wg[l].astype(BF16), ffn1_wu[l].astype(BF16), ffn1_wd[l].astype(BF16))
        w2 = (ffn2_wg[l].astype(BF16), ffn2_wu[l].astype(BF16), ffn2_wd[l].astype(BF16))
        win16, wo16 = w_in[l].astype(BF16), w_out[l].astype(BF16)
        beta_a = beta_attn[l].reshape(1, ATTN_WIDTH)
        beta_c = beta_conv[l].reshape(1, CONV_CH)

        tile_p = dict(alpha=alpha, nseq_blk=1, rows_blk=512)
        x1, qt, k, ktp, vtp, glu = _stage_a(y_p, ada_p, lng[0], lnb[0], *w1, win16, paged=True, **tile_p)
        attn = _moba_prompt(qt, k.reshape(batch, seq, ATTN_WIDTH), vtp, slopes)
        glu3 = glu.reshape(batch, seq, CONV_CH)
        conv = _conv_branch(jnp.zeros((batch, HIST, CONV_CH), F32), glu3, w_dw[l], b_dw[l],
                            gn_g[l], gn_b[l], nb=1, ch=256)
        y_p = _stage_c(x1, attn.reshape(batch * seq, ATTN_WIDTH), conv.reshape(batch * seq, CONV_CH),
                       ada_p, lng[1], lnb[1], lng[2], lnb[2], beta_a, beta_c, wo16, *w2, **tile_p)
        outs["kp"].append(jnp.swapaxes(ktp, -1, -2))
        outs["vp"].append(jnp.swapaxes(vtp, -1, -2))
        outs["cp"].append(glu3[:, seq - HIST:, :])

        tile_s = dict(alpha=alpha, nseq_blk=dec_batch, rows_blk=dec_seq)
        x1, q, k, v, glu = _stage_a(y_s, ada_s, lng[0], lnb[0], *w1, win16, paged=False, **tile_s)
        q4, k4, v4 = (_heads(t2, dec_batch, dec_seq) for t2 in (q, k, v))
        pool_kt = jnp.swapaxes(cache_k[l], -1, -2)
        pool_vt = jnp.swapaxes(cache_v[l], -1, -2)
        kmean5 = _block_means(pool_kt, page_table_flat, dec_batch, n_pages)
        kmean_t = kmean5.transpose(0, 1, 3, 2, 4).reshape(dec_batch, N_HEADS, HEAD_DIM, -1)
        idx = _topk_blocks(q.reshape(dec_batch, dec_seq, ATTN_WIDTH), kmean_t)[..., :MOBA_TOPK]
        to_t = lambda a4: a4.transpose(0, 2, 3, 1)
        attn_t = _moba_sample(to_t(q4), to_t(k4), to_t(v4), pool_kt, pool_vt, idx.reshape(-1),
                              page_table_flat, slopes, past_len=past_len, n_pages=n_pages)
        attn = attn_t.transpose(0, 3, 1, 2).reshape(dec_batch * dec_seq, ATTN_WIDTH)
        glu3 = glu.reshape(dec_batch, dec_seq, CONV_CH)
        hist = state_conv[l]
        conv = _conv_branch(hist, glu3, w_dw[l], b_dw[l], gn_g[l], gn_b[l], nb=dec_batch, ch=dec_seq)
        y_s = _stage_c(x1, attn, conv.reshape(dec_batch * dec_seq, CONV_CH),
                       ada_s, lng[1], lnb[1], lng[2], lnb[2], beta_a, beta_c, wo16, *w2, **tile_s)
        outs["ks"].append(k4.transpose(0, 2, 1, 3))
        outs["vs"].append(v4.transpose(0, 2, 1, 3))
        outs["cs"].append(jnp.concatenate([hist, glu3], axis=1)[:, dec_seq:, :])

    stack = lambda name: jnp.stack(outs[name], 0)
    return (y_p, y_s, stack("kp"), stack("vp"), stack("cp"), stack("ks"), stack("vs"), stack("cs"))
```

```python
import functools

import numpy as np
import jax
import jax.numpy as jnp
from jax import lax
from jax.experimental import pallas as pl
from jax.experimental.pallas import tpu as pltpu

F32 = jnp.float32
BF16 = jnp.bfloat16
HIGHEST = lax.Precision.HIGHEST

D_MODEL = 1024
D_FF = 2816
N_HEADS = 8
HEAD_DIM = 64
ATTN_WIDTH = N_HEADS * HEAD_DIM
CONV_CH = 512
W_IN_COLS = 3 * ATTN_WIDTH + 2 * CONV_CH
CONV_K = 31
HIST = CONV_K - 1
GN_GROUPS = 8
MOBA_BLOCK = 256
MOBA_TOPK = 3
PAGE_SIZE = 128
PAGES_PER_BLOCK = MOBA_BLOCK // PAGE_SIZE
LN_EPS = 1e-5
NEG_INF = -1e30
SCALE = HEAD_DIM ** -0.5

LANES = 128
SUBLANES = 8
HEADS_PER_GROUP = LANES // HEAD_DIM
CONV_PAD_ROWS = 32
LOG2E = 1.4426950408889634
VMEM_LIMIT = 56 * 1024 * 1024

FF_CHUNKS = ((0, 1024), (1024, 2048), (2048, D_FF))


def _silu(x):
    return x / (1.0 + jnp.exp(-x))


def _sigmoid(x):
    return 1.0 / (1.0 + jnp.exp(-x))


def _layernorm(t, g, b):
    mu = jnp.mean(t, axis=-1, keepdims=True)
    d = t - mu
    var = jnp.mean(d * d, axis=-1, keepdims=True)
    return d * lax.rsqrt(var + LN_EPS) * g + b


def _ffn(h, wg_ref, wu_ref, wd_ref):
    acc = None
    for lo, hi in FF_CHUNKS:
        g = jnp.dot(h, wg_ref[:, lo:hi], preferred_element_type=F32)
        u = jnp.dot(h, wu_ref[:, lo:hi], preferred_element_type=F32)
        a = (_silu(g) * u).astype(BF16)
        y = jnp.dot(a, wd_ref[lo:hi, :], preferred_element_type=F32)
        acc = y if acc is None else acc + y
    return acc


def _ada_kernel(c_ref, w_ref, b_ref, o_ref):
    a = _silu(c_ref[...])
    o_ref[...] = jnp.dot(a, w_ref[...], preferred_element_type=F32, precision=HIGHEST) + b_ref[...]


def _ada(c_all, w_ada, b_ada):
    n = c_all.shape[0]
    tn = 1024
    return pl.pallas_call(
        _ada_kernel,
        grid=(w_ada.shape[1] // tn,),
        in_specs=[
            pl.BlockSpec((n, D_MODEL), lambda j: (0, 0)),
            pl.BlockSpec((D_MODEL, tn), lambda j: (0, j)),
            pl.BlockSpec((1, tn), lambda j: (0, j)),
        ],
        out_specs=pl.BlockSpec((n, tn), lambda j: (0, j)),
        out_shape=jax.ShapeDtypeStruct((n, w_ada.shape[1]), F32),
        name="ada",
    )(c_all, w_ada, b_ada.reshape(1, -1))


def _stage_a_kernel(alpha, paged, x_ref, sh0_ref, sc0_ref, g0_ref, sh1_ref, sc1_ref, lng_ref, lnb_ref,
                    wg_ref, wu_ref, wd_ref, win_ref, x1_ref, *out_refs):
    nseq, rows, d = x_ref.shape
    m = nseq * rows
    x = x_ref[...]
    h0 = (x * (1.0 + sc0_ref[...]) + sh0_ref[...]).reshape(m, d).astype(BF16)
    y = _ffn(h0, wg_ref, wu_ref, wd_ref).reshape(nseq, rows, d)
    x1 = _layernorm(alpha * x + g0_ref[...] * (0.5 * y), lng_ref[...], lnb_ref[...])
    x1_ref[...] = x1
    h1 = (x1 * (1.0 + sc1_ref[...]) + sh1_ref[...]).reshape(m, d).astype(BF16)
    u = jnp.dot(h1, win_ref[...], preferred_element_type=F32)
    q = u[:, 0:ATTN_WIDTH]
    k = u[:, ATTN_WIDTH:2 * ATTN_WIDTH]
    v = u[:, 2 * ATTN_WIDTH:3 * ATTN_WIDTH]
    ga = u[:, 3 * ATTN_WIDTH:3 * ATTN_WIDTH + CONV_CH]
    gb = u[:, 3 * ATTN_WIDTH + CONV_CH:]
    glu = ga * _sigmoid(gb)
    if paged:
        qt_ref, k16_ref, kmean_ref, ktp_ref, vtp_ref, vt16_ref, glu_ref = out_refs
        qt_ref[0] = q.T
        k16_ref[...] = k.astype(BF16)
        blocks_per_tile = m // MOBA_BLOCK
        means = [jnp.sum(k[n * MOBA_BLOCK:(n + 1) * MOBA_BLOCK, :], axis=0, keepdims=True)
                 * (1.0 / MOBA_BLOCK) for n in range(blocks_per_tile)]
        tile = pl.program_id(1)
        for jj in range(kmean_ref.shape[1] // blocks_per_tile):
            @pl.when(tile == jj)
            def _():
                for n in range(blocks_per_tile):
                    kmean_ref[0, jj * blocks_per_tile + n:jj * blocks_per_tile + n + 1, :] = means[n]
        kt = k.T
        vt = v.T
        for p in range(m // PAGE_SIZE):
            for h in range(N_HEADS):
                rs = slice(h * HEAD_DIM, (h + 1) * HEAD_DIM)
                cs = slice(p * PAGE_SIZE, (p + 1) * PAGE_SIZE)
                ktp_ref[0, p, h] = kt[rs, cs]
                vtp_ref[0, p, h] = vt[rs, cs]
                vt16_ref[0, p, h] = vt[rs, cs].astype(BF16)
    else:
        q_ref, k_ref, v_ref, glu_ref = out_refs
        q_ref[...] = q
        k_ref[...] = k
        v_ref[...] = v
    glu_ref[...] = glu


def _const_spec(shape):
    return pl.BlockSpec(shape, lambda i, j: (0,) * len(shape), pipeline_mode=pl.Buffered(1))


def _stage_a(x, ada, ln_g, ln_b, wg, wu, wd, win, *, alpha, nseq_blk, rows_blk, paged):
    n_seq, s, _ = x.shape
    tiles_per_seq = s // rows_blk
    assert nseq_blk == 1 or (nseq_blk == n_seq and tiles_per_seq == 1)
    m_blk = nseq_blk * rows_blk
    n_tok = n_seq * s
    x_spec = pl.BlockSpec((nseq_blk, rows_blk, D_MODEL), lambda i, j: (i, j, 0))
    a_spec = pl.BlockSpec((nseq_blk, 1, D_MODEL), lambda i, j: (i, 0, 0))
    flat_spec = pl.BlockSpec((m_blk, ATTN_WIDTH), lambda i, j: (i * tiles_per_seq + j, 0))
    flat_shape = jax.ShapeDtypeStruct((n_tok, ATTN_WIDTH), F32)
    if paged:
        assert nseq_blk == 1 and rows_blk % MOBA_BLOCK == 0 and MOBA_BLOCK % PAGE_SIZE == 0
        ppt = rows_blk // PAGE_SIZE
        page_dims = (n_seq, s // PAGE_SIZE, N_HEADS, HEAD_DIM, PAGE_SIZE)
        page_spec = pl.BlockSpec((1, ppt, N_HEADS, HEAD_DIM, PAGE_SIZE), lambda i, j: (i, j, 0, 0, 0))
        page_shape = jax.ShapeDtypeStruct(page_dims, F32)
        out_specs = [x_spec, pl.BlockSpec((1, ATTN_WIDTH, rows_blk), lambda i, j: (i, 0, j)),
                     flat_spec,
                     pl.BlockSpec((1, s // MOBA_BLOCK, ATTN_WIDTH), lambda i, j: (i, 0, 0)),
                     page_spec, page_spec, page_spec, flat_spec]
        out_shape = [jax.ShapeDtypeStruct((n_seq, s, D_MODEL), F32),
                     jax.ShapeDtypeStruct((n_seq, ATTN_WIDTH, s), F32),
                     jax.ShapeDtypeStruct((n_tok, ATTN_WIDTH), BF16),
                     jax.ShapeDtypeStruct((n_seq, s // MOBA_BLOCK, ATTN_WIDTH), F32),
                     page_shape, page_shape, jax.ShapeDtypeStruct(page_dims, BF16), flat_shape]
    else:
        out_specs = [x_spec, flat_spec, flat_spec, flat_spec, flat_spec]
        out_shape = [jax.ShapeDtypeStruct((n_seq, s, D_MODEL), F32)] + [flat_shape] * 4
    return pl.pallas_call(
        functools.partial(_stage_a_kernel, alpha, paged),
        grid=(n_seq // nseq_blk, tiles_per_seq),
        in_specs=[x_spec, a_spec, a_spec, a_spec, a_spec, a_spec,
                  _const_spec((1, D_MODEL)), _const_spec((1, D_MODEL)),
                  _const_spec((D_MODEL, D_FF)), _const_spec((D_MODEL, D_FF)),
                  _const_spec((D_FF, D_MODEL)), _const_spec((D_MODEL, W_IN_COLS))],
        out_specs=out_specs,
        out_shape=out_shape,
        compiler_params=pltpu.CompilerParams(
            dimension_semantics=("arbitrary", "arbitrary"), vmem_limit_bytes=VMEM_LIMIT),
        name="stage_a",
    )(x, ada["sh0"], ada["sc0"], ada["g0"], ada["sh1"], ada["sc1"], ln_g, ln_b, wg, wu, wd, win)


def _stage_c_kernel(alpha, x1_ref, attn_ref, conv_ref, g1_ref, sh2_ref, sc2_ref, g2_ref,
                    lng1_ref, lnb1_ref, lng2_ref, lnb2_ref, ba_ref, bc_ref, wo_ref,
                    wg_ref, wu_ref, wd_ref, y_ref):
    nseq, rows, d = x1_ref.shape
    m = nseq * rows
    a = (attn_ref[...] * ba_ref[...]).astype(BF16)
    c = (conv_ref[...] * bc_ref[...]).astype(BF16)
    mix = (jnp.dot(a, wo_ref[0:ATTN_WIDTH, :], preferred_element_type=F32)
           + jnp.dot(c, wo_ref[ATTN_WIDTH:, :], preferred_element_type=F32)).reshape(nseq, rows, d)
    x2 = _layernorm(alpha * x1_ref[...] + g1_ref[...] * mix, lng1_ref[...], lnb1_ref[...])
    h2 = (x2 * (1.0 + sc2_ref[...]) + sh2_ref[...]).reshape(m, d).astype(BF16)
    y = _ffn(h2, wg_ref, wu_ref, wd_ref).reshape(nseq, rows, d)
    y_ref[...] = _layernorm(alpha * x2 + g2_ref[...] * (0.5 * y), lng2_ref[...], lnb2_ref[...])


def _stage_c(x1, attn, conv, ada, ln_g1, ln_b1, ln_g2, ln_b2, beta_a, beta_c, wo, wg, wu, wd,
             *, alpha, nseq_blk, rows_blk):
    n_seq, s, _ = x1.shape
    tiles_per_seq = s // rows_blk
    assert nseq_blk == 1 or (nseq_blk == n_seq and tiles_per_seq == 1)
    m_blk = nseq_blk * rows_blk
    x_spec = pl.BlockSpec((nseq_blk, rows_blk, D_MODEL), lambda i, j: (i, j, 0))
    a_spec = pl.BlockSpec((nseq_blk, 1, D_MODEL), lambda i, j: (i, 0, 0))
    flat = pl.BlockSpec((m_blk, ATTN_WIDTH), lambda i, j: (i * tiles_per_seq + j, 0))
    return pl.pallas_call(
        functools.partial(_stage_c_kernel, alpha),
        grid=(n_seq // nseq_blk, tiles_per_seq),
        in_specs=[x_spec, flat, flat, a_spec, a_spec, a_spec, a_spec,
                  _const_spec((1, D_MODEL)), _const_spec((1, D_MODEL)),
                  _const_spec((1, D_MODEL)), _const_spec((1, D_MODEL)),
                  _const_spec((1, ATTN_WIDTH)), _const_spec((1, CONV_CH)),
                  _const_spec((D_MODEL, D_MODEL)),
                  _const_spec((D_MODEL, D_FF)), _const_spec((D_MODEL, D_FF)),
                  _const_spec((D_FF, D_MODEL))],
        out_specs=x_spec,
        out_shape=jax.ShapeDtypeStruct((n_seq, s, D_MODEL), F32),
        compiler_params=pltpu.CompilerParams(
            dimension_semantics=("arbitrary", "arbitrary"), vmem_limit_bytes=VMEM_LIMIT),
        name="stage_c",
    )(x1, attn, conv, ada["g1"], ada["sh2"], ada["sc2"], ada["g2"],
      ln_g1, ln_b1, ln_g2, ln_b2, beta_a, beta_c, wo, wg, wu, wd)


def _conv_kernel(n_chunks, hist_ref, prev_ref, cur_ref, w_ref, bdw_ref, gng_ref, gnb_ref, gavg_ref,
                 o_ref, buf_ref):
    nb, ch, c = cur_ref.shape
    if n_chunks == 1:
        halo = hist_ref[...]
    else:
        halo = jnp.where(pl.program_id(1) == 0, hist_ref[...], prev_ref[:, ch - HIST:, :])
    pad = CONV_PAD_ROWS
    buf_ref[:, 0:pad - HIST, :] = jnp.zeros((nb, pad - HIST, c), F32)
    buf_ref[:, pad - HIST:pad, :] = halo
    buf_ref[:, pad:pad + ch, :] = cur_ref[...]
    buf_ref[:, pad + ch:, :] = jnp.zeros((nb, SUBLANES, c), F32)
    acc = None
    for r in range(SUBLANES):
        part = None
        for a in range((pad + SUBLANES) // SUBLANES):
            j = SUBLANES * a + r - (pad - HIST)
            if 0 <= j < CONV_K:
                term = buf_ref[:, SUBLANES * a:SUBLANES * a + ch + SUBLANES, :] * w_ref[j:j + 1, :]
                part = term if part is None else part + term
        shifted = part[:, r:r + ch, :]
        acc = shifted if acc is None else acc + shifted
    y = (acc + bdw_ref[...]).reshape(nb * ch, c)
    gavg = gavg_ref[...]

    def group_mean(t):
        hi = t.astype(BF16)
        lo = (t - hi.astype(F32)).astype(BF16)
        return (jnp.dot(hi, gavg, preferred_element_type=F32)
                + jnp.dot(lo, gavg, preferred_element_type=F32))

    mu = group_mean(y)
    dlt = y - mu
    var = group_mean(dlt * dlt)
    z = dlt * lax.rsqrt(var + LN_EPS) * gng_ref[...] + gnb_ref[...]
    o_ref[...] = _silu(z).reshape(nb, ch, c)


def _conv_branch(hist, glu3, w_dw, b_dw, gn_g, gn_b, *, nb, ch):
    n_seq, s, c = glu3.shape
    n_chunks = s // ch
    assert ch >= HIST or n_chunks == 1
    grp = np.arange(c) // (c // GN_GROUPS)
    gavg = jnp.asarray((grp[:, None] == grp[None, :]).astype(np.float32) / (c // GN_GROUPS), BF16)
    cur_spec = pl.BlockSpec((nb, ch, c), lambda i, j: (i, j, 0))
    prev_spec = pl.BlockSpec((nb, ch, c), lambda i, j: (i, jnp.maximum(j - 1, 0), 0))
    return pl.pallas_call(
        functools.partial(_conv_kernel, n_chunks),
        grid=(n_seq // nb, n_chunks),
        in_specs=[pl.BlockSpec((nb, HIST, c), lambda i, j: (i, 0, 0)), prev_spec, cur_spec,
                  pl.BlockSpec((CONV_K, c), lambda i, j: (0, 0)),
                  pl.BlockSpec((1, c), lambda i, j: (0, 0)),
                  pl.BlockSpec((1, c), lambda i, j: (0, 0)),
                  pl.BlockSpec((1, c), lambda i, j: (0, 0)),
                  pl.BlockSpec((c, c), lambda i, j: (0, 0))],
        out_specs=cur_spec,
        out_shape=jax.ShapeDtypeStruct((n_seq, s, c), F32),
        scratch_shapes=[pltpu.VMEM((nb, CONV_PAD_ROWS + ch + SUBLANES, c), F32)],
        compiler_params=pltpu.CompilerParams(dimension_semantics=("arbitrary", "arbitrary")),
        name="conv_branch",
    )(hist, glu3, glu3, w_dw, b_dw.reshape(1, c), gn_g.reshape(1, c), gn_b.reshape(1, c), gavg)


MOBA_HEADS_PER_STEP = 8


def _moba_prompt_kernel(slopes_ref, qt_ref, k16_ref, vt16_ref, kmean_ref, o_ref,
                        q16_ref, bias_ref, krs_ref, s_ref, m_ref, l_ref, acc_ref):
    grp0 = pl.program_id(1)
    qb = pl.program_id(2)
    blk = MOBA_BLOCK
    nb = kmean_ref.shape[1]
    hps = MOBA_HEADS_PER_STEP
    row_head = lax.broadcasted_iota(jnp.int32, (LANES, blk), 0) // HEAD_DIM
    key_i = lax.broadcasted_iota(jnp.int32, (blk, blk), 0)
    qry_i = lax.broadcasted_iota(jnp.int32, (blk, blk), 1)
    key_f = key_i.astype(F32)
    blk_id = lax.broadcasted_iota(jnp.int32, (nb, blk), 0)
    past = blk_id < qb
    zero_row = jnp.zeros((1, blk), jnp.int32)

    def lanes_of(hh):
        g = hh // HEADS_PER_GROUP
        return slice(g * LANES, (g + 1) * LANES)

    def scores(hh, n):
        kb = k16_ref[0, pl.ds(pl.multiple_of(n * blk, blk), blk), lanes_of(hh)]
        return jnp.dot(kb, q16_ref[hh], preferred_element_type=F32) + krs_ref[hh]

    def pv(hh, n, p):
        p16 = p.astype(BF16)
        acc = None
        for i in range(PAGES_PER_BLOCK):
            vt = vt16_ref[0, n * PAGES_PER_BLOCK + i, hh]
            part = jnp.dot(vt, p16[i * PAGE_SIZE:(i + 1) * PAGE_SIZE, :], preferred_element_type=F32)
            acc = part if acc is None else acc + part
        return acc

    for hh in range(hps):
        slope2 = slopes_ref[grp0 * hps + hh]
        sub = hh % HEADS_PER_GROUP
        qh = jnp.where(row_head == sub, qt_ref[0, lanes_of(hh), :], 0.0)
        q16_ref[hh] = (qh * (SCALE * LOG2E)).astype(BF16)
        krs_ref[hh] = key_f * slope2

        gate = jnp.dot(kmean_ref[0, :, lanes_of(hh)], qh, preferred_element_type=F32, precision=HIGHEST)
        for n in range(nb):
            row = gate[n:n + 1, :]
            beats = ((gate > row) | ((gate == row) & (blk_id < n))) & past
            cnt = jnp.sum(jnp.where(beats, 1.0, 0.0), axis=0, keepdims=True)
            bias_ref[hh, n:n + 1, :] = jnp.where((cnt < MOBA_TOPK) & (n < qb), 0.0, NEG_INF)

        s = jnp.where(key_i <= qry_i, scores(hh, qb), NEG_INF)
        s_ref[hh, qb] = s
        m_ref[hh] = jnp.max(s, axis=0, keepdims=True)

    def block_const(hh, n):
        rel = ((n - qb) * blk + zero_row).astype(F32)
        return slopes_ref[grp0 * hps + hh] * rel + bias_ref[hh, pl.ds(n, 1), :]

    def pass1(n, carry):
        for hh in range(hps):
            s = scores(hh, n)
            s_ref[hh, n] = s
            m_ref[hh] = jnp.maximum(m_ref[hh], jnp.max(s, axis=0, keepdims=True) + block_const(hh, n))
        return carry

    lax.fori_loop(0, qb, pass1, 0)

    def block_out(hh, n, c):
        p = jnp.exp2(s_ref[hh, n] - (m_ref[hh] - c))
        return jnp.sum(p, axis=0, keepdims=True), pv(hh, n, p)

    for hh in range(hps):
        l_ref[hh], acc_ref[hh] = block_out(hh, qb, 0.0)

    def pass2(n, carry):
        for hh in range(hps):
            l_part, acc_part = block_out(hh, n, block_const(hh, n))
            l_ref[hh] = l_ref[hh] + l_part
            acc_ref[hh] = acc_ref[hh] + acc_part
        return carry

    lax.fori_loop(0, qb, pass2, 0)
    out_t = jnp.concatenate([acc_ref[hh] / l_ref[hh] for hh in range(hps)], axis=0)
    o_ref[0] = out_t.T


def _moba_prompt(qt3, k16, vt16, kmean, slopes):
    b, s, _ = k16.shape
    hps = MOBA_HEADS_PER_STEP
    width = hps * HEAD_DIM
    n_blocks = s // MOBA_BLOCK
    return pl.pallas_call(
        _moba_prompt_kernel,
        grid=(b, N_HEADS // hps, n_blocks),
        in_specs=[pl.BlockSpec(memory_space=pltpu.SMEM),
                  pl.BlockSpec((1, width, MOBA_BLOCK), lambda i, g, j: (i, g, j)),
                  pl.BlockSpec((1, s, width), lambda i, g, j: (i, 0, g)),
                  pl.BlockSpec((1, s // PAGE_SIZE, hps, HEAD_DIM, PAGE_SIZE),
                               lambda i, g, j: (i, 0, g, 0, 0)),
                  pl.BlockSpec((1, n_blocks, width), lambda i, g, j: (i, 0, g))],
        out_specs=pl.BlockSpec((1, MOBA_BLOCK, width), lambda i, g, j: (i, j, g)),
        out_shape=jax.ShapeDtypeStruct((b, s, ATTN_WIDTH), F32),
        scratch_shapes=[pltpu.VMEM((hps, LANES, MOBA_BLOCK), BF16),
                        pltpu.VMEM((hps, n_blocks, MOBA_BLOCK), F32),
                        pltpu.VMEM((hps, MOBA_BLOCK, MOBA_BLOCK), F32),
                        pltpu.VMEM((hps, n_blocks, MOBA_BLOCK, MOBA_BLOCK), F32),
                        pltpu.VMEM((hps, 1, MOBA_BLOCK), F32),
                        pltpu.VMEM((hps, 1, MOBA_BLOCK), F32),
                        pltpu.VMEM((hps, HEAD_DIM, MOBA_BLOCK), F32)],
        compiler_params=pltpu.CompilerParams(
            dimension_semantics=("arbitrary", "arbitrary", "arbitrary"), vmem_limit_bytes=VMEM_LIMIT),
        name="moba_prompt",
    )(slopes, qt3, k16, vt16, kmean)


PAGES_PER_STEP = 32
BLOCKS_PER_STEP = PAGES_PER_STEP // PAGES_PER_BLOCK


def _block_mean_kernel(pt_ref, *refs):
    page_refs, o_ref = refs[:-1], refs[-1]
    inv = 1.0 / MOBA_BLOCK
    lane = lax.broadcasted_iota(jnp.int32, (HEAD_DIM, BLOCKS_PER_STEP), 1)
    for h in range(N_HEADS):
        acc = jnp.zeros((HEAD_DIM, BLOCKS_PER_STEP), F32)
        for jb in range(BLOCKS_PER_STEP):
            tot = page_refs[jb * PAGES_PER_BLOCK][h]
            for i in range(1, PAGES_PER_BLOCK):
                tot = tot + page_refs[jb * PAGES_PER_BLOCK + i][h]
            acc = jnp.where(lane == jb, jnp.sum(tot, axis=1, keepdims=True) * inv, acc)
        o_ref[h] = acc


def _block_means(pool_kt, page_table_flat, n_seq, n_pages):
    steps = n_pages // PAGES_PER_STEP

    def page_spec(i):
        return pl.BlockSpec(
            (None, N_HEADS, HEAD_DIM, PAGE_SIZE),
            lambda b, c, pt: (pt[b * n_pages + c * PAGES_PER_STEP + i], 0, 0, 0))

    return pl.pallas_call(
        _block_mean_kernel,
        grid_spec=pltpu.PrefetchScalarGridSpec(
            num_scalar_prefetch=1,
            grid=(n_seq, steps),
            in_specs=[page_spec(i) for i in range(PAGES_PER_STEP)],
            out_specs=pl.BlockSpec((None, N_HEADS, None, HEAD_DIM, BLOCKS_PER_STEP),
                                   lambda b, c, pt: (b, 0, c, 0, 0)),
        ),
        out_shape=jax.ShapeDtypeStruct((n_seq, N_HEADS, steps, HEAD_DIM, BLOCKS_PER_STEP), F32),
        compiler_params=pltpu.CompilerParams(
            dimension_semantics=("arbitrary", "arbitrary"), vmem_limit_bytes=VMEM_LIMIT),
        name="block_means",
    )(page_table_flat, *([pool_kt] * PAGES_PER_STEP))


def _topk_kernel(q_ref, kmt_ref, o_ref):
    t, nbp = q_ref.shape[0], kmt_ref.shape[2]
    q = q_ref[...]
    lane = lax.broadcasted_iota(jnp.int32, (t, nbp), 1).astype(F32)
    out_lane = lax.broadcasted_iota(jnp.int32, (t, LANES), 1)
    for h in range(N_HEADS):
        g = jnp.dot(q[:, h * HEAD_DIM:(h + 1) * HEAD_DIM], kmt_ref[h],
                    preferred_element_type=F32, precision=HIGHEST)
        res = jnp.zeros((t, LANES), F32)
        for r in range(MOBA_TOPK):
            mx = jnp.max(g, axis=1, keepdims=True)
            idx = jnp.min(jnp.where(g == mx, lane, float(nbp)), axis=1, keepdims=True)
            res = jnp.where(out_lane == r, idx, res)
            g = jnp.where(lane == idx, -jnp.inf, g)
        o_ref[h] = res.astype(jnp.int32)


def _topk_blocks(q3, kmean_t):
    n_seq, t, _ = q3.shape
    nbp = kmean_t.shape[3]
    return pl.pallas_call(
        _topk_kernel,
        grid=(n_seq,),
        in_specs=[pl.BlockSpec((None, t, ATTN_WIDTH), lambda b: (b, 0, 0)),
                  pl.BlockSpec((None, N_HEADS, HEAD_DIM, nbp), lambda b: (b, 0, 0, 0))],
        out_specs=pl.BlockSpec((None, N_HEADS, t, LANES), lambda b: (b, 0, 0, 0)),
        out_shape=jax.ShapeDtypeStruct((n_seq, N_HEADS, t, LANES), jnp.int32),
        name="topk_blocks",
    )(q3, kmean_t)


def _moba_sample_kernel(past_len, n_pages, idx_ref, pt_ref, slopes_ref, qt_ref, knt_ref, vnt_ref,
                        pool_k_ref, pool_v_ref, o_ref, kbuf_ref, vbuf_ref, sem_ref):
    step = pl.program_id(0)
    n_steps = pl.num_programs(0)
    t = qt_ref.shape[1]
    slabs = MOBA_TOPK * PAGES_PER_BLOCK
    n_k = t * slabs

    def slab_copies(page, head, slot, n):
        return (pltpu.make_async_copy(pool_k_ref.at[page, head], kbuf_ref.at[slot, n], sem_ref.at[slot]),
                pltpu.make_async_copy(pool_v_ref.at[page, head], vbuf_ref.at[slot, n], sem_ref.at[slot]))

    def start_gather(for_step, slot):
        seq = for_step // N_HEADS
        head = for_step % N_HEADS
        for n in range(n_k):
            tq, rem = divmod(n, slabs)
            j, i = divmod(rem, PAGES_PER_BLOCK)
            blk_idx = idx_ref[(for_step * t + tq) * MOBA_TOPK + j]
            page = pt_ref[seq * n_pages + blk_idx * PAGES_PER_BLOCK + i]
            for cp in slab_copies(page, head, slot, n):
                cp.start()

    @pl.when(step == 0)
    def _():
        start_gather(0, 0)

    @pl.when(step + 1 < n_steps)
    def _():
        start_gather(step + 1, (step + 1) % 2)

    slot = step % 2
    for n in range(n_k):
        for cp in slab_copies(0, 0, slot, n):
            cp.wait()

    h = step % N_HEADS
    k_refs = [kbuf_ref.at[slot, n] for n in range(n_k)]
    v_refs = [vbuf_ref.at[slot, n] for n in range(n_k)]
    slope = slopes_ref[h]
    qt = qt_ref[...] * SCALE
    knt = knt_ref[...]
    vnt = vnt_ref[...]
    srow = lax.broadcasted_iota(jnp.int32, (slabs, PAGE_SIZE), 0)
    lane = lax.broadcasted_iota(jnp.int32, (slabs, PAGE_SIZE), 1)
    own_pos = lax.broadcasted_iota(jnp.int32, (1, t), 1)
    out_lane = lax.broadcasted_iota(jnp.int32, (HEAD_DIM, t), 1)
    out = jnp.zeros((HEAD_DIM, t), F32)
    for tq in range(t):
        q_col = qt[:, tq:tq + 1]
        base = (step * t + tq) * MOBA_TOPK
        s_rows = []
        blk_of_row = jnp.zeros((slabs, PAGE_SIZE), jnp.int32)
        for j in range(MOBA_TOPK):
            blk_of_row = jnp.where(srow // PAGES_PER_BLOCK == j, idx_ref[base + j], blk_of_row)
            for i in range(PAGES_PER_BLOCK):
                kt = k_refs[(tq * MOBA_TOPK + j) * PAGES_PER_BLOCK + i][...]
                s_rows.append(jnp.sum(kt * q_col, axis=0, keepdims=True))
        s_sel = jnp.concatenate(s_rows, axis=0)
        k_pos = blk_of_row * MOBA_BLOCK + (srow % PAGES_PER_BLOCK) * PAGE_SIZE + lane
        s_sel = s_sel - slope * ((past_len + tq) - k_pos).astype(F32)
        rel_own = tq - own_pos
        s_own = jnp.sum(knt * q_col, axis=0, keepdims=True)
        s_own = jnp.where(rel_own >= 0, s_own - slope * rel_own.astype(F32), NEG_INF)
        m = jnp.maximum(jnp.max(jnp.max(s_sel, axis=1, keepdims=True), axis=0, keepdims=True),
                        jnp.max(s_own, axis=1, keepdims=True))
        p_sel = jnp.exp(s_sel - m)
        p_own = jnp.exp(s_own - m)
        l = (jnp.sum(jnp.sum(p_sel, axis=1, keepdims=True), axis=0, keepdims=True)
             + jnp.sum(p_own, axis=1, keepdims=True))
        acc = jnp.zeros((HEAD_DIM, PAGE_SIZE), F32)
        for r in range(slabs):
            acc = acc + v_refs[tq * slabs + r][...] * p_sel[r:r + 1, :]
        o_col = (jnp.sum(acc, axis=1, keepdims=True) + jnp.sum(vnt * p_own, axis=1, keepdims=True)) / l
        out = jnp.where(out_lane == tq, o_col, out)
    o_ref[...] = out


def _moba_sample(qt4, knt4, vnt4, pool_kt, pool_vt, idx_flat, page_table_flat, slopes, *, past_len, n_pages):
    n_seq, _, _, t = qt4.shape
    n_k = t * MOBA_TOPK * PAGES_PER_BLOCK
    head_spec = pl.BlockSpec((None, None, HEAD_DIM, t),
                             lambda s, idx, pt: (s // N_HEADS, s % N_HEADS, 0, 0))
    hbm_spec = pl.BlockSpec(memory_space=pl.ANY)
    return pl.pallas_call(
        functools.partial(_moba_sample_kernel, past_len, n_pages),
        grid_spec=pltpu.PrefetchScalarGridSpec(
            num_scalar_prefetch=2,
            grid=(n_seq * N_HEADS,),
            in_specs=[pl.BlockSpec(memory_space=pltpu.SMEM), head_spec, head_spec, head_spec,
                      hbm_spec, hbm_spec],
            out_specs=head_spec,
            scratch_shapes=[pltpu.VMEM((2, n_k, HEAD_DIM, PAGE_SIZE), F32),
                            pltpu.VMEM((2, n_k, HEAD_DIM, PAGE_SIZE), F32),
                            pltpu.SemaphoreType.DMA((2,))],
        ),
        out_shape=jax.ShapeDtypeStruct((n_seq, N_HEADS, HEAD_DIM, t), F32),
        compiler_params=pltpu.CompilerParams(dimension_semantics=("arbitrary",)),
        name="moba_sample",
    )(idx_flat, page_table_flat, slopes, qt4, knt4, vnt4, pool_kt, pool_vt)


def _split_ada(ada_rows):
    a = ada_rows.reshape(ada_rows.shape[0], 3, 3, 1, D_MODEL)
    return {"sh0": a[:, 0, 0], "sc0": a[:, 0, 1], "g0": a[:, 0, 2],
            "sh1": a[:, 1, 0], "sc1": a[:, 1, 1], "g1": a[:, 1, 2],
            "sh2": a[:, 2, 0], "sc2": a[:, 2, 1], "g2": a[:, 2, 2]}


def _heads(t2, n_seq, s):
    return t2.reshape(n_seq, s, N_HEADS, HEAD_DIM)


def kernel(x_prompt, x_sample, cache_k, cache_v, state_conv, page_table, c_prompt, c_sample,
           w_ada, b_ada, ffn1_wg, ffn1_wu, ffn1_wd, w_in, w_dw, b_dw, gn_g, gn_b,
           beta_attn, beta_conv, w_out, ffn2_wg, ffn2_wu, ffn2_wd, ln_g, ln_b):
    depth = w_ada.shape[0]
    alpha = (2.0 * depth) ** 0.25
    batch, seq, _ = x_prompt.shape
    dec_batch, dec_seq, _ = x_sample.shape
    n_pages = page_table.shape[1]
    past_len = n_pages * PAGE_SIZE
    assert past_len % MOBA_BLOCK == 0 and dec_seq <= MOBA_BLOCK
    alibi = 2.0 ** (-8.0 * np.arange(1, N_HEADS + 1) / N_HEADS)
    slopes = jnp.asarray(alibi, F32)
    slopes_log2 = jnp.asarray(alibi * LOG2E, F32)
    page_table_flat = page_table.reshape(-1)

    y_p, y_s = x_prompt, x_sample
    outs = {name: [] for name in ("kp", "vp", "cp", "ks", "vs", "cs")}
    for l in range(depth):
        ada = _ada(jnp.concatenate([c_prompt, c_sample], axis=0), w_ada[l], b_ada[l])
        ada_p, ada_s = _split_ada(ada[:batch]), _split_ada(ada[batch:])
        lng = [ln_g[l, i].reshape(1, D_MODEL) for i in range(3)]
        lnb = [ln_b[l, i].reshape(1, D_MODEL) for i in range(3)]
        w1 = (ffn1_wg[l].astype(BF16), ffn1_wu[l].astype(BF16), ffn1_wd[l].astype(BF16))
        w2 = (ffn2_wg[l].astype(BF16), ffn2_wu[l].astype(BF16), ffn2_wd[l].astype(BF16))
        win16, wo16 = w_in[l].astype(BF16), w_out[l].astype(BF16)
        beta_a = beta_attn[l].reshape(1, ATTN_WIDTH)
        beta_c = beta_conv[l].reshape(1, CONV_CH)

        tile_p = dict(alpha=alpha, nseq_blk=1, rows_blk=512)
        x1, qt, k16, kmean, ktp, vtp, vt16, glu = _stage_a(y_p, ada_p, lng[0], lnb[0], *w1, win16,
                                                           paged=True, **tile_p)
        attn = _moba_prompt(qt, k16.reshape(batch, seq, ATTN_WIDTH), vt16, kmean, slopes_log2)
        glu3 = glu.reshape(batch, seq, CONV_CH)
        conv = _conv_branch(jnp.zeros((batch, HIST, CONV_CH), F32), glu3, w_dw[l], b_dw[l],
                            gn_g[l], gn_b[l], nb=1, ch=256)
        y_p = _stage_c(x1, attn.reshape(batch * seq, ATTN_WIDTH), conv.reshape(batch * seq, CONV_CH),
                       ada_p, lng[1], lnb[1], lng[2], lnb[2], beta_a, beta_c, wo16, *w2, **tile_p)
        outs["kp"].append(jnp.swapaxes(ktp, -1, -2))
        outs["vp"].append(jnp.swapaxes(vtp, -1, -2))
        outs["cp"].append(glu3[:, seq - HIST:, :])

        tile_s = dict(alpha=alpha, nseq_blk=dec_batch, rows_blk=dec_seq)
        x1, q, k, v, glu = _stage_a(y_s, ada_s, lng[0], lnb[0], *w1, win16, paged=False, **tile_s)
        q4, k4, v4 = (_heads(t2, dec_batch, dec_seq) for t2 in (q, k, v))
        pool_kt = jnp.swapaxes(cache_k[l], -1, -2)
        pool_vt = jnp.swapaxes(cache_v[l], -1, -2)
        kmean5 = _block_means(pool_kt, page_table_flat, dec_batch, n_pages)
        kmean_t = kmean5.transpose(0, 1, 3, 2, 4).reshape(dec_batch, N_HEADS, HEAD_DIM, -1)
        idx = _topk_blocks(q.reshape(dec_batch, dec_seq, ATTN_WIDTH), kmean_t)[..., :MOBA_TOPK]
        to_t = lambda a4: a4.transpose(0, 2, 3, 1)
        attn_t = _moba_sample(to_t(q4), to_t(k4), to_t(v4), pool_kt, pool_vt, idx.reshape(-1),
                              page_table_flat, slopes, past_len=past_len, n_pages=n_pages)
        attn = attn_t.transpose(0, 3, 1, 2).reshape(dec_batch * dec_seq, ATTN_WIDTH)
        glu3 = glu.reshape(dec_batch, dec_seq, CONV_CH)
        hist = state_conv[l]
        conv = _conv_branch(hist, glu3, w_dw[l], b_dw[l], gn_g[l], gn_b[l], nb=dec_batch, ch=dec_seq)
        y_s = _stage_c(x1, attn, conv.reshape(dec_batch * dec_seq, CONV_CH),
                       ada_s, lng[1], lnb[1], lng[2], lnb[2], beta_a, beta_c, wo16, *w2, **tile_s)
        outs["ks"].append(k4.transpose(0, 2, 1, 3))
        outs["vs"].append(v4.transpose(0, 2, 1, 3))
        outs["cs"].append(jnp.concatenate([hist, glu3], axis=1)[:, dec_seq:, :])

    stack = lambda name: jnp.stack(outs[name], 0)
    return (y_p, y_s, stack("kp"), stack("vp"), stack("cp"), stack("ks"), stack("vs"), stack("cs"))
```

```python
import functools

import numpy as np
import jax
import jax.numpy as jnp
from jax import lax
from jax.experimental import pallas as pl
from jax.experimental.pallas import tpu as pltpu

F32 = jnp.float32
BF16 = jnp.bfloat16
HIGHEST = lax.Precision.HIGHEST

D_MODEL = 1024
D_FF = 2816
N_HEADS = 8
HEAD_DIM = 64
ATTN_WIDTH = N_HEADS * HEAD_DIM
CONV_CH = 512
W_IN_COLS = 3 * ATTN_WIDTH + 2 * CONV_CH
CONV_K = 31
HIST = CONV_K - 1
GN_GROUPS = 8
MOBA_BLOCK = 256
MOBA_TOPK = 3
PAGE_SIZE = 128
PAGES_PER_BLOCK = MOBA_BLOCK // PAGE_SIZE
LN_EPS = 1e-5
NEG_INF = -1e30
SCALE = HEAD_DIM ** -0.5

LANES = 128
SUBLANES = 8
HEADS_PER_GROUP = LANES // HEAD_DIM
CONV_PAD_ROWS = 32
LOG2E = 1.4426950408889634
VMEM_LIMIT = 56 * 1024 * 1024

FF_CHUNKS = ((0, 1024), (1024, 2048), (2048, D_FF))


def _silu(x):
    return x / (1.0 + jnp.exp(-x))


def _sigmoid(x):
    return 1.0 / (1.0 + jnp.exp(-x))


def _layernorm(t, g, b):
    mu = jnp.mean(t, axis=-1, keepdims=True)
    d = t - mu
    var = jnp.mean(d * d, axis=-1, keepdims=True)
    return d * lax.rsqrt(var + LN_EPS) * g + b


def _ffn(h, wg_ref, wu_ref, wd_ref, chunks=FF_CHUNKS, before_chunk=None):
    acc = None
    for ci, (lo, hi) in enumerate(chunks):
        if before_chunk is not None:
            before_chunk(ci)
        g = jnp.dot(h, wg_ref[:, lo:hi], preferred_element_type=F32)
        u = jnp.dot(h, wu_ref[:, lo:hi], preferred_element_type=F32)
        a = (_silu(g) * u).astype(BF16)
        y = jnp.dot(a, wd_ref[lo:hi, :], preferred_element_type=F32)
        acc = y if acc is None else acc + y
    return acc


def _ada_kernel(c_ref, w_ref, b_ref, o_ref):
    a = _silu(c_ref[...])
    o_ref[...] = jnp.dot(a, w_ref[...], preferred_element_type=F32, precision=HIGHEST) + b_ref[...]


def _ada(c_all, w_ada, b_ada):
    n = c_all.shape[0]
    tn = 1024
    return pl.pallas_call(
        _ada_kernel,
        grid=(w_ada.shape[1] // tn,),
        in_specs=[
            pl.BlockSpec((n, D_MODEL), lambda j: (0, 0)),
            pl.BlockSpec((D_MODEL, tn), lambda j: (0, j)),
            pl.BlockSpec((1, tn), lambda j: (0, j)),
        ],
        out_specs=pl.BlockSpec((n, tn), lambda j: (0, j)),
        out_shape=jax.ShapeDtypeStruct((n, w_ada.shape[1]), F32),
        name="ada",
    )(c_all, w_ada, b_ada.reshape(1, -1))


STREAM_CHUNK_PAGES = 16
STREAM_FF_CHUNKS = ((0, 512), (512, 1024), (1024, 1536), (1536, 2048), (2048, 2560), (2560, D_FF))
STREAM_CHUNKS_PER_STEP = len(STREAM_FF_CHUNKS) + 2


def _block_mean_stream(step, n_steps, pt_ref, pool_ref, kmt_ref, ring_ref, sem_ref):
    cps = STREAM_CHUNKS_PER_STEP
    blocks_per_chunk = STREAM_CHUNK_PAGES // PAGES_PER_BLOCK
    n_cols = kmt_ref.shape[3]
    assert cps % 2 == 0 and cps * blocks_per_chunk == n_cols

    def page_copy(page, slot, i):
        return pltpu.make_async_copy(pool_ref.at[page], ring_ref.at[slot, i], sem_ref.at[slot])

    def start_chunk(g, slot):
        for i in range(STREAM_CHUNK_PAGES):
            page_copy(pt_ref[g * STREAM_CHUNK_PAGES + i], slot, i).start()

    def ring_step(c):
        g = step * cps + c
        slot = c % 2
        if c == 0:
            @pl.when(step == 0)
            def _():
                start_chunk(0, 0)
        if c + 1 < cps:
            start_chunk(g + 1, (c + 1) % 2)
        else:
            @pl.when(step + 1 < n_steps)
            def _():
                start_chunk(g + 1, (c + 1) % 2)
        for i in range(STREAM_CHUNK_PAGES):
            page_copy(0, slot, i).wait()
        col_id = lax.broadcasted_iota(jnp.int32, (HEAD_DIM, n_cols), 1)
        cols = slice(c * blocks_per_chunk, (c + 1) * blocks_per_chunk)
        for h in range(N_HEADS):
            acc = jnp.zeros((HEAD_DIM, n_cols), F32)
            for jb in range(blocks_per_chunk):
                tot = ring_ref[slot, jb * PAGES_PER_BLOCK, h]
                for i in range(1, PAGES_PER_BLOCK):
                    tot = tot + ring_ref[slot, jb * PAGES_PER_BLOCK + i, h]
                mean = jnp.sum(tot, axis=1, keepdims=True) * (1.0 / MOBA_BLOCK)
                acc = jnp.where(col_id == c * blocks_per_chunk + jb, mean, acc)
            kmt_ref[0, h, :, cols] = acc[:, cols]

    return ring_step


def _stage_a_kernel(alpha, paged, *refs):
    (x_ref, sh0_ref, sc0_ref, g0_ref, sh1_ref, sc1_ref, lng_ref, lnb_ref,
     wg_ref, wu_ref, wd_ref, win_ref) = refs[:12]
    if paged:
        pt_ref, pool_ref, x1_ref = refs[12:15]
        out_refs = refs[15:22]
        kmt_ref, ring_ref, sem_ref = refs[22:25]
        step = pl.program_id(0) * pl.num_programs(1) + pl.program_id(1)
        ring_step = _block_mean_stream(step, pl.num_programs(0) * pl.num_programs(1),
                                       pt_ref, pool_ref, kmt_ref, ring_ref, sem_ref)
        ff_chunks = STREAM_FF_CHUNKS
    else:
        x1_ref = refs[12]
        out_refs = refs[13:]
        ring_step = None
        ff_chunks = FF_CHUNKS
    nseq, rows, d = x_ref.shape
    m = nseq * rows
    x = x_ref[...]
    h0 = (x * (1.0 + sc0_ref[...]) + sh0_ref[...]).reshape(m, d).astype(BF16)
    y = _ffn(h0, wg_ref, wu_ref, wd_ref, ff_chunks, ring_step).reshape(nseq, rows, d)
    x1 = _layernorm(alpha * x + g0_ref[...] * (0.5 * y), lng_ref[...], lnb_ref[...])
    x1_ref[...] = x1
    h1 = (x1 * (1.0 + sc1_ref[...]) + sh1_ref[...]).reshape(m, d).astype(BF16)
    qkv_cols = 3 * ATTN_WIDTH
    if paged:
        ring_step(len(ff_chunks))
    u = jnp.dot(h1, win_ref[:, 0:qkv_cols], preferred_element_type=F32)
    q = u[:, 0:ATTN_WIDTH]
    k = u[:, ATTN_WIDTH:2 * ATTN_WIDTH]
    v = u[:, 2 * ATTN_WIDTH:3 * ATTN_WIDTH]
    if paged:
        ring_step(len(ff_chunks) + 1)
    ug = jnp.dot(h1, win_ref[:, qkv_cols:], preferred_element_type=F32)
    ga = ug[:, 0:CONV_CH]
    gb = ug[:, CONV_CH:]
    glu = ga * _sigmoid(gb)
    if paged:
        qt_ref, k16_ref, kmean_ref, ktp_ref, vtp_ref, vt16_ref, glu_ref = out_refs
        qt_ref[0] = q.T
        k16_ref[...] = k.astype(BF16)
        blocks_per_tile = m // MOBA_BLOCK
        means = [jnp.sum(k[n * MOBA_BLOCK:(n + 1) * MOBA_BLOCK, :], axis=0, keepdims=True)
                 * (1.0 / MOBA_BLOCK) for n in range(blocks_per_tile)]
        tile = pl.program_id(1)
        for jj in range(kmean_ref.shape[1] // blocks_per_tile):
            @pl.when(tile == jj)
            def _():
                for n in range(blocks_per_tile):
                    kmean_ref[0, jj * blocks_per_tile + n:jj * blocks_per_tile + n + 1, :] = means[n]
        kt = k.T
        vt = v.T
        for p in range(m // PAGE_SIZE):
            for h in range(N_HEADS):
                rs = slice(h * HEAD_DIM, (h + 1) * HEAD_DIM)
                cs = slice(p * PAGE_SIZE, (p + 1) * PAGE_SIZE)
                ktp_ref[0, p, h] = kt[rs, cs]
                vtp_ref[0, p, h] = vt[rs, cs]
                vt16_ref[0, p, h] = vt[rs, cs].astype(BF16)
    else:
        q_ref, k_ref, v_ref, glu_ref = out_refs
        q_ref[...] = q
        k_ref[...] = k
        v_ref[...] = v
    glu_ref[...] = glu


def _const_spec(shape):
    return pl.BlockSpec(shape, lambda i, j: (0,) * len(shape), pipeline_mode=pl.Buffered(1))


def _stage_a(x, ada, ln_g, ln_b, wg, wu, wd, win, *, alpha, nseq_blk, rows_blk, paged,
             page_table_flat=None, pool_kt=None, cache_seqs=None):
    n_seq, s, _ = x.shape
    tiles_per_seq = s // rows_blk
    assert nseq_blk == 1 or (nseq_blk == n_seq and tiles_per_seq == 1)
    m_blk = nseq_blk * rows_blk
    n_tok = n_seq * s
    x_spec = pl.BlockSpec((nseq_blk, rows_blk, D_MODEL), lambda i, j: (i, j, 0))
    a_spec = pl.BlockSpec((nseq_blk, 1, D_MODEL), lambda i, j: (i, 0, 0))
    flat_spec = pl.BlockSpec((m_blk, ATTN_WIDTH), lambda i, j: (i * tiles_per_seq + j, 0))
    flat_shape = jax.ShapeDtypeStruct((n_tok, ATTN_WIDTH), F32)
    if paged:
        assert nseq_blk == 1 and rows_blk % MOBA_BLOCK == 0 and MOBA_BLOCK % PAGE_SIZE == 0
        ppt = rows_blk // PAGE_SIZE
        page_dims = (n_seq, s // PAGE_SIZE, N_HEADS, HEAD_DIM, PAGE_SIZE)
        page_spec = pl.BlockSpec((1, ppt, N_HEADS, HEAD_DIM, PAGE_SIZE), lambda i, j: (i, j, 0, 0, 0))
        page_shape = jax.ShapeDtypeStruct(page_dims, F32)
        out_specs = [x_spec, pl.BlockSpec((1, ATTN_WIDTH, rows_blk), lambda i, j: (i, 0, j)),
                     flat_spec,
                     pl.BlockSpec((1, s // MOBA_BLOCK, ATTN_WIDTH), lambda i, j: (i, 0, 0)),
                     page_spec, page_spec, page_spec, flat_spec]
        out_shape = [jax.ShapeDtypeStruct((n_seq, s, D_MODEL), F32),
                     jax.ShapeDtypeStruct((n_seq, ATTN_WIDTH, s), F32),
                     jax.ShapeDtypeStruct((n_tok, ATTN_WIDTH), BF16),
                     jax.ShapeDtypeStruct((n_seq, s // MOBA_BLOCK, ATTN_WIDTH), F32),
                     page_shape, page_shape, jax.ShapeDtypeStruct(page_dims, BF16), flat_shape]
        n_steps = n_seq * tiles_per_seq
        pages_per_seq = page_table_flat.shape[0] // cache_seqs
        assert cache_seqs == n_steps and pages_per_seq == STREAM_CHUNKS_PER_STEP * STREAM_CHUNK_PAGES
        blocks_per_seq = pages_per_seq // PAGES_PER_BLOCK
        extra_in = [page_table_flat, pool_kt]
        extra_in_specs = [pl.BlockSpec(memory_space=pltpu.SMEM), pl.BlockSpec(memory_space=pl.ANY)]
        out_specs.append(pl.BlockSpec((1, N_HEADS, HEAD_DIM, blocks_per_seq),
                                      lambda i, j: (i * tiles_per_seq + j, 0, 0, 0)))
        out_shape.append(jax.ShapeDtypeStruct((cache_seqs, N_HEADS, HEAD_DIM, blocks_per_seq), F32))
        scratch = [pltpu.VMEM((2, STREAM_CHUNK_PAGES, N_HEADS, HEAD_DIM, PAGE_SIZE), F32),
                   pltpu.SemaphoreType.DMA((2,))]
    else:
        out_specs = [x_spec, flat_spec, flat_spec, flat_spec, flat_spec]
        out_shape = [jax.ShapeDtypeStruct((n_seq, s, D_MODEL), F32)] + [flat_shape] * 4
        extra_in, extra_in_specs, scratch = [], [], []
    return pl.pallas_call(
        functools.partial(_stage_a_kernel, alpha, paged),
        grid=(n_seq // nseq_blk, tiles_per_seq),
        in_specs=[x_spec, a_spec, a_spec, a_spec, a_spec, a_spec,
                  _const_spec((1, D_MODEL)), _const_spec((1, D_MODEL)),
                  _const_spec((D_MODEL, D_FF)), _const_spec((D_MODEL, D_FF)),
                  _const_spec((D_FF, D_MODEL)), _const_spec((D_MODEL, W_IN_COLS))] + extra_in_specs,
        out_specs=out_specs,
        out_shape=out_shape,
        scratch_shapes=scratch,
        compiler_params=pltpu.CompilerParams(
            dimension_semantics=("arbitrary", "arbitrary"), vmem_limit_bytes=VMEM_LIMIT),
        name="stage_a",
    )(x, ada["sh0"], ada["sc0"], ada["g0"], ada["sh1"], ada["sc1"], ln_g, ln_b, wg, wu, wd, win, *extra_in)


def _stage_c_kernel(alpha, sample_cfg, *refs):
    (x1_ref, attn_ref, conv_ref, g1_ref, sh2_ref, sc2_ref, g2_ref, lng1_ref, lnb1_ref, lng2_ref, lnb2_ref,
     ba_ref, bc_ref, wo_ref, wg_ref, wu_ref, wd_ref) = refs[:17]
    if sample_cfg is not None:
        past_len, n_pages = sample_cfg
        y_ref, so_ref = refs[25:27]
        step = pl.program_id(0) * pl.num_programs(1) + pl.program_id(1)
        ring_step = _sample_attention_stream(step, pl.num_programs(0) * pl.num_programs(1), past_len,
                                             n_pages, *refs[17:25], so_ref, *refs[27:30])
        ff_chunks = STREAM_FF_CHUNKS
        assert len(ff_chunks) + 2 == N_HEADS
        before_chunk = lambda ci: ring_step(ci + 1)
        ring_step(0)
    else:
        y_ref = refs[17]
        ring_step, before_chunk, ff_chunks = None, None, FF_CHUNKS
    nseq, rows, d = x1_ref.shape
    m = nseq * rows
    a = (attn_ref[...] * ba_ref[...]).astype(BF16)
    c = (conv_ref[...] * bc_ref[...]).astype(BF16)
    mix = (jnp.dot(a, wo_ref[0:ATTN_WIDTH, :], preferred_element_type=F32)
           + jnp.dot(c, wo_ref[ATTN_WIDTH:, :], preferred_element_type=F32)).reshape(nseq, rows, d)
    x2 = _layernorm(alpha * x1_ref[...] + g1_ref[...] * mix, lng1_ref[...], lnb1_ref[...])
    h2 = (x2 * (1.0 + sc2_ref[...]) + sh2_ref[...]).reshape(m, d).astype(BF16)
    y = _ffn(h2, wg_ref, wu_ref, wd_ref, ff_chunks, before_chunk).reshape(nseq, rows, d)
    if ring_step is not None:
        ring_step(N_HEADS - 1)
    y_ref[...] = _layernorm(alpha * x2 + g2_ref[...] * (0.5 * y), lng2_ref[...], lnb2_ref[...])


def _stage_c(x1, attn, conv, ada, ln_g1, ln_b1, ln_g2, ln_b2, beta_a, beta_c, wo, wg, wu, wd,
             *, alpha, nseq_blk, rows_blk, sample=None):
    n_seq, s, _ = x1.shape
    tiles_per_seq = s // rows_blk
    assert nseq_blk == 1 or (nseq_blk == n_seq and tiles_per_seq == 1)
    m_blk = nseq_blk * rows_blk
    x_spec = pl.BlockSpec((nseq_blk, rows_blk, D_MODEL), lambda i, j: (i, j, 0))
    a_spec = pl.BlockSpec((nseq_blk, 1, D_MODEL), lambda i, j: (i, 0, 0))
    flat = pl.BlockSpec((m_blk, ATTN_WIDTH), lambda i, j: (i * tiles_per_seq + j, 0))
    out_specs = [x_spec]
    out_shape = [jax.ShapeDtypeStruct((n_seq, s, D_MODEL), F32)]
    extra_in, extra_in_specs, scratch, cfg = [], [], [], None
    if sample is not None:
        seqs, _, _, t = sample["qt"].shape
        assert seqs == n_seq * tiles_per_seq
        n_k = t * MOBA_TOPK * PAGES_PER_BLOCK
        cfg = (sample["past_len"], sample["n_pages"])
        smem = pl.BlockSpec(memory_space=pltpu.SMEM)
        hbm = pl.BlockSpec(memory_space=pl.ANY)
        seq_spec = pl.BlockSpec((1, N_HEADS, HEAD_DIM, t), lambda i, j: (i * tiles_per_seq + j, 0, 0, 0))
        extra_in = [sample["idx_flat"], sample["page_table_flat"], sample["slopes"],
                    sample["qt"], sample["knt"], sample["vnt"], sample["pool_kt"], sample["pool_vt"]]
        extra_in_specs = [smem, smem, smem, seq_spec, seq_spec, seq_spec, hbm, hbm]
        out_specs.append(seq_spec)
        out_shape.append(jax.ShapeDtypeStruct((seqs, N_HEADS, HEAD_DIM, t), F32))
        scratch = [pltpu.VMEM((2, n_k, HEAD_DIM, PAGE_SIZE), F32),
                   pltpu.VMEM((2, n_k, HEAD_DIM, PAGE_SIZE), F32),
                   pltpu.SemaphoreType.DMA((2,))]
    res = pl.pallas_call(
        functools.partial(_stage_c_kernel, alpha, cfg),
        grid=(n_seq // nseq_blk, tiles_per_seq),
        in_specs=[x_spec, flat, flat, a_spec, a_spec, a_spec, a_spec,
                  _const_spec((1, D_MODEL)), _const_spec((1, D_MODEL)),
                  _const_spec((1, D_MODEL)), _const_spec((1, D_MODEL)),
                  _const_spec((1, ATTN_WIDTH)), _const_spec((1, CONV_CH)),
                  _const_spec((D_MODEL, D_MODEL)),
                  _const_spec((D_MODEL, D_FF)), _const_spec((D_MODEL, D_FF)),
                  _const_spec((D_FF, D_MODEL))] + extra_in_specs,
        out_specs=out_specs,
        out_shape=out_shape,
        scratch_shapes=scratch,
        compiler_params=pltpu.CompilerParams(
            dimension_semantics=("arbitrary", "arbitrary"), vmem_limit_bytes=VMEM_LIMIT),
        name="stage_c",
    )(x1, attn, conv, ada["g1"], ada["sh2"], ada["sc2"], ada["g2"],
      ln_g1, ln_b1, ln_g2, ln_b2, beta_a, beta_c, wo, wg, wu, wd, *extra_in)
    return res if sample is not None else res[0]


def _conv_kernel(n_chunks, hist_ref, prev_ref, cur_ref, w_ref, bdw_ref, gng_ref, gnb_ref, gavg_ref,
                 o_ref, buf_ref):
    nb, ch, c = cur_ref.shape
    if n_chunks == 1:
        halo = hist_ref[...]
    else:
        halo = jnp.where(pl.program_id(1) == 0, hist_ref[...], prev_ref[:, ch - HIST:, :])
    pad = CONV_PAD_ROWS
    buf_ref[:, 0:pad - HIST, :] = jnp.zeros((nb, pad - HIST, c), F32)
    buf_ref[:, pad - HIST:pad, :] = halo
    buf_ref[:, pad:pad + ch, :] = cur_ref[...]
    buf_ref[:, pad + ch:, :] = jnp.zeros((nb, SUBLANES, c), F32)
    acc = None
    for r in range(SUBLANES):
        part = None
        for a in range((pad + SUBLANES) // SUBLANES):
            j = SUBLANES * a + r - (pad - HIST)
            if 0 <= j < CONV_K:
                term = buf_ref[:, SUBLANES * a:SUBLANES * a + ch + SUBLANES, :] * w_ref[j:j + 1, :]
                part = term if part is None else part + term
        shifted = part[:, r:r + ch, :]
        acc = shifted if acc is None else acc + shifted
    y = (acc + bdw_ref[...]).reshape(nb * ch, c)
    gavg = gavg_ref[...]

    def group_mean(t):
        hi = t.astype(BF16)
        lo = (t - hi.astype(F32)).astype(BF16)
        return (jnp.dot(hi, gavg, preferred_element_type=F32)
                + jnp.dot(lo, gavg, preferred_element_type=F32))

    mu = group_mean(y)
    dlt = y - mu
    var = group_mean(dlt * dlt)
    z = dlt * lax.rsqrt(var + LN_EPS) * gng_ref[...] + gnb_ref[...]
    o_ref[...] = _silu(z).reshape(nb, ch, c)


def _conv_branch(hist, glu3, w_dw, b_dw, gn_g, gn_b, *, nb, ch):
    n_seq, s, c = glu3.shape
    n_chunks = s // ch
    assert ch >= HIST or n_chunks == 1
    grp = np.arange(c) // (c // GN_GROUPS)
    gavg = jnp.asarray((grp[:, None] == grp[None, :]).astype(np.float32) / (c // GN_GROUPS), BF16)
    cur_spec = pl.BlockSpec((nb, ch, c), lambda i, j: (i, j, 0))
    prev_spec = pl.BlockSpec((nb, ch, c), lambda i, j: (i, jnp.maximum(j - 1, 0), 0))
    return pl.pallas_call(
        functools.partial(_conv_kernel, n_chunks),
        grid=(n_seq // nb, n_chunks),
        in_specs=[pl.BlockSpec((nb, HIST, c), lambda i, j: (i, 0, 0)), prev_spec, cur_spec,
                  pl.BlockSpec((CONV_K, c), lambda i, j: (0, 0)),
                  pl.BlockSpec((1, c), lambda i, j: (0, 0)),
                  pl.BlockSpec((1, c), lambda i, j: (0, 0)),
                  pl.BlockSpec((1, c), lambda i, j: (0, 0)),
                  pl.BlockSpec((c, c), lambda i, j: (0, 0))],
        out_specs=cur_spec,
        out_shape=jax.ShapeDtypeStruct((n_seq, s, c), F32),
        scratch_shapes=[pltpu.VMEM((nb, CONV_PAD_ROWS + ch + SUBLANES, c), F32)],
        compiler_params=pltpu.CompilerParams(dimension_semantics=("arbitrary", "arbitrary")),
        name="conv_branch",
    )(hist, glu3, glu3, w_dw, b_dw.reshape(1, c), gn_g.reshape(1, c), gn_b.reshape(1, c), gavg)


MOBA_HEADS_PER_STEP = 8


def _moba_prompt_kernel(slopes_ref, qt_ref, k16_ref, vt16_ref, kmean_ref, o_ref,
                        q16_ref, bias_ref, krs_ref, s_ref, m_ref, l_ref, acc_ref):
    grp0 = pl.program_id(1)
    qb = pl.program_id(2)
    blk = MOBA_BLOCK
    nb = kmean_ref.shape[1]
    hps = MOBA_HEADS_PER_STEP
    row_head = lax.broadcasted_iota(jnp.int32, (LANES, blk), 0) // HEAD_DIM
    key_i = lax.broadcasted_iota(jnp.int32, (blk, blk), 0)
    qry_i = lax.broadcasted_iota(jnp.int32, (blk, blk), 1)
    key_f = key_i.astype(F32)
    blk_id = lax.broadcasted_iota(jnp.int32, (nb, blk), 0)
    past = blk_id < qb
    zero_row = jnp.zeros((1, blk), jnp.int32)

    def lanes_of(hh):
        g = hh // HEADS_PER_GROUP
        return slice(g * LANES, (g + 1) * LANES)

    def scores(hh, n):
        kb = k16_ref[0, pl.ds(pl.multiple_of(n * blk, blk), blk), lanes_of(hh)]
        return jnp.dot(kb, q16_ref[hh], preferred_element_type=F32) + krs_ref[hh]

    def pv(hh, n, p):
        p16 = p.astype(BF16)
        acc = None
        for i in range(PAGES_PER_BLOCK):
            vt = vt16_ref[0, n * PAGES_PER_BLOCK + i, hh]
            part = jnp.dot(vt, p16[i * PAGE_SIZE:(i + 1) * PAGE_SIZE, :], preferred_element_type=F32)
            acc = part if acc is None else acc + part
        return acc

    for hh in range(hps):
        slope2 = slopes_ref[grp0 * hps + hh]
        sub = hh % HEADS_PER_GROUP
        qh = jnp.where(row_head == sub, qt_ref[0, lanes_of(hh), :], 0.0)
        q16_ref[hh] = (qh * (SCALE * LOG2E)).astype(BF16)
        krs_ref[hh] = key_f * slope2

        gate = jnp.dot(kmean_ref[0, :, lanes_of(hh)], qh, preferred_element_type=F32, precision=HIGHEST)
        for n in range(nb):
            row = gate[n:n + 1, :]
            beats = ((gate > row) | ((gate == row) & (blk_id < n))) & past
            cnt = jnp.sum(jnp.where(beats, 1.0, 0.0), axis=0, keepdims=True)
            bias_ref[hh, n:n + 1, :] = jnp.where((cnt < MOBA_TOPK) & (n < qb), 0.0, NEG_INF)

        s = jnp.where(key_i <= qry_i, scores(hh, qb), NEG_INF)
        s_ref[hh, qb] = s
        m_ref[hh] = jnp.max(s, axis=0, keepdims=True)

    def block_const(hh, n):
        rel = ((n - qb) * blk + zero_row).astype(F32)
        return slopes_ref[grp0 * hps + hh] * rel + bias_ref[hh, pl.ds(n, 1), :]

    def pass1(n, carry):
        for hh in range(hps):
            s = scores(hh, n)
            s_ref[hh, n] = s
            m_ref[hh] = jnp.maximum(m_ref[hh], jnp.max(s, axis=0, keepdims=True) + block_const(hh, n))
        return carry

    lax.fori_loop(0, qb, pass1, 0)

    def block_out(hh, n, c):
        p = jnp.exp2(s_ref[hh, n] - (m_ref[hh] - c))
        return jnp.sum(p, axis=0, keepdims=True), pv(hh, n, p)

    for hh in range(hps):
        l_ref[hh], acc_ref[hh] = block_out(hh, qb, 0.0)

    def pass2(n, carry):
        for hh in range(hps):
            l_part, acc_part = block_out(hh, n, block_const(hh, n))
            l_ref[hh] = l_ref[hh] + l_part
            acc_ref[hh] = acc_ref[hh] + acc_part
        return carry

    lax.fori_loop(0, qb, pass2, 0)
    out_t = jnp.concatenate([acc_ref[hh] / l_ref[hh] for hh in range(hps)], axis=0)
    o_ref[0] = out_t.T


def _moba_prompt(qt3, k16, vt16, kmean, slopes):
    b, s, _ = k16.shape
    hps = MOBA_HEADS_PER_STEP
    width = hps * HEAD_DIM
    n_blocks = s // MOBA_BLOCK
    return pl.pallas_call(
        _moba_prompt_kernel,
        grid=(b, N_HEADS // hps, n_blocks),
        in_specs=[pl.BlockSpec(memory_space=pltpu.SMEM),
                  pl.BlockSpec((1, width, MOBA_BLOCK), lambda i, g, j: (i, g, j)),
                  pl.BlockSpec((1, s, width), lambda i, g, j: (i, 0, g)),
                  pl.BlockSpec((1, s // PAGE_SIZE, hps, HEAD_DIM, PAGE_SIZE),
                               lambda i, g, j: (i, 0, g, 0, 0)),
                  pl.BlockSpec((1, n_blocks, width), lambda i, g, j: (i, 0, g))],
        out_specs=pl.BlockSpec((1, MOBA_BLOCK, width), lambda i, g, j: (i, j, g)),
        out_shape=jax.ShapeDtypeStruct((b, s, ATTN_WIDTH), F32),
        scratch_shapes=[pltpu.VMEM((hps, LANES, MOBA_BLOCK), BF16),
                        pltpu.VMEM((hps, n_blocks, MOBA_BLOCK), F32),
                        pltpu.VMEM((hps, MOBA_BLOCK, MOBA_BLOCK), F32),
                        pltpu.VMEM((hps, n_blocks, MOBA_BLOCK, MOBA_BLOCK), F32),
                        pltpu.VMEM((hps, 1, MOBA_BLOCK), F32),
                        pltpu.VMEM((hps, 1, MOBA_BLOCK), F32),
                        pltpu.VMEM((hps, HEAD_DIM, MOBA_BLOCK), F32)],
        compiler_params=pltpu.CompilerParams(
            dimension_semantics=("arbitrary", "arbitrary", "arbitrary"), vmem_limit_bytes=VMEM_LIMIT),
        name="moba_prompt",
    )(slopes, qt3, k16, vt16, kmean)


def _topk_kernel(q_ref, kmt_ref, o_ref):
    t, nbp = q_ref.shape[0], kmt_ref.shape[2]
    q = q_ref[...]
    lane = lax.broadcasted_iota(jnp.int32, (t, nbp), 1).astype(F32)
    out_lane = lax.broadcasted_iota(jnp.int32, (t, LANES), 1)
    for h in range(N_HEADS):
        g = jnp.dot(q[:, h * HEAD_DIM:(h + 1) * HEAD_DIM], kmt_ref[h],
                    preferred_element_type=F32, precision=HIGHEST)
        res = jnp.zeros((t, LANES), F32)
        for r in range(MOBA_TOPK):
            mx = jnp.max(g, axis=1, keepdims=True)
            idx = jnp.min(jnp.where(g == mx, lane, float(nbp)), axis=1, keepdims=True)
            res = jnp.where(out_lane == r, idx, res)
            g = jnp.where(lane == idx, -jnp.inf, g)
        o_ref[h] = res.astype(jnp.int32)


def _topk_blocks(q3, kmean_t):
    n_seq, t, _ = q3.shape
    nbp = kmean_t.shape[3]
    return pl.pallas_call(
        _topk_kernel,
        grid=(n_seq,),
        in_specs=[pl.BlockSpec((None, t, ATTN_WIDTH), lambda b: (b, 0, 0)),
                  pl.BlockSpec((None, N_HEADS, HEAD_DIM, nbp), lambda b: (b, 0, 0, 0))],
        out_specs=pl.BlockSpec((None, N_HEADS, t, LANES), lambda b: (b, 0, 0, 0)),
        out_shape=jax.ShapeDtypeStruct((n_seq, N_HEADS, t, LANES), jnp.int32),
        name="topk_blocks",
    )(q3, kmean_t)


def _sample_attention_stream(step, n_steps, past_len, n_pages, idx_ref, pt_ref, slopes_ref,
                             qt_ref, knt_ref, vnt_ref, pool_k_ref, pool_v_ref, o_ref,
                             kbuf_ref, vbuf_ref, sem_ref):
    t = qt_ref.shape[3]
    slabs = MOBA_TOPK * PAGES_PER_BLOCK
    n_k = t * slabs
    assert N_HEADS % 2 == 0

    def slab_copies(page, head, slot, n):
        return (pltpu.make_async_copy(pool_k_ref.at[page, head], kbuf_ref.at[slot, n], sem_ref.at[slot]),
                pltpu.make_async_copy(pool_v_ref.at[page, head], vbuf_ref.at[slot, n], sem_ref.at[slot]))

    def start_gather(seq, head, slot):
        for n in range(n_k):
            tq, rem = divmod(n, slabs)
            j, i = divmod(rem, PAGES_PER_BLOCK)
            blk_idx = idx_ref[((seq * N_HEADS + head) * t + tq) * MOBA_TOPK + j]
            page = pt_ref[seq * n_pages + blk_idx * PAGES_PER_BLOCK + i]
            for cp in slab_copies(page, head, slot, n):
                cp.start()

    def ring_step(h):
        slot = h % 2
        if h == 0:
            @pl.when(step == 0)
            def _():
                start_gather(0, 0, 0)
        if h + 1 < N_HEADS:
            start_gather(step, h + 1, (h + 1) % 2)
        else:
            @pl.when(step + 1 < n_steps)
            def _():
                start_gather(step + 1, 0, 0)
        for n in range(n_k):
            for cp in slab_copies(0, 0, slot, n):
                cp.wait()

        slope = slopes_ref[h]
        qt = qt_ref[0, h] * SCALE
        knt = knt_ref[0, h]
        vnt = vnt_ref[0, h]
        srow = lax.broadcasted_iota(jnp.int32, (slabs, PAGE_SIZE), 0)
        lane = lax.broadcasted_iota(jnp.int32, (slabs, PAGE_SIZE), 1)
        own_pos = lax.broadcasted_iota(jnp.int32, (1, t), 1)
        out_lane = lax.broadcasted_iota(jnp.int32, (HEAD_DIM, t), 1)
        out = jnp.zeros((HEAD_DIM, t), F32)
        for tq in range(t):
            q_col = qt[:, tq:tq + 1]
            base = ((step * N_HEADS + h) * t + tq) * MOBA_TOPK
            s_rows = []
            blk_of_row = jnp.zeros((slabs, PAGE_SIZE), jnp.int32)
            for j in range(MOBA_TOPK):
                blk_of_row = jnp.where(srow // PAGES_PER_BLOCK == j, idx_ref[base + j], blk_of_row)
                for i in range(PAGES_PER_BLOCK):
                    kt = kbuf_ref[slot, (tq * MOBA_TOPK + j) * PAGES_PER_BLOCK + i]
                    s_rows.append(jnp.sum(kt * q_col, axis=0, keepdims=True))
            s_sel = jnp.concatenate(s_rows, axis=0)
            k_pos = blk_of_row * MOBA_BLOCK + (srow % PAGES_PER_BLOCK) * PAGE_SIZE + lane
            s_sel = s_sel - slope * ((past_len + tq) - k_pos).astype(F32)
            rel_own = tq - own_pos
            s_own = jnp.sum(knt * q_col, axis=0, keepdims=True)
            s_own = jnp.where(rel_own >= 0, s_own - slope * rel_own.astype(F32), NEG_INF)
            m = jnp.maximum(jnp.max(jnp.max(s_sel, axis=1, keepdims=True), axis=0, keepdims=True),
                            jnp.max(s_own, axis=1, keepdims=True))
            p_sel = jnp.exp(s_sel - m)
            p_own = jnp.exp(s_own - m)
            l = (jnp.sum(jnp.sum(p_sel, axis=1, keepdims=True), axis=0, keepdims=True)
                 + jnp.sum(p_own, axis=1, keepdims=True))
            acc = jnp.zeros((HEAD_DIM, PAGE_SIZE), F32)
            for r in range(slabs):
                acc = acc + vbuf_ref[slot, tq * slabs + r] * p_sel[r:r + 1, :]
            o_col = (jnp.sum(acc, axis=1, keepdims=True) + jnp.sum(vnt * p_own, axis=1, keepdims=True)) / l
            out = jnp.where(out_lane == tq, o_col, out)
        o_ref[0, h] = out

    return ring_step


def _split_ada(ada_rows):
    a = ada_rows.reshape(ada_rows.shape[0], 3, 3, 1, D_MODEL)
    return {"sh0": a[:, 0, 0], "sc0": a[:, 0, 1], "g0": a[:, 0, 2],
            "sh1": a[:, 1, 0], "sc1": a[:, 1, 1], "g1": a[:, 1, 2],
            "sh2": a[:, 2, 0], "sc2": a[:, 2, 1], "g2": a[:, 2, 2]}


def _heads(t2, n_seq, s):
    return t2.reshape(n_seq, s, N_HEADS, HEAD_DIM)


def kernel(x_prompt, x_sample, cache_k, cache_v, state_conv, page_table, c_prompt, c_sample,
           w_ada, b_ada, ffn1_wg, ffn1_wu, ffn1_wd, w_in, w_dw, b_dw, gn_g, gn_b,
           beta_attn, beta_conv, w_out, ffn2_wg, ffn2_wu, ffn2_wd, ln_g, ln_b):
    depth = w_ada.shape[0]
    alpha = (2.0 * depth) ** 0.25
    batch, seq, _ = x_prompt.shape
    dec_batch, dec_seq, _ = x_sample.shape
    n_pages = page_table.shape[1]
    past_len = n_pages * PAGE_SIZE
    assert past_len % MOBA_BLOCK == 0 and dec_seq <= MOBA_BLOCK
    alibi = 2.0 ** (-8.0 * np.arange(1, N_HEADS + 1) / N_HEADS)
    slopes = jnp.asarray(alibi, F32)
    slopes_log2 = jnp.asarray(alibi * LOG2E, F32)
    page_table_flat = page_table.reshape(-1)

    y_p, y_s = x_prompt, x_sample
    outs = {name: [] for name in ("kp", "vp", "cp", "ks", "vs", "cs")}
    for l in range(depth):
        ada = _ada(jnp.concatenate([c_prompt, c_sample], axis=0), w_ada[l], b_ada[l])
        ada_p, ada_s = _split_ada(ada[:batch]), _split_ada(ada[batch:])
        lng = [ln_g[l, i].reshape(1, D_MODEL) for i in range(3)]
        lnb = [ln_b[l, i].reshape(1, D_MODEL) for i in range(3)]
        w1 = (ffn1_wg[l].astype(BF16), ffn1_wu[l].astype(BF16), ffn1_wd[l].astype(BF16))
        w2 = (ffn2_wg[l].astype(BF16), ffn2_wu[l].astype(BF16), ffn2_wd[l].astype(BF16))
        win16, wo16 = w_in[l].astype(BF16), w_out[l].astype(BF16)
        beta_a = beta_attn[l].reshape(1, ATTN_WIDTH)
        beta_c = beta_conv[l].reshape(1, CONV_CH)

        tile_p = dict(alpha=alpha, nseq_blk=1, rows_blk=512)
        tile_s = dict(alpha=alpha, nseq_blk=dec_batch, rows_blk=dec_seq)
        pool_kt = jnp.swapaxes(cache_k[l], -1, -2)
        pool_vt = jnp.swapaxes(cache_v[l], -1, -2)

        x1_s, q, k, v, glu_s = _stage_a(y_s, ada_s, lng[0], lnb[0], *w1, win16, paged=False, **tile_s)
        q4, k4, v4 = (_heads(t2, dec_batch, dec_seq) for t2 in (q, k, v))

        x1, qt, k16, kmean, ktp, vtp, vt16, glu, kmean_t = _stage_a(
            y_p, ada_p, lng[0], lnb[0], *w1, win16, paged=True, page_table_flat=page_table_flat,
            pool_kt=pool_kt, cache_seqs=dec_batch, **tile_p)
        idx = _topk_blocks(q.reshape(dec_batch, dec_seq, ATTN_WIDTH), kmean_t)[..., :MOBA_TOPK]
        attn = _moba_prompt(qt, k16.reshape(batch, seq, ATTN_WIDTH), vt16, kmean, slopes_log2)
        glu3 = glu.reshape(batch, seq, CONV_CH)
        conv = _conv_branch(jnp.zeros((batch, HIST, CONV_CH), F32), glu3, w_dw[l], b_dw[l],
                            gn_g[l], gn_b[l], nb=1, ch=256)
        to_t = lambda a4: a4.transpose(0, 2, 3, 1)
        sample = dict(qt=to_t(q4), knt=to_t(k4), vnt=to_t(v4), pool_kt=pool_kt, pool_vt=pool_vt,
                      idx_flat=idx.reshape(-1), page_table_flat=page_table_flat, slopes=slopes,
                      past_len=past_len, n_pages=n_pages)
        y_p, attn_t = _stage_c(x1, attn.reshape(batch * seq, ATTN_WIDTH),
                               conv.reshape(batch * seq, CONV_CH), ada_p, lng[1], lnb[1], lng[2], lnb[2],
                               beta_a, beta_c, wo16, *w2, sample=sample, **tile_p)
        outs["kp"].append(jnp.swapaxes(ktp, -1, -2))
        outs["vp"].append(jnp.swapaxes(vtp, -1, -2))
        outs["cp"].append(glu3[:, seq - HIST:, :])

        x1, glu = x1_s, glu_s
        attn = attn_t.transpose(0, 3, 1, 2).reshape(dec_batch * dec_seq, ATTN_WIDTH)
        glu3 = glu.reshape(dec_batch, dec_seq, CONV_CH)
        hist = state_conv[l]
        conv = _conv_branch(hist, glu3, w_dw[l], b_dw[l], gn_g[l], gn_b[l], nb=dec_batch, ch=dec_seq)
        y_s = _stage_c(x1, attn, conv.reshape(dec_batch * dec_seq, CONV_CH),
                       ada_s, lng[1], lnb[1], lng[2], lnb[2], beta_a, beta_c, wo16, *w2, **tile_s)
        outs["ks"].append(k4.transpose(0, 2, 1, 3))
        outs["vs"].append(v4.transpose(0, 2, 1, 3))
        outs["cs"].append(jnp.concatenate([hist, glu3], axis=1)[:, dec_seq:, :])

    stack = lambda name: jnp.stack(outs[name], 0)
    return (y_p, y_s, stack("kp"), stack("vp"), stack("cp"), stack("ks"), stack("vs"), stack("cs"))
```

```python
import functools

import numpy as np
import jax
import jax.numpy as jnp
from jax import lax
from jax.experimental import pallas as pl
from jax.experimental.pallas import tpu as pltpu

F32 = jnp.float32
BF16 = jnp.bfloat16
HIGHEST = lax.Precision.HIGHEST

D_MODEL = 1024
D_FF = 2816
N_HEADS = 8
HEAD_DIM = 64
ATTN_WIDTH = N_HEADS * HEAD_DIM
CONV_CH = 512
W_IN_COLS = 3 * ATTN_WIDTH + 2 * CONV_CH
CONV_K = 31
HIST = CONV_K - 1
GN_GROUPS = 8
MOBA_BLOCK = 256
MOBA_TOPK = 3
PAGE_SIZE = 128
PAGES_PER_BLOCK = MOBA_BLOCK // PAGE_SIZE
LN_EPS = 1e-5
NEG_INF = -1e30
SCALE = HEAD_DIM ** -0.5

LANES = 128
SUBLANES = 8
HEADS_PER_GROUP = LANES // HEAD_DIM
CONV_PAD_ROWS = 32
LOG2E = 1.4426950408889634
VMEM_LIMIT = 60 * 1024 * 1024

FF_CHUNKS = ((0, 1024), (1024, 2048), (2048, D_FF))


def _silu(x):
    return x / (1.0 + jnp.exp(-x))


def _sigmoid(x):
    return 1.0 / (1.0 + jnp.exp(-x))


def _layernorm(t, g, b):
    mu = jnp.mean(t, axis=-1, keepdims=True)
    d = t - mu
    var = jnp.mean(d * d, axis=-1, keepdims=True)
    return d * lax.rsqrt(var + LN_EPS) * g + b


def _ffn(h, wg_ref, wu_ref, wd_ref, chunks=FF_CHUNKS, before_up=None, before_down=None):
    acc = None
    for ci, (lo, hi) in enumerate(chunks):
        if before_up is not None:
            before_up(ci)
        g = jnp.dot(h, wg_ref[:, lo:hi], preferred_element_type=F32)
        u = jnp.dot(h, wu_ref[:, lo:hi], preferred_element_type=F32)
        a = (_silu(g) * u).astype(BF16)
        if before_down is not None:
            before_down(ci)
        y = jnp.dot(a, wd_ref[lo:hi, :], preferred_element_type=F32)
        acc = y if acc is None else acc + y
    return acc


def _ada_kernel(c_ref, w_ref, b_ref, o_ref):
    a = _silu(c_ref[...])
    o_ref[...] = jnp.dot(a, w_ref[...], preferred_element_type=F32, precision=HIGHEST) + b_ref[...]


def _ada(c_all, w_ada, b_ada):
    n = c_all.shape[0]
    tn = 1024
    return pl.pallas_call(
        _ada_kernel,
        grid=(w_ada.shape[1] // tn,),
        in_specs=[
            pl.BlockSpec((n, D_MODEL), lambda j: (0, 0)),
            pl.BlockSpec((D_MODEL, tn), lambda j: (0, j)),
            pl.BlockSpec((1, tn), lambda j: (0, j)),
        ],
        out_specs=pl.BlockSpec((n, tn), lambda j: (0, j)),
        out_shape=jax.ShapeDtypeStruct((n, w_ada.shape[1]), F32),
        name="ada",
    )(c_all, w_ada, b_ada.reshape(1, -1))


STREAM_CHUNK_PAGES = 8
STREAM_SLOTS = 4
STREAM_FF_CHUNKS = ((0, 512), (512, 1024), (1024, 1536), (1536, 2048), (2048, 2560), (2560, D_FF))
W_IN_PARTS = ((0, ATTN_WIDTH), (ATTN_WIDTH, 2 * ATTN_WIDTH), (2 * ATTN_WIDTH, 3 * ATTN_WIDTH),
              (3 * ATTN_WIDTH, W_IN_COLS))
STREAM_CHUNKS_PER_STEP = 2 * len(STREAM_FF_CHUNKS) + len(W_IN_PARTS)


def _block_mean_stream(step, n_steps, pt_ref, pool_ref, kmt_ref, ring_ref, sem_ref):
    cps = STREAM_CHUNKS_PER_STEP
    ahead = STREAM_SLOTS - 1
    blocks_per_chunk = STREAM_CHUNK_PAGES // PAGES_PER_BLOCK
    n_cols = kmt_ref.shape[3]
    assert cps % STREAM_SLOTS == 0 and cps * blocks_per_chunk == n_cols and ahead < cps

    def page_copy(page, slot, i):
        return pltpu.make_async_copy(pool_ref.at[page], ring_ref.at[slot, i], sem_ref.at[slot])

    def start_chunk(g, slot):
        for i in range(STREAM_CHUNK_PAGES):
            page_copy(pt_ref[g * STREAM_CHUNK_PAGES + i], slot, i).start()

    def ring_step(c):
        g = step * cps + c
        slot = c % STREAM_SLOTS
        if c == 0:
            @pl.when(step == 0)
            def _():
                for a in range(ahead):
                    start_chunk(a, a)
        if c + ahead < cps:
            start_chunk(g + ahead, (c + ahead) % STREAM_SLOTS)
        else:
            @pl.when(step + 1 < n_steps)
            def _():
                start_chunk(g + ahead, (c + ahead) % STREAM_SLOTS)
        for i in range(STREAM_CHUNK_PAGES):
            page_copy(0, slot, i).wait()
        col_id = lax.broadcasted_iota(jnp.int32, (HEAD_DIM, n_cols), 1)
        cols = slice(c * blocks_per_chunk, (c + 1) * blocks_per_chunk)
        for h in range(N_HEADS):
            acc = jnp.zeros((HEAD_DIM, n_cols), F32)
            for jb in range(blocks_per_chunk):
                tot = ring_ref[slot, jb * PAGES_PER_BLOCK, h]
                for i in range(1, PAGES_PER_BLOCK):
                    tot = tot + ring_ref[slot, jb * PAGES_PER_BLOCK + i, h]
                mean = jnp.sum(tot, axis=1, keepdims=True) * (1.0 / MOBA_BLOCK)
                acc = jnp.where(col_id == c * blocks_per_chunk + jb, mean, acc)
            kmt_ref[0, h, :, cols] = acc[:, cols]

    return ring_step


def _stage_a_kernel(alpha, paged, *refs):
    (x_ref, sh0_ref, sc0_ref, g0_ref, sh1_ref, sc1_ref, lng_ref, lnb_ref,
     wg_ref, wu_ref, wd_ref, win_ref) = refs[:12]
    if paged:
        pt_ref, pool_ref, x1_ref = refs[12:15]
        out_refs = refs[15:22]
        kmt_ref, ring_ref, sem_ref = refs[22:25]
        step = pl.program_id(0) * pl.num_programs(1) + pl.program_id(1)
        ring_step = _block_mean_stream(step, pl.num_programs(0) * pl.num_programs(1),
                                       pt_ref, pool_ref, kmt_ref, ring_ref, sem_ref)
        ff_chunks = STREAM_FF_CHUNKS
    else:
        x1_ref = refs[12]
        out_refs = refs[13:]
        ring_step = None
        ff_chunks = FF_CHUNKS
    nseq, rows, d = x_ref.shape
    m = nseq * rows
    x = x_ref[...]
    h0 = (x * (1.0 + sc0_ref[...]) + sh0_ref[...]).reshape(m, d).astype(BF16)
    if paged:
        y = _ffn(h0, wg_ref, wu_ref, wd_ref, ff_chunks,
                 lambda ci: ring_step(2 * ci), lambda ci: ring_step(2 * ci + 1))
    else:
        y = _ffn(h0, wg_ref, wu_ref, wd_ref, ff_chunks)
    y = y.reshape(nseq, rows, d)
    x1 = _layernorm(alpha * x + g0_ref[...] * (0.5 * y), lng_ref[...], lnb_ref[...])
    x1_ref[...] = x1
    h1 = (x1 * (1.0 + sc1_ref[...]) + sh1_ref[...]).reshape(m, d).astype(BF16)
    parts = []
    for pi, (lo, hi) in enumerate(W_IN_PARTS):
        if paged:
            ring_step(2 * len(ff_chunks) + pi)
        parts.append(jnp.dot(h1, win_ref[:, lo:hi], preferred_element_type=F32))
    q, k, v, ug = parts
    ga = ug[:, 0:CONV_CH]
    gb = ug[:, CONV_CH:]
    glu = ga * _sigmoid(gb)
    if paged:
        qt_ref, k16_ref, kmean_ref, ktp_ref, vtp_ref, vt16_ref, glu_ref = out_refs
        qt_ref[0] = q.T
        k16_ref[...] = k.astype(BF16)
        blocks_per_tile = m // MOBA_BLOCK
        means = [jnp.sum(k[n * MOBA_BLOCK:(n + 1) * MOBA_BLOCK, :], axis=0, keepdims=True)
                 * (1.0 / MOBA_BLOCK) for n in range(blocks_per_tile)]
        tile = pl.program_id(1)
        for jj in range(kmean_ref.shape[1] // blocks_per_tile):
            @pl.when(tile == jj)
            def _():
                for n in range(blocks_per_tile):
                    kmean_ref[0, jj * blocks_per_tile + n:jj * blocks_per_tile + n + 1, :] = means[n]
        kt = k.T
        vt = v.T
        for p in range(m // PAGE_SIZE):
            for h in range(N_HEADS):
                rs = slice(h * HEAD_DIM, (h + 1) * HEAD_DIM)
                cs = slice(p * PAGE_SIZE, (p + 1) * PAGE_SIZE)
                ktp_ref[0, p, h] = kt[rs, cs]
                vtp_ref[0, p, h] = vt[rs, cs]
                vt16_ref[0, p, h] = vt[rs, cs].astype(BF16)
    else:
        q_ref, k_ref, v_ref, glu_ref = out_refs
        q_ref[...] = q
        k_ref[...] = k
        v_ref[...] = v
    glu_ref[...] = glu


def _const_spec(shape):
    return pl.BlockSpec(shape, lambda i, j: (0,) * len(shape), pipeline_mode=pl.Buffered(1))


def _stage_a(x, ada, ln_g, ln_b, wg, wu, wd, win, *, alpha, nseq_blk, rows_blk, paged,
             page_table_flat=None, pool_kt=None, cache_seqs=None):
    n_seq, s, _ = x.shape
    tiles_per_seq = s // rows_blk
    assert nseq_blk == 1 or (nseq_blk == n_seq and tiles_per_seq == 1)
    m_blk = nseq_blk * rows_blk
    n_tok = n_seq * s
    x_spec = pl.BlockSpec((nseq_blk, rows_blk, D_MODEL), lambda i, j: (i, j, 0))
    a_spec = pl.BlockSpec((nseq_blk, 1, D_MODEL), lambda i, j: (i, 0, 0))
    flat_spec = pl.BlockSpec((m_blk, ATTN_WIDTH), lambda i, j: (i * tiles_per_seq + j, 0))
    flat_shape = jax.ShapeDtypeStruct((n_tok, ATTN_WIDTH), F32)
    if paged:
        assert nseq_blk == 1 and rows_blk % MOBA_BLOCK == 0 and MOBA_BLOCK % PAGE_SIZE == 0
        ppt = rows_blk // PAGE_SIZE
        page_dims = (n_seq, s // PAGE_SIZE, N_HEADS, HEAD_DIM, PAGE_SIZE)
        page_spec = pl.BlockSpec((1, ppt, N_HEADS, HEAD_DIM, PAGE_SIZE), lambda i, j: (i, j, 0, 0, 0))
        page_shape = jax.ShapeDtypeStruct(page_dims, F32)
        out_specs = [x_spec, pl.BlockSpec((1, ATTN_WIDTH, rows_blk), lambda i, j: (i, 0, j)),
                     flat_spec,
                     pl.BlockSpec((1, s // MOBA_BLOCK, ATTN_WIDTH), lambda i, j: (i, 0, 0)),
                     page_spec, page_spec, page_spec, flat_spec]
        out_shape = [jax.ShapeDtypeStruct((n_seq, s, D_MODEL), F32),
                     jax.ShapeDtypeStruct((n_seq, ATTN_WIDTH, s), F32),
                     jax.ShapeDtypeStruct((n_tok, ATTN_WIDTH), BF16),
                     jax.ShapeDtypeStruct((n_seq, s // MOBA_BLOCK, ATTN_WIDTH), F32),
                     page_shape, page_shape, jax.ShapeDtypeStruct(page_dims, BF16), flat_shape]
        n_steps = n_seq * tiles_per_seq
        pages_per_seq = page_table_flat.shape[0] // cache_seqs
        assert cache_seqs == n_steps and pages_per_seq == STREAM_CHUNKS_PER_STEP * STREAM_CHUNK_PAGES
        blocks_per_seq = pages_per_seq // PAGES_PER_BLOCK
        extra_in = [page_table_flat, pool_kt]
        extra_in_specs = [pl.BlockSpec(memory_space=pltpu.SMEM), pl.BlockSpec(memory_space=pl.ANY)]
        out_specs.append(pl.BlockSpec((1, N_HEADS, HEAD_DIM, blocks_per_seq),
                                      lambda i, j: (i * tiles_per_seq + j, 0, 0, 0)))
        out_shape.append(jax.ShapeDtypeStruct((cache_seqs, N_HEADS, HEAD_DIM, blocks_per_seq), F32))
        scratch = [pltpu.VMEM((STREAM_SLOTS, STREAM_CHUNK_PAGES, N_HEADS, HEAD_DIM, PAGE_SIZE), F32),
                   pltpu.SemaphoreType.DMA((STREAM_SLOTS,))]
    else:
        out_specs = [x_spec, flat_spec, flat_spec, flat_spec, flat_spec]
        out_shape = [jax.ShapeDtypeStruct((n_seq, s, D_MODEL), F32)] + [flat_shape] * 4
        extra_in, extra_in_specs, scratch = [], [], []
    return pl.pallas_call(
        functools.partial(_stage_a_kernel, alpha, paged),
        grid=(n_seq // nseq_blk, tiles_per_seq),
        in_specs=[x_spec, a_spec, a_spec, a_spec, a_spec, a_spec,
                  _const_spec((1, D_MODEL)), _const_spec((1, D_MODEL)),
                  _const_spec((D_MODEL, D_FF)), _const_spec((D_MODEL, D_FF)),
                  _const_spec((D_FF, D_MODEL)), _const_spec((D_MODEL, W_IN_COLS))] + extra_in_specs,
        out_specs=out_specs,
        out_shape=out_shape,
        scratch_shapes=scratch,
        compiler_params=pltpu.CompilerParams(
            dimension_semantics=("arbitrary", "arbitrary"), vmem_limit_bytes=VMEM_LIMIT),
        name="stage_a",
    )(x, ada["sh0"], ada["sc0"], ada["g0"], ada["sh1"], ada["sc1"], ln_g, ln_b, wg, wu, wd, win, *extra_in)


SAMPLE_GATHER_SLOTS = 3


def _stage_c_kernel(alpha, sample_cfg, *refs):
    (x1_ref, attn_ref, conv_ref, g1_ref, sh2_ref, sc2_ref, g2_ref, lng1_ref, lnb1_ref, lng2_ref, lnb2_ref,
     ba_ref, bc_ref, wo_ref, wg_ref, wu_ref, wd_ref) = refs[:17]
    if sample_cfg is not None:
        past_len, n_pages = sample_cfg
        y_ref, so_ref = refs[25:27]
        step = pl.program_id(0) * pl.num_programs(1) + pl.program_id(1)
        ring_step = _sample_attention_stream(step, pl.num_programs(0) * pl.num_programs(1), past_len,
                                             n_pages, *refs[17:25], so_ref, *refs[27:30])
        ff_chunks = STREAM_FF_CHUNKS
        assert len(ff_chunks) + 2 == N_HEADS
        before_chunk = lambda ci: ring_step(ci + 1)
        ring_step(0)
    else:
        y_ref = refs[17]
        ring_step, before_chunk, ff_chunks = None, None, FF_CHUNKS
    nseq, rows, d = x1_ref.shape
    m = nseq * rows
    a = (attn_ref[...] * ba_ref[...]).astype(BF16)
    c = (conv_ref[...] * bc_ref[...]).astype(BF16)
    mix = (jnp.dot(a, wo_ref[0:ATTN_WIDTH, :], preferred_element_type=F32)
           + jnp.dot(c, wo_ref[ATTN_WIDTH:, :], preferred_element_type=F32)).reshape(nseq, rows, d)
    x2 = _layernorm(alpha * x1_ref[...] + g1_ref[...] * mix, lng1_ref[...], lnb1_ref[...])
    h2 = (x2 * (1.0 + sc2_ref[...]) + sh2_ref[...]).reshape(m, d).astype(BF16)
    y = _ffn(h2, wg_ref, wu_ref, wd_ref, ff_chunks, before_chunk).reshape(nseq, rows, d)
    if ring_step is not None:
        ring_step(N_HEADS - 1)
    y_ref[...] = _layernorm(alpha * x2 + g2_ref[...] * (0.5 * y), lng2_ref[...], lnb2_ref[...])


def _stage_c(x1, attn, conv, ada, ln_g1, ln_b1, ln_g2, ln_b2, beta_a, beta_c, wo, wg, wu, wd,
             *, alpha, nseq_blk, rows_blk, sample=None):
    n_seq, s, _ = x1.shape
    tiles_per_seq = s // rows_blk
    assert nseq_blk == 1 or (nseq_blk == n_seq and tiles_per_seq == 1)
    m_blk = nseq_blk * rows_blk
    x_spec = pl.BlockSpec((nseq_blk, rows_blk, D_MODEL), lambda i, j: (i, j, 0))
    a_spec = pl.BlockSpec((nseq_blk, 1, D_MODEL), lambda i, j: (i, 0, 0))
    flat = pl.BlockSpec((m_blk, ATTN_WIDTH), lambda i, j: (i * tiles_per_seq + j, 0))
    out_specs = [x_spec]
    out_shape = [jax.ShapeDtypeStruct((n_seq, s, D_MODEL), F32)]
    extra_in, extra_in_specs, scratch, cfg = [], [], [], None
    if sample is not None:
        seqs, _, _, t = sample["qt"].shape
        assert seqs == n_seq * tiles_per_seq
        n_k = t * MOBA_TOPK * PAGES_PER_BLOCK
        cfg = (sample["past_len"], sample["n_pages"])
        smem = pl.BlockSpec(memory_space=pltpu.SMEM)
        hbm = pl.BlockSpec(memory_space=pl.ANY)
        seq_spec = pl.BlockSpec((1, N_HEADS, HEAD_DIM, t), lambda i, j: (i * tiles_per_seq + j, 0, 0, 0))
        extra_in = [sample["idx_flat"], sample["page_table_flat"], sample["slopes"],
                    sample["qt"], sample["knt"], sample["vnt"], sample["pool_kt"], sample["pool_vt"]]
        extra_in_specs = [smem, smem, smem, seq_spec, seq_spec, seq_spec, hbm, hbm]
        out_specs.append(seq_spec)
        out_shape.append(jax.ShapeDtypeStruct((seqs, N_HEADS, HEAD_DIM, t), F32))
        scratch = [pltpu.VMEM((SAMPLE_GATHER_SLOTS, n_k, HEAD_DIM, PAGE_SIZE), F32),
                   pltpu.VMEM((SAMPLE_GATHER_SLOTS, n_k, HEAD_DIM, PAGE_SIZE), F32),
                   pltpu.SemaphoreType.DMA((SAMPLE_GATHER_SLOTS,))]
    res = pl.pallas_call(
        functools.partial(_stage_c_kernel, alpha, cfg),
        grid=(n_seq // nseq_blk, tiles_per_seq),
        in_specs=[x_spec, flat, flat, a_spec, a_spec, a_spec, a_spec,
                  _const_spec((1, D_MODEL)), _const_spec((1, D_MODEL)),
                  _const_spec((1, D_MODEL)), _const_spec((1, D_MODEL)),
                  _const_spec((1, ATTN_WIDTH)), _const_spec((1, CONV_CH)),
                  _const_spec((D_MODEL, D_MODEL)),
                  _const_spec((D_MODEL, D_FF)), _const_spec((D_MODEL, D_FF)),
                  _const_spec((D_FF, D_MODEL))] + extra_in_specs,
        out_specs=out_specs,
        out_shape=out_shape,
        scratch_shapes=scratch,
        compiler_params=pltpu.CompilerParams(
            dimension_semantics=("arbitrary", "arbitrary"), vmem_limit_bytes=VMEM_LIMIT),
        name="stage_c",
    )(x1, attn, conv, ada["g1"], ada["sh2"], ada["sc2"], ada["g2"],
      ln_g1, ln_b1, ln_g2, ln_b2, beta_a, beta_c, wo, wg, wu, wd, *extra_in)
    return res if sample is not None else res[0]


def _conv_kernel(n_chunks, hist_ref, prev_ref, cur_ref, w_ref, bdw_ref, gng_ref, gnb_ref, gavg_ref,
                 o_ref, buf_ref):
    nb, ch, c = cur_ref.shape
    if n_chunks == 1:
        halo = hist_ref[...]
    else:
        halo = jnp.where(pl.program_id(1) == 0, hist_ref[...], prev_ref[:, ch - HIST:, :])
    pad = CONV_PAD_ROWS
    buf_ref[:, 0:pad - HIST, :] = jnp.zeros((nb, pad - HIST, c), F32)
    buf_ref[:, pad - HIST:pad, :] = halo
    buf_ref[:, pad:pad + ch, :] = cur_ref[...]
    buf_ref[:, pad + ch:, :] = jnp.zeros((nb, SUBLANES, c), F32)
    acc = None
    for r in range(SUBLANES):
        part = None
        for a in range((pad + SUBLANES) // SUBLANES):
            j = SUBLANES * a + r - (pad - HIST)
            if 0 <= j < CONV_K:
                term = buf_ref[:, SUBLANES * a:SUBLANES * a + ch + SUBLANES, :] * w_ref[j:j + 1, :]
                part = term if part is None else part + term
        shifted = part[:, r:r + ch, :]
        acc = shifted if acc is None else acc + shifted
    y = (acc + bdw_ref[...]).reshape(nb * ch, c)
    gavg = gavg_ref[...]

    def group_mean(t):
        hi = t.astype(BF16)
        lo = (t - hi.astype(F32)).astype(BF16)
        return (jnp.dot(hi, gavg, preferred_element_type=F32)
                + jnp.dot(lo, gavg, preferred_element_type=F32))

    mu = group_mean(y)
    dlt = y - mu
    var = group_mean(dlt * dlt)
    z = dlt * lax.rsqrt(var + LN_EPS) * gng_ref[...] + gnb_ref[...]
    o_ref[...] = _silu(z).reshape(nb, ch, c)


def _conv_branch(hist, glu3, w_dw, b_dw, gn_g, gn_b, *, nb, ch):
    n_seq, s, c = glu3.shape
    n_chunks = s // ch
    assert ch >= HIST or n_chunks == 1
    grp = np.arange(c) // (c // GN_GROUPS)
    gavg = jnp.asarray((grp[:, None] == grp[None, :]).astype(np.float32) / (c // GN_GROUPS), BF16)
    cur_spec = pl.BlockSpec((nb, ch, c), lambda i, j: (i, j, 0))
    prev_spec = pl.BlockSpec((nb, ch, c), lambda i, j: (i, jnp.maximum(j - 1, 0), 0))
    return pl.pallas_call(
        functools.partial(_conv_kernel, n_chunks),
        grid=(n_seq // nb, n_chunks),
        in_specs=[pl.BlockSpec((nb, HIST, c), lambda i, j: (i, 0, 0)), prev_spec, cur_spec,
                  pl.BlockSpec((CONV_K, c), lambda i, j: (0, 0)),
                  pl.BlockSpec((1, c), lambda i, j: (0, 0)),
                  pl.BlockSpec((1, c), lambda i, j: (0, 0)),
                  pl.BlockSpec((1, c), lambda i, j: (0, 0)),
                  pl.BlockSpec((c, c), lambda i, j: (0, 0))],
        out_specs=cur_spec,
        out_shape=jax.ShapeDtypeStruct((n_seq, s, c), F32),
        scratch_shapes=[pltpu.VMEM((nb, CONV_PAD_ROWS + ch + SUBLANES, c), F32)],
        compiler_params=pltpu.CompilerParams(dimension_semantics=("arbitrary", "arbitrary")),
        name="conv_branch",
    )(hist, glu3, glu3, w_dw, b_dw.reshape(1, c), gn_g.reshape(1, c), gn_b.reshape(1, c), gavg)


MOBA_HEADS_PER_STEP = 8


def _moba_prompt_kernel(slopes_ref, qt_ref, k16_ref, vt16_ref, kmean_ref, o_ref,
                        q16_ref, bias_ref, krs_ref, s_ref, m_ref, l_ref, acc_ref):
    grp0 = pl.program_id(1)
    qb = pl.program_id(2)
    blk = MOBA_BLOCK
    nb = kmean_ref.shape[1]
    hps = MOBA_HEADS_PER_STEP
    row_head = lax.broadcasted_iota(jnp.int32, (LANES, blk), 0) // HEAD_DIM
    key_i = lax.broadcasted_iota(jnp.int32, (blk, blk), 0)
    qry_i = lax.broadcasted_iota(jnp.int32, (blk, blk), 1)
    key_f = key_i.astype(F32)
    blk_id = lax.broadcasted_iota(jnp.int32, (nb, blk), 0)
    past = blk_id < qb
    zero_row = jnp.zeros((1, blk), jnp.int32)

    def lanes_of(hh):
        g = hh // HEADS_PER_GROUP
        return slice(g * LANES, (g + 1) * LANES)

    def scores(hh, n):
        kb = k16_ref[0, pl.ds(pl.multiple_of(n * blk, blk), blk), lanes_of(hh)]
        return jnp.dot(kb, q16_ref[hh], preferred_element_type=F32) + krs_ref[hh]

    def pv(hh, n, p):
        p16 = p.astype(BF16)
        acc = None
        for i in range(PAGES_PER_BLOCK):
            vt = vt16_ref[0, n * PAGES_PER_BLOCK + i, hh]
            part = jnp.dot(vt, p16[i * PAGE_SIZE:(i + 1) * PAGE_SIZE, :], preferred_element_type=F32)
            acc = part if acc is None else acc + part
        return acc

    for hh in range(hps):
        slope2 = slopes_ref[grp0 * hps + hh]
        sub = hh % HEADS_PER_GROUP
        qh = jnp.where(row_head == sub, qt_ref[0, lanes_of(hh), :], 0.0)
        q16_ref[hh] = (qh * (SCALE * LOG2E)).astype(BF16)
        krs_ref[hh] = key_f * slope2

        gate = jnp.dot(kmean_ref[0, :, lanes_of(hh)], qh, preferred_element_type=F32, precision=HIGHEST)
        for n in range(nb):
            row = gate[n:n + 1, :]
            beats = ((gate > row) | ((gate == row) & (blk_id < n))) & past
            cnt = jnp.sum(jnp.where(beats, 1.0, 0.0), axis=0, keepdims=True)
            bias_ref[hh, n:n + 1, :] = jnp.where((cnt < MOBA_TOPK) & (n < qb), 0.0, NEG_INF)

        s = jnp.where(key_i <= qry_i, scores(hh, qb), NEG_INF)
        s_ref[hh, qb] = s
        m_ref[hh] = jnp.max(s, axis=0, keepdims=True)

    def block_const(hh, n):
        rel = ((n - qb) * blk + zero_row).astype(F32)
        return slopes_ref[grp0 * hps + hh] * rel + bias_ref[hh, pl.ds(n, 1), :]

    def pass1(n, carry):
        for hh in range(hps):
            s = scores(hh, n)
            s_ref[hh, n] = s
            m_ref[hh] = jnp.maximum(m_ref[hh], jnp.max(s, axis=0, keepdims=True) + block_const(hh, n))
        return carry

    lax.fori_loop(0, qb, pass1, 0)

    def block_out(hh, n, c):
        p = jnp.exp2(s_ref[hh, n] - (m_ref[hh] - c))
        return jnp.sum(p, axis=0, keepdims=True), pv(hh, n, p)

    for hh in range(hps):
        l_ref[hh], acc_ref[hh] = block_out(hh, qb, 0.0)

    def pass2(n, carry):
        for hh in range(hps):
            l_part, acc_part = block_out(hh, n, block_const(hh, n))
            l_ref[hh] = l_ref[hh] + l_part
            acc_ref[hh] = acc_ref[hh] + acc_part
        return carry

    lax.fori_loop(0, qb, pass2, 0)
    out_t = jnp.concatenate([acc_ref[hh] / l_ref[hh] for hh in range(hps)], axis=0)
    o_ref[0] = out_t.T


def _moba_prompt(qt3, k16, vt16, kmean, slopes):
    b, s, _ = k16.shape
    hps = MOBA_HEADS_PER_STEP
    width = hps * HEAD_DIM
    n_blocks = s // MOBA_BLOCK
    return pl.pallas_call(
        _moba_prompt_kernel,
        grid=(b, N_HEADS // hps, n_blocks),
        in_specs=[pl.BlockSpec(memory_space=pltpu.SMEM),
                  pl.BlockSpec((1, width, MOBA_BLOCK), lambda i, g, j: (i, g, j)),
                  pl.BlockSpec((1, s, width), lambda i, g, j: (i, 0, g)),
                  pl.BlockSpec((1, s // PAGE_SIZE, hps, HEAD_DIM, PAGE_SIZE),
                               lambda i, g, j: (i, 0, g, 0, 0)),
                  pl.BlockSpec((1, n_blocks, width), lambda i, g, j: (i, 0, g))],
        out_specs=pl.BlockSpec((1, MOBA_BLOCK, width), lambda i, g, j: (i, j, g)),
        out_shape=jax.ShapeDtypeStruct((b, s, ATTN_WIDTH), F32),
        scratch_shapes=[pltpu.VMEM((hps, LANES, MOBA_BLOCK), BF16),
                        pltpu.VMEM((hps, n_blocks, MOBA_BLOCK), F32),
                        pltpu.VMEM((hps, MOBA_BLOCK, MOBA_BLOCK), F32),
                        pltpu.VMEM((hps, n_blocks, MOBA_BLOCK, MOBA_BLOCK), F32),
                        pltpu.VMEM((hps, 1, MOBA_BLOCK), F32),
                        pltpu.VMEM((hps, 1, MOBA_BLOCK), F32),
                        pltpu.VMEM((hps, HEAD_DIM, MOBA_BLOCK), F32)],
        compiler_params=pltpu.CompilerParams(
            dimension_semantics=("arbitrary", "arbitrary", "arbitrary"), vmem_limit_bytes=VMEM_LIMIT),
        name="moba_prompt",
    )(slopes, qt3, k16, vt16, kmean)


def _topk_kernel(q_ref, kmt_ref, o_ref):
    t, nbp = q_ref.shape[0], kmt_ref.shape[2]
    q = q_ref[...]
    lane = lax.broadcasted_iota(jnp.int32, (t, nbp), 1).astype(F32)
    out_lane = lax.broadcasted_iota(jnp.int32, (t, LANES), 1)
    for h in range(N_HEADS):
        g = jnp.dot(q[:, h * HEAD_DIM:(h + 1) * HEAD_DIM], kmt_ref[h],
                    preferred_element_type=F32, precision=HIGHEST)
        res = jnp.zeros((t, LANES), F32)
        for r in range(MOBA_TOPK):
            mx = jnp.max(g, axis=1, keepdims=True)
            idx = jnp.min(jnp.where(g == mx, lane, float(nbp)), axis=1, keepdims=True)
            res = jnp.where(out_lane == r, idx, res)
            g = jnp.where(lane == idx, -jnp.inf, g)
        o_ref[h] = res.astype(jnp.int32)


def _topk_blocks(q3, kmean_t):
    n_seq, t, _ = q3.shape
    nbp = kmean_t.shape[3]
    return pl.pallas_call(
        _topk_kernel,
        grid=(n_seq,),
        in_specs=[pl.BlockSpec((None, t, ATTN_WIDTH), lambda b: (b, 0, 0)),
                  pl.BlockSpec((None, N_HEADS, HEAD_DIM, nbp), lambda b: (b, 0, 0, 0))],
        out_specs=pl.BlockSpec((None, N_HEADS, t, LANES), lambda b: (b, 0, 0, 0)),
        out_shape=jax.ShapeDtypeStruct((n_seq, N_HEADS, t, LANES), jnp.int32),
        name="topk_blocks",
    )(q3, kmean_t)


def _sample_attention_stream(step, n_steps, past_len, n_pages, idx_ref, pt_ref, slopes_ref,
                             qt_ref, knt_ref, vnt_ref, pool_k_ref, pool_v_ref, o_ref,
                             kbuf_ref, vbuf_ref, sem_ref):
    t = qt_ref.shape[3]
    slabs = MOBA_TOPK * PAGES_PER_BLOCK
    n_k = t * slabs
    assert N_HEADS % 2 == 0

    def slab_copies(page, head, slot, n):
        return (pltpu.make_async_copy(pool_k_ref.at[page, head], kbuf_ref.at[slot, n], sem_ref.at[slot]),
                pltpu.make_async_copy(pool_v_ref.at[page, head], vbuf_ref.at[slot, n], sem_ref.at[slot]))

    def start_gather(seq, head, slot):
        for n in range(n_k):
            tq, rem = divmod(n, slabs)
            j, i = divmod(rem, PAGES_PER_BLOCK)
            blk_idx = idx_ref[((seq * N_HEADS + head) * t + tq) * MOBA_TOPK + j]
            page = pt_ref[seq * n_pages + blk_idx * PAGES_PER_BLOCK + i]
            for cp in slab_copies(page, head, slot, n):
                cp.start()

    n_slots = kbuf_ref.shape[0]
    ahead = n_slots - 1
    assert ahead < N_HEADS

    def ring_step(h):
        unit = step * N_HEADS + h
        slot = lax.rem(unit, n_slots)
        if h == 0:
            @pl.when(step == 0)
            def _():
                for a in range(ahead):
                    start_gather(0, a, a)
        nxt_slot = lax.rem(unit + ahead, n_slots)
        if h + ahead < N_HEADS:
            start_gather(step, h + ahead, nxt_slot)
        else:
            @pl.when(step + 1 < n_steps)
            def _():
                start_gather(step + 1, h + ahead - N_HEADS, nxt_slot)
        for n in range(n_k):
            for cp in slab_copies(0, 0, slot, n):
                cp.wait()

        slope = slopes_ref[h]
        qt = qt_ref[0, h] * SCALE
        knt = knt_ref[0, h]
        vnt = vnt_ref[0, h]
        srow = lax.broadcasted_iota(jnp.int32, (slabs, PAGE_SIZE), 0)
        lane = lax.broadcasted_iota(jnp.int32, (slabs, PAGE_SIZE), 1)
        own_pos = lax.broadcasted_iota(jnp.int32, (1, t), 1)
        out_lane = lax.broadcasted_iota(jnp.int32, (HEAD_DIM, t), 1)
        out = jnp.zeros((HEAD_DIM, t), F32)
        for tq in range(t):
            q_col = qt[:, tq:tq + 1]
            base = ((step * N_HEADS + h) * t + tq) * MOBA_TOPK
            s_rows = []
            blk_of_row = jnp.zeros((slabs, PAGE_SIZE), jnp.int32)
            for j in range(MOBA_TOPK):
                blk_of_row = jnp.where(srow // PAGES_PER_BLOCK == j, idx_ref[base + j], blk_of_row)
                for i in range(PAGES_PER_BLOCK):
                    kt = kbuf_ref[slot, (tq * MOBA_TOPK + j) * PAGES_PER_BLOCK + i]
                    s_rows.append(jnp.sum(kt * q_col, axis=0, keepdims=True))
            s_sel = jnp.concatenate(s_rows, axis=0)
            k_pos = blk_of_row * MOBA_BLOCK + (srow % PAGES_PER_BLOCK) * PAGE_SIZE + lane
            s_sel = s_sel - slope * ((past_len + tq) - k_pos).astype(F32)
            rel_own = tq - own_pos
            s_own = jnp.sum(knt * q_col, axis=0, keepdims=True)
            s_own = jnp.where(rel_own >= 0, s_own - slope * rel_own.astype(F32), NEG_INF)
            m = jnp.maximum(jnp.max(jnp.max(s_sel, axis=1, keepdims=True), axis=0, keepdims=True),
                            jnp.max(s_own, axis=1, keepdims=True))
            p_sel = jnp.exp(s_sel - m)
            p_own = jnp.exp(s_own - m)
            l = (jnp.sum(jnp.sum(p_sel, axis=1, keepdims=True), axis=0, keepdims=True)
                 + jnp.sum(p_own, axis=1, keepdims=True))
            acc = jnp.zeros((HEAD_DIM, PAGE_SIZE), F32)
            for r in range(slabs):
                acc = acc + vbuf_ref[slot, tq * slabs + r] * p_sel[r:r + 1, :]
            o_col = (jnp.sum(acc, axis=1, keepdims=True) + jnp.sum(vnt * p_own, axis=1, keepdims=True)) / l
            out = jnp.where(out_lane == tq, o_col, out)
        o_ref[0, h] = out

    return ring_step


def _split_ada(ada_rows):
    a = ada_rows.reshape(ada_rows.shape[0], 3, 3, 1, D_MODEL)
    return {"sh0": a[:, 0, 0], "sc0": a[:, 0, 1], "g0": a[:, 0, 2],
            "sh1": a[:, 1, 0], "sc1": a[:, 1, 1], "g1": a[:, 1, 2],
            "sh2": a[:, 2, 0], "sc2": a[:, 2, 1], "g2": a[:, 2, 2]}


def _heads(t2, n_seq, s):
    return t2.reshape(n_seq, s, N_HEADS, HEAD_DIM)


def kernel(x_prompt, x_sample, cache_k, cache_v, state_conv, page_table, c_prompt, c_sample,
           w_ada, b_ada, ffn1_wg, ffn1_wu, ffn1_wd, w_in, w_dw, b_dw, gn_g, gn_b,
           beta_attn, beta_conv, w_out, ffn2_wg, ffn2_wu, ffn2_wd, ln_g, ln_b):
    depth = w_ada.shape[0]
    alpha = (2.0 * depth) ** 0.25
    batch, seq, _ = x_prompt.shape
    dec_batch, dec_seq, _ = x_sample.shape
    n_pages = page_table.shape[1]
    past_len = n_pages * PAGE_SIZE
    assert past_len % MOBA_BLOCK == 0 and dec_seq <= MOBA_BLOCK
    alibi = 2.0 ** (-8.0 * np.arange(1, N_HEADS + 1) / N_HEADS)
    slopes = jnp.asarray(alibi, F32)
    slopes_log2 = jnp.asarray(alibi * LOG2E, F32)
    page_table_flat = page_table.reshape(-1)

    y_p, y_s = x_prompt, x_sample
    outs = {name: [] for name in ("kp", "vp", "cp", "ks", "vs", "cs")}
    for l in range(depth):
        ada = _ada(jnp.concatenate([c_prompt, c_sample], axis=0), w_ada[l], b_ada[l])
        ada_p, ada_s = _split_ada(ada[:batch]), _split_ada(ada[batch:])
        lng = [ln_g[l, i].reshape(1, D_MODEL) for i in range(3)]
        lnb = [ln_b[l, i].reshape(1, D_MODEL) for i in range(3)]
        w1 = (ffn1_wg[l].astype(BF16), ffn1_wu[l].astype(BF16), ffn1_wd[l].astype(BF16))
        w2 = (ffn2_wg[l].astype(BF16), ffn2_wu[l].astype(BF16), ffn2_wd[l].astype(BF16))
        win16, wo16 = w_in[l].astype(BF16), w_out[l].astype(BF16)
        beta_a = beta_attn[l].reshape(1, ATTN_WIDTH)
        beta_c = beta_conv[l].reshape(1, CONV_CH)

        tile_p = dict(alpha=alpha, nseq_blk=1, rows_blk=512)
        tile_s = dict(alpha=alpha, nseq_blk=dec_batch, rows_blk=dec_seq)
        pool_kt = jnp.swapaxes(cache_k[l], -1, -2)
        pool_vt = jnp.swapaxes(cache_v[l], -1, -2)

        x1_s, q, k, v, glu_s = _stage_a(y_s, ada_s, lng[0], lnb[0], *w1, win16, paged=False, **tile_s)
        q4, k4, v4 = (_heads(t2, dec_batch, dec_seq) for t2 in (q, k, v))

        x1, qt, k16, kmean, ktp, vtp, vt16, glu, kmean_t = _stage_a(
            y_p, ada_p, lng[0], lnb[0], *w1, win16, paged=True, page_table_flat=page_table_flat,
            pool_kt=pool_kt, cache_seqs=dec_batch, **tile_p)
        idx = _topk_blocks(q.reshape(dec_batch, dec_seq, ATTN_WIDTH), kmean_t)[..., :MOBA_TOPK]
        attn = _moba_prompt(qt, k16.reshape(batch, seq, ATTN_WIDTH), vt16, kmean, slopes_log2)
        glu3 = glu.reshape(batch, seq, CONV_CH)
        conv = _conv_branch(jnp.zeros((batch, HIST, CONV_CH), F32), glu3, w_dw[l], b_dw[l],
                            gn_g[l], gn_b[l], nb=1, ch=256)
        to_t = lambda a4: a4.transpose(0, 2, 3, 1)
        sample = dict(qt=to_t(q4), knt=to_t(k4), vnt=to_t(v4), pool_kt=pool_kt, pool_vt=pool_vt,
                      idx_flat=idx.reshape(-1), page_table_flat=page_table_flat, slopes=slopes,
                      past_len=past_len, n_pages=n_pages)
        y_p, attn_t = _stage_c(x1, attn.reshape(batch * seq, ATTN_WIDTH),
                               conv.reshape(batch * seq, CONV_CH), ada_p, lng[1], lnb[1], lng[2], lnb[2],
                               beta_a, beta_c, wo16, *w2, sample=sample, **tile_p)
        outs["kp"].append(jnp.swapaxes(ktp, -1, -2))
        outs["vp"].append(jnp.swapaxes(vtp, -1, -2))
        outs["cp"].append(glu3[:, seq - HIST:, :])

        x1, glu = x1_s, glu_s
        attn = attn_t.transpose(0, 3, 1, 2).reshape(dec_batch * dec_seq, ATTN_WIDTH)
        glu3 = glu.reshape(dec_batch, dec_seq, CONV_CH)
        hist = state_conv[l]
        conv = _conv_branch(hist, glu3, w_dw[l], b_dw[l], gn_g[l], gn_b[l], nb=dec_batch, ch=dec_seq)
        y_s = _stage_c(x1, attn, conv.reshape(dec_batch * dec_seq, CONV_CH),
                       ada_s, lng[1], lnb[1], lng[2], lnb[2], beta_a, beta_c, wo16, *w2, **tile_s)
        outs["ks"].append(k4.transpose(0, 2, 1, 3))
        outs["vs"].append(v4.transpose(0, 2, 1, 3))
        outs["cs"].append(jnp.concatenate([hist, glu3], axis=1)[:, dec_seq:, :])

    stack = lambda name: jnp.stack(outs[name], 0)
    return (y_p, y_s, stack("kp"), stack("vp"), stack("cp"), stack("ks"), stack("vs"), stack("cs"))
```

```python
import functools

import numpy as np
import jax
import jax.numpy as jnp
from jax import lax
from jax.experimental import pallas as pl
from jax.experimental.pallas import tpu as pltpu

F32 = jnp.float32
BF16 = jnp.bfloat16
HIGHEST = lax.Precision.HIGHEST

D_MODEL = 1024
D_FF = 2816
N_HEADS = 8
HEAD_DIM = 64
ATTN_WIDTH = N_HEADS * HEAD_DIM
CONV_CH = 512
W_IN_COLS = 3 * ATTN_WIDTH + 2 * CONV_CH
CONV_K = 31
HIST = CONV_K - 1
GN_GROUPS = 8
MOBA_BLOCK = 256
MOBA_TOPK = 3
PAGE_SIZE = 128
PAGES_PER_BLOCK = MOBA_BLOCK // PAGE_SIZE
LN_EPS = 1e-5
NEG_INF = -1e30
SCALE = HEAD_DIM ** -0.5

LANES = 128
SUBLANES = 8
HEADS_PER_GROUP = LANES // HEAD_DIM
CONV_PAD_ROWS = 32
LOG2E = 1.4426950408889634
VMEM_LIMIT = 60 * 1024 * 1024

FF_CHUNKS = ((0, 1024), (1024, 2048), (2048, D_FF))


def _silu(x):
    return x / (1.0 + jnp.exp(-x))


def _sigmoid(x):
    return 1.0 / (1.0 + jnp.exp(-x))


def _layernorm(t, g, b):
    mu = jnp.mean(t, axis=-1, keepdims=True)
    d = t - mu
    var = jnp.mean(d * d, axis=-1, keepdims=True)
    return d * lax.rsqrt(var + LN_EPS) * g + b


def _ffn(h, wg_ref, wu_ref, wd_ref, chunks=FF_CHUNKS, before_up=None, before_down=None):
    acc = None
    for ci, (lo, hi) in enumerate(chunks):
        if before_up is not None:
            before_up(ci)
        g = jnp.dot(h, wg_ref[:, lo:hi], preferred_element_type=F32)
        u = jnp.dot(h, wu_ref[:, lo:hi], preferred_element_type=F32)
        a = (_silu(g) * u).astype(BF16)
        if before_down is not None:
            before_down(ci)
        y = jnp.dot(a, wd_ref[lo:hi, :], preferred_element_type=F32)
        acc = y if acc is None else acc + y
    return acc


def _split_bf16(t):
    hi = t.astype(BF16)
    return hi, (t - hi.astype(F32)).astype(BF16)


def _ada_kernel(c_ref, w_ref, b_ref, o_ref):
    a_hi, a_lo = _split_bf16(_silu(c_ref[...]))
    w_hi, w_lo = _split_bf16(w_ref[...])
    o_ref[...] = (jnp.dot(a_hi, w_hi, preferred_element_type=F32)
                  + jnp.dot(a_lo, w_hi, preferred_element_type=F32)
                  + jnp.dot(a_hi, w_lo, preferred_element_type=F32)) + b_ref[...]


def _ada(c_all, w_ada, b_ada):
    n = c_all.shape[0]
    tn = 1024
    return pl.pallas_call(
        _ada_kernel,
        grid=(w_ada.shape[1] // tn,),
        in_specs=[
            pl.BlockSpec((n, D_MODEL), lambda j: (0, 0)),
            pl.BlockSpec((D_MODEL, tn), lambda j: (0, j)),
            pl.BlockSpec((1, tn), lambda j: (0, j)),
        ],
        out_specs=pl.BlockSpec((n, tn), lambda j: (0, j)),
        out_shape=jax.ShapeDtypeStruct((n, w_ada.shape[1]), F32),
        name="ada",
    )(c_all, w_ada, b_ada.reshape(1, -1))


STREAM_CHUNK_PAGES = 8
STREAM_SLOTS = 4
STREAM_FF_CHUNKS = ((0, 512), (512, 1024), (1024, 1536), (1536, 2048), (2048, 2560), (2560, D_FF))
W_IN_PARTS = ((0, ATTN_WIDTH), (ATTN_WIDTH, 2 * ATTN_WIDTH), (2 * ATTN_WIDTH, 3 * ATTN_WIDTH),
              (3 * ATTN_WIDTH, W_IN_COLS))
STREAM_CHUNKS_PER_STEP = 2 * len(STREAM_FF_CHUNKS) + len(W_IN_PARTS)


def _block_mean_stream(step, n_steps, pt_ref, pool_ref, kmt_ref, ring_ref, sem_ref):
    cps = STREAM_CHUNKS_PER_STEP
    ahead = STREAM_SLOTS - 1
    blocks_per_chunk = STREAM_CHUNK_PAGES // PAGES_PER_BLOCK
    n_cols = kmt_ref.shape[3]
    assert cps % STREAM_SLOTS == 0 and cps * blocks_per_chunk == n_cols and ahead < cps

    def page_copy(page, slot, i):
        return pltpu.make_async_copy(pool_ref.at[page], ring_ref.at[slot, i], sem_ref.at[slot])

    def start_chunk(g, slot):
        for i in range(STREAM_CHUNK_PAGES):
            page_copy(pt_ref[g * STREAM_CHUNK_PAGES + i], slot, i).start()

    def ring_step(c):
        g = step * cps + c
        slot = c % STREAM_SLOTS
        if c == 0:
            @pl.when(step == 0)
            def _():
                for a in range(ahead):
                    start_chunk(a, a)
        if c + ahead < cps:
            start_chunk(g + ahead, (c + ahead) % STREAM_SLOTS)
        else:
            @pl.when(step + 1 < n_steps)
            def _():
                start_chunk(g + ahead, (c + ahead) % STREAM_SLOTS)
        for i in range(STREAM_CHUNK_PAGES):
            page_copy(0, slot, i).wait()
        col_id = lax.broadcasted_iota(jnp.int32, (HEAD_DIM, n_cols), 1)
        cols = slice(c * blocks_per_chunk, (c + 1) * blocks_per_chunk)
        for h in range(N_HEADS):
            acc = jnp.zeros((HEAD_DIM, n_cols), F32)
            for jb in range(blocks_per_chunk):
                tot = ring_ref[slot, jb * PAGES_PER_BLOCK, h]
                for i in range(1, PAGES_PER_BLOCK):
                    tot = tot + ring_ref[slot, jb * PAGES_PER_BLOCK + i, h]
                mean = jnp.sum(tot, axis=1, keepdims=True) * (1.0 / MOBA_BLOCK)
                acc = jnp.where(col_id == c * blocks_per_chunk + jb, mean, acc)
            kmt_ref[0, h, :, cols] = acc[:, cols]

    return ring_step


def _stage_a_kernel(alpha, paged, *refs):
    (x_ref, sh0_ref, sc0_ref, g0_ref, sh1_ref, sc1_ref, lng_ref, lnb_ref,
     wg_ref, wu_ref, wd_ref, win_ref) = refs[:12]
    if paged:
        pt_ref, pool_ref, x1_ref = refs[12:15]
        out_refs = refs[15:22]
        kmt_ref, ring_ref, sem_ref = refs[22:25]
        step = pl.program_id(0) * pl.num_programs(1) + pl.program_id(1)
        ring_step = _block_mean_stream(step, pl.num_programs(0) * pl.num_programs(1),
                                       pt_ref, pool_ref, kmt_ref, ring_ref, sem_ref)
        ff_chunks = STREAM_FF_CHUNKS
    else:
        x1_ref = refs[12]
        out_refs = refs[13:]
        ring_step = None
        ff_chunks = FF_CHUNKS
    nseq, rows, d = x_ref.shape
    m = nseq * rows
    x = x_ref[...]
    h0 = (x * (1.0 + sc0_ref[...]) + sh0_ref[...]).reshape(m, d).astype(BF16)
    if paged:
        y = _ffn(h0, wg_ref, wu_ref, wd_ref, ff_chunks,
                 lambda ci: ring_step(2 * ci), lambda ci: ring_step(2 * ci + 1))
    else:
        y = _ffn(h0, wg_ref, wu_ref, wd_ref, ff_chunks)
    y = y.reshape(nseq, rows, d)
    x1 = _layernorm(alpha * x + g0_ref[...] * (0.5 * y), lng_ref[...], lnb_ref[...])
    x1_ref[...] = x1
    h1 = (x1 * (1.0 + sc1_ref[...]) + sh1_ref[...]).reshape(m, d).astype(BF16)
    parts = []
    for pi, (lo, hi) in enumerate(W_IN_PARTS):
        if paged:
            ring_step(2 * len(ff_chunks) + pi)
        parts.append(jnp.dot(h1, win_ref[:, lo:hi], preferred_element_type=F32))
    q, k, v, ug = parts
    ga = ug[:, 0:CONV_CH]
    gb = ug[:, CONV_CH:]
    glu = ga * _sigmoid(gb)
    if paged:
        qt_ref, k16_ref, kmean_ref, ktp_ref, vtp_ref, vt16_ref, glu_ref = out_refs
        qt_ref[0] = q.T
        k16_ref[...] = k.astype(BF16)
        blocks_per_tile = m // MOBA_BLOCK
        means = [jnp.sum(k[n * MOBA_BLOCK:(n + 1) * MOBA_BLOCK, :], axis=0, keepdims=True)
                 * (1.0 / MOBA_BLOCK) for n in range(blocks_per_tile)]
        tile = pl.program_id(1)
        for jj in range(kmean_ref.shape[1] // blocks_per_tile):
            @pl.when(tile == jj)
            def _():
                for n in range(blocks_per_tile):
                    kmean_ref[0, jj * blocks_per_tile + n:jj * blocks_per_tile + n + 1, :] = means[n]
        kt = k.T
        vt = v.T
        for p in range(m // PAGE_SIZE):
            for h in range(N_HEADS):
                rs = slice(h * HEAD_DIM, (h + 1) * HEAD_DIM)
                cs = slice(p * PAGE_SIZE, (p + 1) * PAGE_SIZE)
                ktp_ref[0, p, h] = kt[rs, cs]
                vtp_ref[0, p, h] = vt[rs, cs]
                vt16_ref[0, p, h] = vt[rs, cs].astype(BF16)
    else:
        q_ref, k_ref, v_ref, glu_ref = out_refs
        q_ref[...] = q
        k_ref[...] = k
        v_ref[...] = v
    glu_ref[...] = glu


def _const_spec(shape):
    return pl.BlockSpec(shape, lambda i, j: (0,) * len(shape), pipeline_mode=pl.Buffered(1))


def _stage_a(x, ada, ln_g, ln_b, wg, wu, wd, win, *, alpha, nseq_blk, rows_blk, paged,
             page_table_flat=None, pool_kt=None, cache_seqs=None):
    n_seq, s, _ = x.shape
    tiles_per_seq = s // rows_blk
    assert nseq_blk == 1 or (nseq_blk == n_seq and tiles_per_seq == 1)
    m_blk = nseq_blk * rows_blk
    n_tok = n_seq * s
    x_spec = pl.BlockSpec((nseq_blk, rows_blk, D_MODEL), lambda i, j: (i, j, 0))
    a_spec = pl.BlockSpec((nseq_blk, 1, D_MODEL), lambda i, j: (i, 0, 0))
    flat_spec = pl.BlockSpec((m_blk, ATTN_WIDTH), lambda i, j: (i * tiles_per_seq + j, 0))
    flat_shape = jax.ShapeDtypeStruct((n_tok, ATTN_WIDTH), F32)
    if paged:
        assert nseq_blk == 1 and rows_blk % MOBA_BLOCK == 0 and MOBA_BLOCK % PAGE_SIZE == 0
        ppt = rows_blk // PAGE_SIZE
        page_dims = (n_seq, s // PAGE_SIZE, N_HEADS, HEAD_DIM, PAGE_SIZE)
        page_spec = pl.BlockSpec((1, ppt, N_HEADS, HEAD_DIM, PAGE_SIZE), lambda i, j: (i, j, 0, 0, 0))
        page_shape = jax.ShapeDtypeStruct(page_dims, F32)
        out_specs = [x_spec, pl.BlockSpec((1, ATTN_WIDTH, rows_blk), lambda i, j: (i, 0, j)),
                     flat_spec,
                     pl.BlockSpec((1, s // MOBA_BLOCK, ATTN_WIDTH), lambda i, j: (i, 0, 0)),
                     page_spec, page_spec, page_spec, flat_spec]
        out_shape = [jax.ShapeDtypeStruct((n_seq, s, D_MODEL), F32),
                     jax.ShapeDtypeStruct((n_seq, ATTN_WIDTH, s), F32),
                     jax.ShapeDtypeStruct((n_tok, ATTN_WIDTH), BF16),
                     jax.ShapeDtypeStruct((n_seq, s // MOBA_BLOCK, ATTN_WIDTH), F32),
                     page_shape, page_shape, jax.ShapeDtypeStruct(page_dims, BF16), flat_shape]
        n_steps = n_seq * tiles_per_seq
        pages_per_seq = page_table_flat.shape[0] // cache_seqs
        assert cache_seqs == n_steps and pages_per_seq == STREAM_CHUNKS_PER_STEP * STREAM_CHUNK_PAGES
        blocks_per_seq = pages_per_seq // PAGES_PER_BLOCK
        extra_in = [page_table_flat, pool_kt]
        extra_in_specs = [pl.BlockSpec(memory_space=pltpu.SMEM), pl.BlockSpec(memory_space=pl.ANY)]
        out_specs.append(pl.BlockSpec((1, N_HEADS, HEAD_DIM, blocks_per_seq),
                                      lambda i, j: (i * tiles_per_seq + j, 0, 0, 0)))
        out_shape.append(jax.ShapeDtypeStruct((cache_seqs, N_HEADS, HEAD_DIM, blocks_per_seq), F32))
        scratch = [pltpu.VMEM((STREAM_SLOTS, STREAM_CHUNK_PAGES, N_HEADS, HEAD_DIM, PAGE_SIZE), F32),
                   pltpu.SemaphoreType.DMA((STREAM_SLOTS,))]
    else:
        out_specs = [x_spec, flat_spec, flat_spec, flat_spec, flat_spec]
        out_shape = [jax.ShapeDtypeStruct((n_seq, s, D_MODEL), F32)] + [flat_shape] * 4
        extra_in, extra_in_specs, scratch = [], [], []
    return pl.pallas_call(
        functools.partial(_stage_a_kernel, alpha, paged),
        grid=(n_seq // nseq_blk, tiles_per_seq),
        in_specs=[x_spec, a_spec, a_spec, a_spec, a_spec, a_spec,
                  _const_spec((1, D_MODEL)), _const_spec((1, D_MODEL)),
                  _const_spec((D_MODEL, D_FF)), _const_spec((D_MODEL, D_FF)),
                  _const_spec((D_FF, D_MODEL)), _const_spec((D_MODEL, W_IN_COLS))] + extra_in_specs,
        out_specs=out_specs,
        out_shape=out_shape,
        scratch_shapes=scratch,
        compiler_params=pltpu.CompilerParams(
            dimension_semantics=("arbitrary", "arbitrary"), vmem_limit_bytes=VMEM_LIMIT),
        name="stage_a",
    )(x, ada["sh0"], ada["sc0"], ada["g0"], ada["sh1"], ada["sc1"], ln_g, ln_b, wg, wu, wd, win, *extra_in)


SAMPLE_GATHER_SLOTS = 4


def _stage_c_kernel(alpha, sample_cfg, *refs):
    (x1_ref, attn_ref, conv_ref, g1_ref, sh2_ref, sc2_ref, g2_ref, lng1_ref, lnb1_ref, lng2_ref, lnb2_ref,
     ba_ref, bc_ref, wo_ref, wg_ref, wu_ref, wd_ref) = refs[:17]
    if sample_cfg is not None:
        past_len, n_pages = sample_cfg
        y_ref, so_ref = refs[25:27]
        step = pl.program_id(0) * pl.num_programs(1) + pl.program_id(1)
        ring_step = _sample_attention_stream(step, pl.num_programs(0) * pl.num_programs(1), past_len,
                                             n_pages, *refs[17:25], so_ref, *refs[27:30])
        ff_chunks = STREAM_FF_CHUNKS
        assert len(ff_chunks) + 2 == N_HEADS
        before_chunk = lambda ci: ring_step(ci + 1)
        ring_step(0)
    else:
        y_ref = refs[17]
        ring_step, before_chunk, ff_chunks = None, None, FF_CHUNKS
    nseq, rows, d = x1_ref.shape
    m = nseq * rows
    a = (attn_ref[...] * ba_ref[...]).astype(BF16)
    c = (conv_ref[...] * bc_ref[...]).astype(BF16)
    mix = (jnp.dot(a, wo_ref[0:ATTN_WIDTH, :], preferred_element_type=F32)
           + jnp.dot(c, wo_ref[ATTN_WIDTH:, :], preferred_element_type=F32)).reshape(nseq, rows, d)
    x2 = _layernorm(alpha * x1_ref[...] + g1_ref[...] * mix, lng1_ref[...], lnb1_ref[...])
    h2 = (x2 * (1.0 + sc2_ref[...]) + sh2_ref[...]).reshape(m, d).astype(BF16)
    y = _ffn(h2, wg_ref, wu_ref, wd_ref, ff_chunks, before_chunk).reshape(nseq, rows, d)
    if ring_step is not None:
        ring_step(N_HEADS - 1)
    y_ref[...] = _layernorm(alpha * x2 + g2_ref[...] * (0.5 * y), lng2_ref[...], lnb2_ref[...])


def _stage_c(x1, attn, conv, ada, ln_g1, ln_b1, ln_g2, ln_b2, beta_a, beta_c, wo, wg, wu, wd,
             *, alpha, nseq_blk, rows_blk, sample=None):
    n_seq, s, _ = x1.shape
    tiles_per_seq = s // rows_blk
    assert nseq_blk == 1 or (nseq_blk == n_seq and tiles_per_seq == 1)
    m_blk = nseq_blk * rows_blk
    x_spec = pl.BlockSpec((nseq_blk, rows_blk, D_MODEL), lambda i, j: (i, j, 0))
    a_spec = pl.BlockSpec((nseq_blk, 1, D_MODEL), lambda i, j: (i, 0, 0))
    flat = pl.BlockSpec((m_blk, ATTN_WIDTH), lambda i, j: (i * tiles_per_seq + j, 0))
    out_specs = [x_spec]
    out_shape = [jax.ShapeDtypeStruct((n_seq, s, D_MODEL), F32)]
    extra_in, extra_in_specs, scratch, cfg = [], [], [], None
    if sample is not None:
        seqs, _, _, t = sample["qt"].shape
        assert seqs == n_seq * tiles_per_seq
        n_k = t * MOBA_TOPK * PAGES_PER_BLOCK
        cfg = (sample["past_len"], sample["n_pages"])
        smem = pl.BlockSpec(memory_space=pltpu.SMEM)
        hbm = pl.BlockSpec(memory_space=pl.ANY)
        seq_spec = pl.BlockSpec((1, N_HEADS, HEAD_DIM, t), lambda i, j: (i * tiles_per_seq + j, 0, 0, 0))
        extra_in = [sample["idx_flat"], sample["page_table_flat"], sample["slopes"],
                    sample["qt"], sample["knt"], sample["vnt"], sample["pool_kt"], sample["pool_vt"]]
        extra_in_specs = [smem, smem, smem, seq_spec, seq_spec, seq_spec, hbm, hbm]
        out_specs.append(seq_spec)
        out_shape.append(jax.ShapeDtypeStruct((seqs, N_HEADS, HEAD_DIM, t), F32))
        scratch = [pltpu.VMEM((SAMPLE_GATHER_SLOTS, n_k, HEAD_DIM, PAGE_SIZE), F32),
                   pltpu.VMEM((SAMPLE_GATHER_SLOTS, n_k, HEAD_DIM, PAGE_SIZE), F32),
                   pltpu.SemaphoreType.DMA((SAMPLE_GATHER_SLOTS,))]
    res = pl.pallas_call(
        functools.partial(_stage_c_kernel, alpha, cfg),
        grid=(n_seq // nseq_blk, tiles_per_seq),
        in_specs=[x_spec, flat, flat, a_spec, a_spec, a_spec, a_spec,
                  _const_spec((1, D_MODEL)), _const_spec((1, D_MODEL)),
                  _const_spec((1, D_MODEL)), _const_spec((1, D_MODEL)),
                  _const_spec((1, ATTN_WIDTH)), _const_spec((1, CONV_CH)),
                  _const_spec((D_MODEL, D_MODEL)),
                  _const_spec((D_MODEL, D_FF)), _const_spec((D_MODEL, D_FF)),
                  _const_spec((D_FF, D_MODEL))] + extra_in_specs,
        out_specs=out_specs,
        out_shape=out_shape,
        scratch_shapes=scratch,
        compiler_params=pltpu.CompilerParams(
            dimension_semantics=("arbitrary", "arbitrary"), vmem_limit_bytes=VMEM_LIMIT),
        name="stage_c",
    )(x1, attn, conv, ada["g1"], ada["sh2"], ada["sc2"], ada["g2"],
      ln_g1, ln_b1, ln_g2, ln_b2, beta_a, beta_c, wo, wg, wu, wd, *extra_in)
    return res if sample is not None else res[0]


def _conv_kernel(n_chunks, hist_ref, prev_ref, cur_ref, w_ref, bdw_ref, gng_ref, gnb_ref, gavg_ref,
                 o_ref, buf_ref):
    nb, ch, c = cur_ref.shape
    if n_chunks == 1:
        halo = hist_ref[...]
    else:
        halo = jnp.where(pl.program_id(1) == 0, hist_ref[...], prev_ref[:, ch - HIST:, :])
    pad = CONV_PAD_ROWS
    buf_ref[:, 0:pad - HIST, :] = jnp.zeros((nb, pad - HIST, c), F32)
    buf_ref[:, pad - HIST:pad, :] = halo
    buf_ref[:, pad:pad + ch, :] = cur_ref[...]
    buf_ref[:, pad + ch:, :] = jnp.zeros((nb, SUBLANES, c), F32)
    acc = None
    for r in range(SUBLANES):
        part = None
        for a in range((pad + SUBLANES) // SUBLANES):
            j = SUBLANES * a + r - (pad - HIST)
            if 0 <= j < CONV_K:
                term = buf_ref[:, SUBLANES * a:SUBLANES * a + ch + SUBLANES, :] * w_ref[j:j + 1, :]
                part = term if part is None else part + term
        shifted = part[:, r:r + ch, :]
        acc = shifted if acc is None else acc + shifted
    y = (acc + bdw_ref[...]).reshape(nb * ch, c)
    gavg = gavg_ref[...]

    def group_mean(t):
        hi, lo = _split_bf16(t)
        return (jnp.dot(hi, gavg, preferred_element_type=F32)
                + jnp.dot(lo, gavg, preferred_element_type=F32))

    mu = group_mean(y)
    dlt = y - mu
    var = group_mean(dlt * dlt)
    z = dlt * lax.rsqrt(var + LN_EPS) * gng_ref[...] + gnb_ref[...]
    o_ref[...] = _silu(z).reshape(nb, ch, c)


def _conv_branch(hist, glu3, w_dw, b_dw, gn_g, gn_b, *, nb, ch):
    n_seq, s, c = glu3.shape
    n_chunks = s // ch
    assert ch >= HIST or n_chunks == 1
    grp = np.arange(c) // (c // GN_GROUPS)
    gavg = jnp.asarray((grp[:, None] == grp[None, :]).astype(np.float32) / (c // GN_GROUPS), BF16)
    cur_spec = pl.BlockSpec((nb, ch, c), lambda i, j: (i, j, 0))
    prev_spec = pl.BlockSpec((nb, ch, c), lambda i, j: (i, jnp.maximum(j - 1, 0), 0))
    return pl.pallas_call(
        functools.partial(_conv_kernel, n_chunks),
        grid=(n_seq // nb, n_chunks),
        in_specs=[pl.BlockSpec((nb, HIST, c), lambda i, j: (i, 0, 0)), prev_spec, cur_spec,
                  pl.BlockSpec((CONV_K, c), lambda i, j: (0, 0)),
                  pl.BlockSpec((1, c), lambda i, j: (0, 0)),
                  pl.BlockSpec((1, c), lambda i, j: (0, 0)),
                  pl.BlockSpec((1, c), lambda i, j: (0, 0)),
                  pl.BlockSpec((c, c), lambda i, j: (0, 0))],
        out_specs=cur_spec,
        out_shape=jax.ShapeDtypeStruct((n_seq, s, c), F32),
        scratch_shapes=[pltpu.VMEM((nb, CONV_PAD_ROWS + ch + SUBLANES, c), F32)],
        compiler_params=pltpu.CompilerParams(dimension_semantics=("arbitrary", "arbitrary")),
        name="conv_branch",
    )(hist, glu3, glu3, w_dw, b_dw.reshape(1, c), gn_g.reshape(1, c), gn_b.reshape(1, c), gavg)


MOBA_HEADS_PER_STEP = 8


def _moba_prompt_kernel(slopes_ref, qt_ref, k16_ref, vt16_ref, kmean_ref, o_ref,
                        q16_ref, bias_ref, krs_ref, s_ref, m_ref, l_ref, acc_ref):
    grp0 = pl.program_id(1)
    qb = pl.program_id(2)
    blk = MOBA_BLOCK
    nb = kmean_ref.shape[1]
    hps = MOBA_HEADS_PER_STEP
    row_head = lax.broadcasted_iota(jnp.int32, (LANES, blk), 0) // HEAD_DIM
    key_i = lax.broadcasted_iota(jnp.int32, (blk, blk), 0)
    qry_i = lax.broadcasted_iota(jnp.int32, (blk, blk), 1)
    key_f = key_i.astype(F32)
    blk_id = lax.broadcasted_iota(jnp.int32, (nb, blk), 0)
    past = blk_id < qb
    zero_row = jnp.zeros((1, blk), jnp.int32)

    def lanes_of(hh):
        g = hh // HEADS_PER_GROUP
        return slice(g * LANES, (g + 1) * LANES)

    def scores(hh, n):
        kb = k16_ref[0, pl.ds(pl.multiple_of(n * blk, blk), blk), lanes_of(hh)]
        return jnp.dot(kb, q16_ref[hh], preferred_element_type=F32) + krs_ref[hh]

    def pv(hh, n, p):
        p16 = p.astype(BF16)
        acc = None
        for i in range(PAGES_PER_BLOCK):
            vt = vt16_ref[0, n * PAGES_PER_BLOCK + i, hh]
            part = jnp.dot(vt, p16[i * PAGE_SIZE:(i + 1) * PAGE_SIZE, :], preferred_element_type=F32)
            acc = part if acc is None else acc + part
        return acc

    @pl.when(qb == 0)
    def _():
        for hh in range(hps):
            krs_ref[hh] = key_f * slopes_ref[grp0 * hps + hh]

    for hh in range(hps):
        sub = hh % HEADS_PER_GROUP
        qh = jnp.where(row_head == sub, qt_ref[0, lanes_of(hh), :], 0.0)
        q16_ref[hh] = (qh * (SCALE * LOG2E)).astype(BF16)

        gate = jnp.dot(kmean_ref[0, :, lanes_of(hh)], qh, preferred_element_type=F32, precision=HIGHEST)
        cnt = jnp.zeros((nb, blk), F32)
        for mth in range(nb):
            other = gate[mth:mth + 1, :]
            ahead = (other > gate) | ((other == gate) & (blk_id > mth))
            cnt = cnt + jnp.where(ahead & (mth < qb), 1.0, 0.0)
        bias_ref[hh] = jnp.where((cnt < MOBA_TOPK) & past, 0.0, NEG_INF)

        s = jnp.where(key_i <= qry_i, scores(hh, qb), NEG_INF)
        s_ref[hh, qb] = s
        m_ref[hh] = jnp.max(s, axis=0, keepdims=True)

    def block_const(hh, n):
        rel = ((n - qb) * blk + zero_row).astype(F32)
        return slopes_ref[grp0 * hps + hh] * rel + bias_ref[hh, pl.ds(n, 1), :]

    def pass1(n, carry):
        for hh in range(hps):
            s = scores(hh, n)
            s_ref[hh, n] = s
            m_ref[hh] = jnp.maximum(m_ref[hh], jnp.max(s, axis=0, keepdims=True) + block_const(hh, n))
        return carry

    lax.fori_loop(0, qb, pass1, 0)

    def block_out(hh, n, c):
        p = jnp.exp2(s_ref[hh, n] - (m_ref[hh] - c))
        return jnp.sum(p, axis=0, keepdims=True), pv(hh, n, p)

    for hh in range(hps):
        l_ref[hh], acc_ref[hh] = block_out(hh, qb, 0.0)

    def pass2(n, carry):
        for hh in range(hps):
            l_part, acc_part = block_out(hh, n, block_const(hh, n))
            l_ref[hh] = l_ref[hh] + l_part
            acc_ref[hh] = acc_ref[hh] + acc_part
        return carry

    lax.fori_loop(0, qb, pass2, 0)
    out_t = jnp.concatenate([acc_ref[hh] / l_ref[hh] for hh in range(hps)], axis=0)
    o_ref[0] = out_t.T


def _moba_prompt(qt3, k16, vt16, kmean, slopes):
    b, s, _ = k16.shape
    hps = MOBA_HEADS_PER_STEP
    width = hps * HEAD_DIM
    n_blocks = s // MOBA_BLOCK
    return pl.pallas_call(
        _moba_prompt_kernel,
        grid=(b, N_HEADS // hps, n_blocks),
        in_specs=[pl.BlockSpec(memory_space=pltpu.SMEM),
                  pl.BlockSpec((1, width, MOBA_BLOCK), lambda i, g, j: (i, g, j)),
                  pl.BlockSpec((1, s, width), lambda i, g, j: (i, 0, g)),
                  pl.BlockSpec((1, s // PAGE_SIZE, hps, HEAD_DIM, PAGE_SIZE),
                               lambda i, g, j: (i, 0, g, 0, 0)),
                  pl.BlockSpec((1, n_blocks, width), lambda i, g, j: (i, 0, g))],
        out_specs=pl.BlockSpec((1, MOBA_BLOCK, width), lambda i, g, j: (i, j, g)),
        out_shape=jax.ShapeDtypeStruct((b, s, ATTN_WIDTH), F32),
        scratch_shapes=[pltpu.VMEM((hps, LANES, MOBA_BLOCK), BF16),
                        pltpu.VMEM((hps, n_blocks, MOBA_BLOCK), F32),
                        pltpu.VMEM((hps, MOBA_BLOCK, MOBA_BLOCK), F32),
                        pltpu.VMEM((hps, n_blocks, MOBA_BLOCK, MOBA_BLOCK), F32),
                        pltpu.VMEM((hps, 1, MOBA_BLOCK), F32),
                        pltpu.VMEM((hps, 1, MOBA_BLOCK), F32),
                        pltpu.VMEM((hps, HEAD_DIM, MOBA_BLOCK), F32)],
        compiler_params=pltpu.CompilerParams(
            dimension_semantics=("arbitrary", "arbitrary", "arbitrary"), vmem_limit_bytes=VMEM_LIMIT),
        name="moba_prompt",
    )(slopes, qt3, k16, vt16, kmean)


TOPK_SEQS_PER_STEP = 4


def _topk_kernel(q_ref, kmt_ref, o_ref):
    ns, t, nbp = q_ref.shape[0], q_ref.shape[1], kmt_ref.shape[3]
    rows = ns * N_HEADS * t
    gates = []
    for s in range(ns):
        q = q_ref[s]
        for h in range(N_HEADS):
            gates.append(jnp.dot(q[:, h * HEAD_DIM:(h + 1) * HEAD_DIM], kmt_ref[s, h],
                                 preferred_element_type=F32, precision=HIGHEST))
    g = jnp.concatenate(gates, axis=0)
    lane = lax.broadcasted_iota(jnp.int32, (rows, nbp), 1).astype(F32)
    out_lane = lax.broadcasted_iota(jnp.int32, (rows, LANES), 1)
    res = jnp.zeros((rows, LANES), F32)
    for r in range(MOBA_TOPK):
        mx = jnp.max(g, axis=1, keepdims=True)
        idx = jnp.min(jnp.where(g == mx, lane, float(nbp)), axis=1, keepdims=True)
        res = jnp.where(out_lane == r, idx, res)
        g = jnp.where(lane == idx, -jnp.inf, g)
    o_ref[...] = res.astype(jnp.int32).reshape(ns, N_HEADS, t, LANES)


def _topk_blocks(q3, kmean_t):
    n_seq, t, _ = q3.shape
    nbp = kmean_t.shape[3]
    ns = TOPK_SEQS_PER_STEP
    return pl.pallas_call(
        _topk_kernel,
        grid=(n_seq // ns,),
        in_specs=[pl.BlockSpec((ns, t, ATTN_WIDTH), lambda b: (b, 0, 0)),
                  pl.BlockSpec((ns, N_HEADS, HEAD_DIM, nbp), lambda b: (b, 0, 0, 0))],
        out_specs=pl.BlockSpec((ns, N_HEADS, t, LANES), lambda b: (b, 0, 0, 0)),
        out_shape=jax.ShapeDtypeStruct((n_seq, N_HEADS, t, LANES), jnp.int32),
        name="topk_blocks",
    )(q3, kmean_t)


def _sample_attention_stream(step, n_steps, past_len, n_pages, idx_ref, pt_ref, slopes_ref,
                             qt_ref, knt_ref, vnt_ref, pool_k_ref, pool_v_ref, o_ref,
                             kbuf_ref, vbuf_ref, sem_ref):
    t = qt_ref.shape[3]
    slabs = MOBA_TOPK * PAGES_PER_BLOCK
    n_k = t * slabs
    assert N_HEADS % 2 == 0

    def slab_copies(page, head, slot, n):
        return (pltpu.make_async_copy(pool_k_ref.at[page, head], kbuf_ref.at[slot, n], sem_ref.at[slot]),
                pltpu.make_async_copy(pool_v_ref.at[page, head], vbuf_ref.at[slot, n], sem_ref.at[slot]))

    def start_gather(seq, head, slot):
        for n in range(n_k):
            tq, rem = divmod(n, slabs)
            j, i = divmod(rem, PAGES_PER_BLOCK)
            blk_idx = idx_ref[((seq * N_HEADS + head) * t + tq) * MOBA_TOPK + j]
            page = pt_ref[seq * n_pages + blk_idx * PAGES_PER_BLOCK + i]
            for cp in slab_copies(page, head, slot, n):
                cp.start()

    n_slots = kbuf_ref.shape[0]
    ahead = n_slots - 1
    assert ahead < N_HEADS

    def ring_step(h):
        unit = step * N_HEADS + h
        slot = lax.rem(unit, n_slots)
        if h == 0:
            @pl.when(step == 0)
            def _():
                for a in range(ahead):
                    start_gather(0, a, a)
        nxt_slot = lax.rem(unit + ahead, n_slots)
        if h + ahead < N_HEADS:
            start_gather(step, h + ahead, nxt_slot)
        else:
            @pl.when(step + 1 < n_steps)
            def _():
                start_gather(step + 1, h + ahead - N_HEADS, nxt_slot)
        for n in range(n_k):
            for cp in slab_copies(0, 0, slot, n):
                cp.wait()

        slope = slopes_ref[h]
        qt = qt_ref[0, h] * SCALE
        knt = knt_ref[0, h]
        vnt = vnt_ref[0, h]
        srow = lax.broadcasted_iota(jnp.int32, (slabs, PAGE_SIZE), 0)
        lane = lax.broadcasted_iota(jnp.int32, (slabs, PAGE_SIZE), 1)
        own_pos = lax.broadcasted_iota(jnp.int32, (1, t), 1)
        out_lane = lax.broadcasted_iota(jnp.int32, (HEAD_DIM, t), 1)
        out = jnp.zeros((HEAD_DIM, t), F32)
        for tq in range(t):
            q_col = qt[:, tq:tq + 1]
            base = ((step * N_HEADS + h) * t + tq) * MOBA_TOPK
            s_rows = []
            blk_of_row = jnp.zeros((slabs, PAGE_SIZE), jnp.int32)
            for j in range(MOBA_TOPK):
                blk_of_row = jnp.where(srow // PAGES_PER_BLOCK == j, idx_ref[base + j], blk_of_row)
                for i in range(PAGES_PER_BLOCK):
                    kt = kbuf_ref[slot, (tq * MOBA_TOPK + j) * PAGES_PER_BLOCK + i]
                    s_rows.append(jnp.sum(kt * q_col, axis=0, keepdims=True))
            s_sel = jnp.concatenate(s_rows, axis=0)
            k_pos = blk_of_row * MOBA_BLOCK + (srow % PAGES_PER_BLOCK) * PAGE_SIZE + lane
            s_sel = s_sel - slope * ((past_len + tq) - k_pos).astype(F32)
            rel_own = tq - own_pos
            s_own = jnp.sum(knt * q_col, axis=0, keepdims=True)
            s_own = jnp.where(rel_own >= 0, s_own - slope * rel_own.astype(F32), NEG_INF)
            m = jnp.maximum(jnp.max(jnp.max(s_sel, axis=1, keepdims=True), axis=0, keepdims=True),
                            jnp.max(s_own, axis=1, keepdims=True))
            p_sel = jnp.exp(s_sel - m)
            p_own = jnp.exp(s_own - m)
            l = (jnp.sum(jnp.sum(p_sel, axis=1, keepdims=True), axis=0, keepdims=True)
                 + jnp.sum(p_own, axis=1, keepdims=True))
            acc = jnp.zeros((HEAD_DIM, PAGE_SIZE), F32)
            for r in range(slabs):
                acc = acc + vbuf_ref[slot, tq * slabs + r] * p_sel[r:r + 1, :]
            o_col = (jnp.sum(acc, axis=1, keepdims=True) + jnp.sum(vnt * p_own, axis=1, keepdims=True)) / l
            out = jnp.where(out_lane == tq, o_col, out)
        o_ref[0, h] = out

    return ring_step


def _split_ada(ada_rows):
    a = ada_rows.reshape(ada_rows.shape[0], 3, 3, 1, D_MODEL)
    return {"sh0": a[:, 0, 0], "sc0": a[:, 0, 1], "g0": a[:, 0, 2],
            "sh1": a[:, 1, 0], "sc1": a[:, 1, 1], "g1": a[:, 1, 2],
            "sh2": a[:, 2, 0], "sc2": a[:, 2, 1], "g2": a[:, 2, 2]}


def _heads(t2, n_seq, s):
    return t2.reshape(n_seq, s, N_HEADS, HEAD_DIM)


def kernel(x_prompt, x_sample, cache_k, cache_v, state_conv, page_table, c_prompt, c_sample,
           w_ada, b_ada, ffn1_wg, ffn1_wu, ffn1_wd, w_in, w_dw, b_dw, gn_g, gn_b,
           beta_attn, beta_conv, w_out, ffn2_wg, ffn2_wu, ffn2_wd, ln_g, ln_b):
    depth = w_ada.shape[0]
    alpha = (2.0 * depth) ** 0.25
    batch, seq, _ = x_prompt.shape
    dec_batch, dec_seq, _ = x_sample.shape
    n_pages = page_table.shape[1]
    past_len = n_pages * PAGE_SIZE
    assert past_len % MOBA_BLOCK == 0 and dec_seq <= MOBA_BLOCK
    alibi = 2.0 ** (-8.0 * np.arange(1, N_HEADS + 1) / N_HEADS)
    slopes = jnp.asarray(alibi, F32)
    slopes_log2 = jnp.asarray(alibi * LOG2E, F32)
    page_table_flat = page_table.reshape(-1)

    y_p, y_s = x_prompt, x_sample
    outs = {name: [] for name in ("kp", "vp", "cp", "ks", "vs", "cs")}
    for l in range(depth):
        ada = _ada(jnp.concatenate([c_prompt, c_sample], axis=0), w_ada[l], b_ada[l])
        ada_p, ada_s = _split_ada(ada[:batch]), _split_ada(ada[batch:])
        lng = [ln_g[l, i].reshape(1, D_MODEL) for i in range(3)]
        lnb = [ln_b[l, i].reshape(1, D_MODEL) for i in range(3)]
        w1 = (ffn1_wg[l].astype(BF16), ffn1_wu[l].astype(BF16), ffn1_wd[l].astype(BF16))
        w2 = (ffn2_wg[l].astype(BF16), ffn2_wu[l].astype(BF16), ffn2_wd[l].astype(BF16))
        win16, wo16 = w_in[l].astype(BF16), w_out[l].astype(BF16)
        beta_a = beta_attn[l].reshape(1, ATTN_WIDTH)
        beta_c = beta_conv[l].reshape(1, CONV_CH)

        tile_p = dict(alpha=alpha, nseq_blk=1, rows_blk=512)
        tile_s = dict(alpha=alpha, nseq_blk=dec_batch, rows_blk=dec_seq)
        pool_kt = jnp.swapaxes(cache_k[l], -1, -2)
        pool_vt = jnp.swapaxes(cache_v[l], -1, -2)

        x1_s, q, k, v, glu_s = _stage_a(y_s, ada_s, lng[0], lnb[0], *w1, win16, paged=False, **tile_s)
        q4, k4, v4 = (_heads(t2, dec_batch, dec_seq) for t2 in (q, k, v))

        x1, qt, k16, kmean, ktp, vtp, vt16, glu, kmean_t = _stage_a(
            y_p, ada_p, lng[0], lnb[0], *w1, win16, paged=True, page_table_flat=page_table_flat,
            pool_kt=pool_kt, cache_seqs=dec_batch, **tile_p)
        idx = _topk_blocks(q.reshape(dec_batch, dec_seq, ATTN_WIDTH), kmean_t)[..., :MOBA_TOPK]
        attn = _moba_prompt(qt, k16.reshape(batch, seq, ATTN_WIDTH), vt16, kmean, slopes_log2)
        glu3 = glu.reshape(batch, seq, CONV_CH)
        conv = _conv_branch(jnp.zeros((batch, HIST, CONV_CH), F32), glu3, w_dw[l], b_dw[l],
                            gn_g[l], gn_b[l], nb=1, ch=256)
        to_t = lambda a4: a4.transpose(0, 2, 3, 1)
        sample = dict(qt=to_t(q4), knt=to_t(k4), vnt=to_t(v4), pool_kt=pool_kt, pool_vt=pool_vt,
                      idx_flat=idx.reshape(-1), page_table_flat=page_table_flat, slopes=slopes,
                      past_len=past_len, n_pages=n_pages)
        y_p, attn_t = _stage_c(x1, attn.reshape(batch * seq, ATTN_WIDTH),
                               conv.reshape(batch * seq, CONV_CH), ada_p, lng[1], lnb[1], lng[2], lnb[2],
                               beta_a, beta_c, wo16, *w2, sample=sample, **tile_p)
        outs["kp"].append(jnp.swapaxes(ktp, -1, -2))
        outs["vp"].append(jnp.swapaxes(vtp, -1, -2))
        outs["cp"].append(glu3[:, seq - HIST:, :])

        x1, glu = x1_s, glu_s
        attn = attn_t.transpose(0, 3, 1, 2).reshape(dec_batch * dec_seq, ATTN_WIDTH)
        glu3 = glu.reshape(dec_batch, dec_seq, CONV_CH)
        hist = state_conv[l]
        conv = _conv_branch(hist, glu3, w_dw[l], b_dw[l], gn_g[l], gn_b[l], nb=dec_batch, ch=dec_seq)
        y_s = _stage_c(x1, attn, conv.reshape(dec_batch * dec_seq, CONV_CH),
                       ada_s, lng[1], lnb[1], lng[2], lnb[2], beta_a, beta_c, wo16, *w2, **tile_s)
        outs["ks"].append(k4.transpose(0, 2, 1, 3))
        outs["vs"].append(v4.transpose(0, 2, 1, 3))
        outs["cs"].append(jnp.concatenate([hist, glu3], axis=1)[:, dec_seq:, :])

    stack = lambda name: jnp.stack(outs[name], 0)
    return (y_p, y_s, stack("kp"), stack("vp"), stack("cp"), stack("ks"), stack("vs"), stack("cs"))
```

```python
import functools

import numpy as np
import jax
import jax.numpy as jnp
from jax import lax
from jax.experimental import pallas as pl
from jax.experimental.pallas import tpu as pltpu

F32 = jnp.float32
BF16 = jnp.bfloat16
HIGHEST = lax.Precision.HIGHEST

D_MODEL = 1024
D_FF = 2816
N_HEADS = 8
HEAD_DIM = 64
ATTN_WIDTH = N_HEADS * HEAD_DIM
CONV_CH = 512
W_IN_COLS = 3 * ATTN_WIDTH + 2 * CONV_CH
CONV_K = 31
HIST = CONV_K - 1
GN_GROUPS = 8
MOBA_BLOCK = 256
MOBA_TOPK = 3
PAGE_SIZE = 128
PAGES_PER_BLOCK = MOBA_BLOCK // PAGE_SIZE
LN_EPS = 1e-5
NEG_INF = -1e30
SCALE = HEAD_DIM ** -0.5

LANES = 128
SUBLANES = 8
HEADS_PER_GROUP = LANES // HEAD_DIM
CONV_PAD_ROWS = 32
LOG2E = 1.4426950408889634
VMEM_LIMIT = 60 * 1024 * 1024

FF_CHUNKS = ((0, 1024), (1024, 2048), (2048, D_FF))


def _silu(x):
    return x / (1.0 + jnp.exp(-x))


def _sigmoid(x):
    return 1.0 / (1.0 + jnp.exp(-x))


def _layernorm(t, g, b):
    mu = jnp.mean(t, axis=-1, keepdims=True)
    d = t - mu
    var = jnp.mean(d * d, axis=-1, keepdims=True)
    return d * lax.rsqrt(var + LN_EPS) * g + b


def _ffn(h, wg_ref, wu_ref, wd_ref, chunks=FF_CHUNKS, before_up=None, before_down=None):
    acc = None
    for ci, (lo, hi) in enumerate(chunks):
        if before_up is not None:
            before_up(ci)
        g = jnp.dot(h, wg_ref[:, lo:hi], preferred_element_type=F32)
        u = jnp.dot(h, wu_ref[:, lo:hi], preferred_element_type=F32)
        a = (_silu(g) * u).astype(BF16)
        if before_down is not None:
            before_down(ci)
        y = jnp.dot(a, wd_ref[lo:hi, :], preferred_element_type=F32)
        acc = y if acc is None else acc + y
    return acc


def _split_bf16(t):
    hi = t.astype(BF16)
    return hi, (t - hi.astype(F32)).astype(BF16)


def _ada_kernel(c_ref, w_ref, b_ref, o_ref):
    a_hi, a_lo = _split_bf16(_silu(c_ref[...]))
    w_hi, w_lo = _split_bf16(w_ref[...])
    o_ref[...] = (jnp.dot(a_hi, w_hi, preferred_element_type=F32)
                  + jnp.dot(a_lo, w_hi, preferred_element_type=F32)
                  + jnp.dot(a_hi, w_lo, preferred_element_type=F32)) + b_ref[...]


def _ada(c_all, w_ada, b_ada):
    n = c_all.shape[0]
    tn = 1024
    return pl.pallas_call(
        _ada_kernel,
        grid=(w_ada.shape[1] // tn,),
        in_specs=[
            pl.BlockSpec((n, D_MODEL), lambda j: (0, 0)),
            pl.BlockSpec((D_MODEL, tn), lambda j: (0, j)),
            pl.BlockSpec((1, tn), lambda j: (0, j)),
        ],
        out_specs=pl.BlockSpec((n, tn), lambda j: (0, j)),
        out_shape=jax.ShapeDtypeStruct((n, w_ada.shape[1]), F32),
        name="ada",
    )(c_all, w_ada, b_ada.reshape(1, -1))


STREAM_CHUNK_PAGES = 8
STREAM_SLOTS = 4
STREAM_FF_CHUNKS = ((0, 512), (512, 1024), (1024, 1536), (1536, 2048), (2048, 2560), (2560, D_FF))
W_IN_PARTS = ((0, ATTN_WIDTH), (ATTN_WIDTH, 2 * ATTN_WIDTH), (2 * ATTN_WIDTH, 3 * ATTN_WIDTH),
              (3 * ATTN_WIDTH, W_IN_COLS))
STREAM_CHUNKS_PER_STEP = 2 * len(STREAM_FF_CHUNKS) + len(W_IN_PARTS)


def _block_mean_stream(step, n_steps, pt_ref, pool_ref, kmt_ref, ring_ref, sem_ref):
    cps = STREAM_CHUNKS_PER_STEP
    ahead = STREAM_SLOTS - 1
    blocks_per_chunk = STREAM_CHUNK_PAGES // PAGES_PER_BLOCK
    n_cols = kmt_ref.shape[3]
    assert cps % STREAM_SLOTS == 0 and cps * blocks_per_chunk == n_cols and ahead < cps

    def page_copy(page, slot, i):
        return pltpu.make_async_copy(pool_ref.at[page], ring_ref.at[slot, i], sem_ref.at[slot])

    def start_chunk(g, slot):
        for i in range(STREAM_CHUNK_PAGES):
            page_copy(pt_ref[g * STREAM_CHUNK_PAGES + i], slot, i).start()

    def ring_step(c):
        g = step * cps + c
        slot = c % STREAM_SLOTS
        if c == 0:
            @pl.when(step == 0)
            def _():
                for a in range(ahead):
                    start_chunk(a, a)
        if c + ahead < cps:
            start_chunk(g + ahead, (c + ahead) % STREAM_SLOTS)
        else:
            @pl.when(step + 1 < n_steps)
            def _():
                start_chunk(g + ahead, (c + ahead) % STREAM_SLOTS)
        for i in range(STREAM_CHUNK_PAGES):
            page_copy(0, slot, i).wait()
        col_id = lax.broadcasted_iota(jnp.int32, (HEAD_DIM, n_cols), 1)
        cols = slice(c * blocks_per_chunk, (c + 1) * blocks_per_chunk)
        for h in range(N_HEADS):
            acc = jnp.zeros((HEAD_DIM, n_cols), F32)
            for jb in range(blocks_per_chunk):
                tot = ring_ref[slot, jb * PAGES_PER_BLOCK, h]
                for i in range(1, PAGES_PER_BLOCK):
                    tot = tot + ring_ref[slot, jb * PAGES_PER_BLOCK + i, h]
                mean = jnp.sum(tot, axis=1, keepdims=True) * (1.0 / MOBA_BLOCK)
                acc = jnp.where(col_id == c * blocks_per_chunk + jb, mean, acc)
            kmt_ref[0, h, :, cols] = acc[:, cols]

    return ring_step


def _stage_a_kernel(alpha, paged, *refs):
    (x_ref, sh0_ref, sc0_ref, g0_ref, sh1_ref, sc1_ref, lng_ref, lnb_ref,
     wg_ref, wu_ref, wd_ref, win_ref) = refs[:12]
    if paged:
        pt_ref, pool_ref, x1_ref = refs[12:15]
        out_refs = refs[15:22]
        kmt_ref, ring_ref, sem_ref = refs[22:25]
        step = pl.program_id(0) * pl.num_programs(1) + pl.program_id(1)
        ring_step = _block_mean_stream(step, pl.num_programs(0) * pl.num_programs(1),
                                       pt_ref, pool_ref, kmt_ref, ring_ref, sem_ref)
        ff_chunks = STREAM_FF_CHUNKS
    else:
        x1_ref = refs[12]
        out_refs = refs[13:]
        ring_step = None
        ff_chunks = FF_CHUNKS
    nseq, rows, d = x_ref.shape
    m = nseq * rows
    x = x_ref[...]
    h0 = (x * (1.0 + sc0_ref[...]) + sh0_ref[...]).reshape(m, d).astype(BF16)
    if paged:
        y = _ffn(h0, wg_ref, wu_ref, wd_ref, ff_chunks,
                 lambda ci: ring_step(2 * ci), lambda ci: ring_step(2 * ci + 1))
    else:
        y = _ffn(h0, wg_ref, wu_ref, wd_ref, ff_chunks)
    y = y.reshape(nseq, rows, d)
    x1 = _layernorm(alpha * x + g0_ref[...] * (0.5 * y), lng_ref[...], lnb_ref[...])
    x1_ref[...] = x1
    h1 = (x1 * (1.0 + sc1_ref[...]) + sh1_ref[...]).reshape(m, d).astype(BF16)
    parts = []
    for pi, (lo, hi) in enumerate(W_IN_PARTS):
        if paged:
            ring_step(2 * len(ff_chunks) + pi)
        parts.append(jnp.dot(h1, win_ref[:, lo:hi], preferred_element_type=F32))
    q, k, v, ug = parts
    ga = ug[:, 0:CONV_CH]
    gb = ug[:, CONV_CH:]
    glu = ga * _sigmoid(gb)
    if paged:
        qt_ref, k16_ref, kmean_ref, ktp_ref, vtp_ref, vt16_ref, glu_ref = out_refs
        qt_ref[0] = q.T
        k16_ref[...] = k.astype(BF16)
        blocks_per_tile = m // MOBA_BLOCK
        means = [jnp.sum(k[n * MOBA_BLOCK:(n + 1) * MOBA_BLOCK, :], axis=0, keepdims=True)
                 * (1.0 / MOBA_BLOCK) for n in range(blocks_per_tile)]
        tile = pl.program_id(1)
        for jj in range(kmean_ref.shape[1] // blocks_per_tile):
            @pl.when(tile == jj)
            def _():
                for n in range(blocks_per_tile):
                    kmean_ref[0, jj * blocks_per_tile + n:jj * blocks_per_tile + n + 1, :] = means[n]
        kt = k.T
        vt = v.T
        for p in range(m // PAGE_SIZE):
            for h in range(N_HEADS):
                rs = slice(h * HEAD_DIM, (h + 1) * HEAD_DIM)
                cs = slice(p * PAGE_SIZE, (p + 1) * PAGE_SIZE)
                ktp_ref[0, p, h] = kt[rs, cs]
                vtp_ref[0, p, h] = vt[rs, cs]
                vt16_ref[0, p, h] = vt[rs, cs].astype(BF16)
    else:
        q_ref, k_ref, v_ref, glu_ref = out_refs
        q_ref[...] = q
        k_ref[...] = k
        v_ref[...] = v
    glu_ref[...] = glu


def _const_spec(shape):
    return pl.BlockSpec(shape, lambda i, j: (0,) * len(shape), pipeline_mode=pl.Buffered(1))


def _stage_a(x, ada, ln_g, ln_b, wg, wu, wd, win, *, alpha, nseq_blk, rows_blk, paged,
             page_table_flat=None, pool_kt=None, cache_seqs=None):
    n_seq, s, _ = x.shape
    tiles_per_seq = s // rows_blk
    assert nseq_blk == 1 or (nseq_blk == n_seq and tiles_per_seq == 1)
    m_blk = nseq_blk * rows_blk
    n_tok = n_seq * s
    x_spec = pl.BlockSpec((nseq_blk, rows_blk, D_MODEL), lambda i, j: (i, j, 0))
    a_spec = pl.BlockSpec((nseq_blk, 1, D_MODEL), lambda i, j: (i, 0, 0))
    flat_spec = pl.BlockSpec((m_blk, ATTN_WIDTH), lambda i, j: (i * tiles_per_seq + j, 0))
    flat_shape = jax.ShapeDtypeStruct((n_tok, ATTN_WIDTH), F32)
    if paged:
        assert nseq_blk == 1 and rows_blk % MOBA_BLOCK == 0 and MOBA_BLOCK % PAGE_SIZE == 0
        ppt = rows_blk // PAGE_SIZE
        page_dims = (n_seq, s // PAGE_SIZE, N_HEADS, HEAD_DIM, PAGE_SIZE)
        page_spec = pl.BlockSpec((1, ppt, N_HEADS, HEAD_DIM, PAGE_SIZE), lambda i, j: (i, j, 0, 0, 0))
        page_shape = jax.ShapeDtypeStruct(page_dims, F32)
        out_specs = [x_spec, pl.BlockSpec((1, ATTN_WIDTH, rows_blk), lambda i, j: (i, 0, j)),
                     flat_spec,
                     pl.BlockSpec((1, s // MOBA_BLOCK, ATTN_WIDTH), lambda i, j: (i, 0, 0)),
                     page_spec, page_spec, page_spec, flat_spec]
        out_shape = [jax.ShapeDtypeStruct((n_seq, s, D_MODEL), F32),
                     jax.ShapeDtypeStruct((n_seq, ATTN_WIDTH, s), F32),
                     jax.ShapeDtypeStruct((n_tok, ATTN_WIDTH), BF16),
                     jax.ShapeDtypeStruct((n_seq, s // MOBA_BLOCK, ATTN_WIDTH), F32),
                     page_shape, page_shape, jax.ShapeDtypeStruct(page_dims, BF16), flat_shape]
        n_steps = n_seq * tiles_per_seq
        pages_per_seq = page_table_flat.shape[0] // cache_seqs
        assert cache_seqs == n_steps and pages_per_seq == STREAM_CHUNKS_PER_STEP * STREAM_CHUNK_PAGES
        blocks_per_seq = pages_per_seq // PAGES_PER_BLOCK
        extra_in = [page_table_flat, pool_kt]
        extra_in_specs = [pl.BlockSpec(memory_space=pltpu.SMEM), pl.BlockSpec(memory_space=pl.ANY)]
        out_specs.append(pl.BlockSpec((1, N_HEADS, HEAD_DIM, blocks_per_seq),
                                      lambda i, j: (i * tiles_per_seq + j, 0, 0, 0)))
        out_shape.append(jax.ShapeDtypeStruct((cache_seqs, N_HEADS, HEAD_DIM, blocks_per_seq), F32))
        scratch = [pltpu.VMEM((STREAM_SLOTS, STREAM_CHUNK_PAGES, N_HEADS, HEAD_DIM, PAGE_SIZE), F32),
                   pltpu.SemaphoreType.DMA((STREAM_SLOTS,))]
    else:
        out_specs = [x_spec, flat_spec, flat_spec, flat_spec, flat_spec]
        out_shape = [jax.ShapeDtypeStruct((n_seq, s, D_MODEL), F32)] + [flat_shape] * 4
        extra_in, extra_in_specs, scratch = [], [], []
    return pl.pallas_call(
        functools.partial(_stage_a_kernel, alpha, paged),
        grid=(n_seq // nseq_blk, tiles_per_seq),
        in_specs=[x_spec, a_spec, a_spec, a_spec, a_spec, a_spec,
                  _const_spec((1, D_MODEL)), _const_spec((1, D_MODEL)),
                  _const_spec((D_MODEL, D_FF)), _const_spec((D_MODEL, D_FF)),
                  _const_spec((D_FF, D_MODEL)), _const_spec((D_MODEL, W_IN_COLS))] + extra_in_specs,
        out_specs=out_specs,
        out_shape=out_shape,
        scratch_shapes=scratch,
        compiler_params=pltpu.CompilerParams(
            dimension_semantics=("arbitrary", "arbitrary"), vmem_limit_bytes=VMEM_LIMIT),
        name="stage_a",
    )(x, ada["sh0"], ada["sc0"], ada["g0"], ada["sh1"], ada["sc1"], ln_g, ln_b, wg, wu, wd, win, *extra_in)


SAMPLE_GATHER_SLOTS = 3


def _stage_c_kernel(alpha, sample_cfg, *refs):
    (x1_ref, attn_ref, conv_ref, g1_ref, sh2_ref, sc2_ref, g2_ref, lng1_ref, lnb1_ref, lng2_ref, lnb2_ref,
     ba_ref, bc_ref, wo_ref, wg_ref, wu_ref, wd_ref) = refs[:17]
    if sample_cfg is not None:
        past_len, n_pages = sample_cfg
        y_ref, so_ref = refs[25:27]
        step = pl.program_id(0) * pl.num_programs(1) + pl.program_id(1)
        ring_step = _sample_attention_stream(step, pl.num_programs(0) * pl.num_programs(1), past_len,
                                             n_pages, *refs[17:25], so_ref, *refs[27:30])
        ff_chunks = STREAM_FF_CHUNKS
        last = len(ff_chunks) - 1
        assert len(ff_chunks) + 2 == N_HEADS
        before_up = lambda ci: ring_step(ci + 1)
        before_down = lambda ci: ring_step(N_HEADS - 1) if ci == last else None
        ring_step(0)
    else:
        y_ref = refs[17]
        before_up, before_down, ff_chunks = None, None, FF_CHUNKS
    nseq, rows, d = x1_ref.shape
    m = nseq * rows
    a = (attn_ref[...] * ba_ref[...]).astype(BF16)
    c = (conv_ref[...] * bc_ref[...]).astype(BF16)
    mix = (jnp.dot(a, wo_ref[0:ATTN_WIDTH, :], preferred_element_type=F32)
           + jnp.dot(c, wo_ref[ATTN_WIDTH:, :], preferred_element_type=F32)).reshape(nseq, rows, d)
    x2 = _layernorm(alpha * x1_ref[...] + g1_ref[...] * mix, lng1_ref[...], lnb1_ref[...])
    h2 = (x2 * (1.0 + sc2_ref[...]) + sh2_ref[...]).reshape(m, d).astype(BF16)
    y = _ffn(h2, wg_ref, wu_ref, wd_ref, ff_chunks, before_up, before_down).reshape(nseq, rows, d)
    y_ref[...] = _layernorm(alpha * x2 + g2_ref[...] * (0.5 * y), lng2_ref[...], lnb2_ref[...])


def _stage_c(x1, attn, conv, ada, ln_g1, ln_b1, ln_g2, ln_b2, beta_a, beta_c, wo, wg, wu, wd,
             *, alpha, nseq_blk, rows_blk, sample=None):
    n_seq, s, _ = x1.shape
    tiles_per_seq = s // rows_blk
    assert nseq_blk == 1 or (nseq_blk == n_seq and tiles_per_seq == 1)
    m_blk = nseq_blk * rows_blk
    x_spec = pl.BlockSpec((nseq_blk, rows_blk, D_MODEL), lambda i, j: (i, j, 0))
    a_spec = pl.BlockSpec((nseq_blk, 1, D_MODEL), lambda i, j: (i, 0, 0))
    flat = pl.BlockSpec((m_blk, ATTN_WIDTH), lambda i, j: (i * tiles_per_seq + j, 0))
    out_specs = [x_spec]
    out_shape = [jax.ShapeDtypeStruct((n_seq, s, D_MODEL), F32)]
    extra_in, extra_in_specs, scratch, cfg = [], [], [], None
    if sample is not None:
        seqs, _, _, t = sample["qt"].shape
        assert seqs == n_seq * tiles_per_seq
        n_k = t * MOBA_TOPK * PAGES_PER_BLOCK
        cfg = (sample["past_len"], sample["n_pages"])
        smem = pl.BlockSpec(memory_space=pltpu.SMEM)
        hbm = pl.BlockSpec(memory_space=pl.ANY)
        seq_spec = pl.BlockSpec((1, N_HEADS, HEAD_DIM, t), lambda i, j: (i * tiles_per_seq + j, 0, 0, 0))
        extra_in = [sample["idx_flat"], sample["page_table_flat"], sample["slopes"],
                    sample["qt"], sample["knt"], sample["vnt"], sample["pool_kt"], sample["pool_vt"]]
        extra_in_specs = [smem, smem, smem, seq_spec, seq_spec, seq_spec, hbm, hbm]
        out_specs.append(seq_spec)
        out_shape.append(jax.ShapeDtypeStruct((seqs, N_HEADS, HEAD_DIM, t), F32))
        scratch = [pltpu.VMEM((SAMPLE_GATHER_SLOTS, n_k, HEAD_DIM, PAGE_SIZE), F32),
                   pltpu.VMEM((SAMPLE_GATHER_SLOTS, n_k, HEAD_DIM, PAGE_SIZE), F32),
                   pltpu.SemaphoreType.DMA((SAMPLE_GATHER_SLOTS,))]
    res = pl.pallas_call(
        functools.partial(_stage_c_kernel, alpha, cfg),
        grid=(n_seq // nseq_blk, tiles_per_seq),
        in_specs=[x_spec, flat, flat, a_spec, a_spec, a_spec, a_spec,
                  _const_spec((1, D_MODEL)), _const_spec((1, D_MODEL)),
                  _const_spec((1, D_MODEL)), _const_spec((1, D_MODEL)),
                  _const_spec((1, ATTN_WIDTH)), _const_spec((1, CONV_CH)),
                  _const_spec((D_MODEL, D_MODEL)),
                  _const_spec((D_MODEL, D_FF)), _const_spec((D_MODEL, D_FF)),
                  _const_spec((D_FF, D_MODEL))] + extra_in_specs,
        out_specs=out_specs,
        out_shape=out_shape,
        scratch_shapes=scratch,
        compiler_params=pltpu.CompilerParams(
            dimension_semantics=("arbitrary", "arbitrary"), vmem_limit_bytes=VMEM_LIMIT),
        name="stage_c",
    )(x1, attn, conv, ada["g1"], ada["sh2"], ada["sc2"], ada["g2"],
      ln_g1, ln_b1, ln_g2, ln_b2, beta_a, beta_c, wo, wg, wu, wd, *extra_in)
    return res if sample is not None else res[0]


def _conv_rows(buf_ref, halo, cur, w_ref, bdw_ref, gng_ref, gnb_ref, gavg_ref):
    nb, ch, c = cur.shape
    pad = CONV_PAD_ROWS
    buf_ref[:, 0:pad - HIST, :] = jnp.zeros((nb, pad - HIST, c), F32)
    buf_ref[:, pad - HIST:pad, :] = halo
    buf_ref[:, pad:pad + ch, :] = cur
    buf_ref[:, pad + ch:, :] = jnp.zeros((nb, SUBLANES, c), F32)
    acc = None
    for r in range(SUBLANES):
        part = None
        for a in range((pad + SUBLANES) // SUBLANES):
            j = SUBLANES * a + r - (pad - HIST)
            if 0 <= j < CONV_K:
                term = buf_ref[:, SUBLANES * a:SUBLANES * a + ch + SUBLANES, :] * w_ref[j:j + 1, :]
                part = term if part is None else part + term
        shifted = part[:, r:r + ch, :]
        acc = shifted if acc is None else acc + shifted
    y = (acc + bdw_ref[...]).reshape(nb * ch, c)
    gavg = gavg_ref[...]

    def group_mean(t):
        hi, lo = _split_bf16(t)
        return (jnp.dot(hi, gavg, preferred_element_type=F32)
                + jnp.dot(lo, gavg, preferred_element_type=F32))

    mu = group_mean(y)
    dlt = y - mu
    var = group_mean(dlt * dlt)
    z = dlt * lax.rsqrt(var + LN_EPS) * gng_ref[...] + gnb_ref[...]
    return _silu(z)


def _conv_kernel(n_chunks, hist_ref, prev_ref, cur_ref, w_ref, bdw_ref, gng_ref, gnb_ref, gavg_ref,
                 o_ref, buf_ref):
    nb, ch, c = cur_ref.shape
    if n_chunks == 1:
        halo = hist_ref[...]
    else:
        halo = jnp.where(pl.program_id(1) == 0, hist_ref[...], prev_ref[:, ch - HIST:, :])
    out = _conv_rows(buf_ref, halo, cur_ref[...], w_ref, bdw_ref, gng_ref, gnb_ref, gavg_ref)
    o_ref[...] = out.reshape(nb, ch, c)


def _group_average_matrix(c):
    grp = np.arange(c) // (c // GN_GROUPS)
    return jnp.asarray((grp[:, None] == grp[None, :]).astype(np.float32) / (c // GN_GROUPS), BF16)


def _conv_branch(hist, glu3, w_dw, b_dw, gn_g, gn_b, *, nb, ch):
    n_seq, s, c = glu3.shape
    n_chunks = s // ch
    assert ch >= HIST or n_chunks == 1
    gavg = _group_average_matrix(c)
    cur_spec = pl.BlockSpec((nb, ch, c), lambda i, j: (i, j, 0))
    prev_spec = pl.BlockSpec((nb, ch, c), lambda i, j: (i, jnp.maximum(j - 1, 0), 0))
    return pl.pallas_call(
        functools.partial(_conv_kernel, n_chunks),
        grid=(n_seq // nb, n_chunks),
        in_specs=[pl.BlockSpec((nb, HIST, c), lambda i, j: (i, 0, 0)), prev_spec, cur_spec,
                  pl.BlockSpec((CONV_K, c), lambda i, j: (0, 0)),
                  pl.BlockSpec((1, c), lambda i, j: (0, 0)),
                  pl.BlockSpec((1, c), lambda i, j: (0, 0)),
                  pl.BlockSpec((1, c), lambda i, j: (0, 0)),
                  pl.BlockSpec((c, c), lambda i, j: (0, 0))],
        out_specs=cur_spec,
        out_shape=jax.ShapeDtypeStruct((n_seq, s, c), F32),
        scratch_shapes=[pltpu.VMEM((nb, CONV_PAD_ROWS + ch + SUBLANES, c), F32)],
        compiler_params=pltpu.CompilerParams(dimension_semantics=("arbitrary", "arbitrary")),
        name="conv_branch",
    )(hist, glu3, glu3, w_dw, b_dw.reshape(1, c), gn_g.reshape(1, c), gn_b.reshape(1, c), gavg)


MOBA_HEADS_PER_STEP = 8


def _moba_prompt_kernel(slopes_ref, qt_ref, k16_ref, vt16_ref, kmean_ref, o_ref,
                        q16_ref, bias_ref, krs_ref, s_ref, m_ref, l_ref, acc_ref):
    grp0 = pl.program_id(1)
    qb = pl.program_id(2)
    blk = MOBA_BLOCK
    nb = kmean_ref.shape[1]
    hps = MOBA_HEADS_PER_STEP
    row_head = lax.broadcasted_iota(jnp.int32, (LANES, blk), 0) // HEAD_DIM
    key_i = lax.broadcasted_iota(jnp.int32, (blk, blk), 0)
    qry_i = lax.broadcasted_iota(jnp.int32, (blk, blk), 1)
    key_f = key_i.astype(F32)
    blk_id = lax.broadcasted_iota(jnp.int32, (nb, blk), 0)
    past = blk_id < qb
    zero_row = jnp.zeros((1, blk), jnp.int32)

    def lanes_of(hh):
        g = hh // HEADS_PER_GROUP
        return slice(g * LANES, (g + 1) * LANES)

    def scores(hh, n):
        kb = k16_ref[0, pl.ds(pl.multiple_of(n * blk, blk), blk), lanes_of(hh)]
        return jnp.dot(kb, q16_ref[hh], preferred_element_type=F32) + krs_ref[hh]

    def pv(hh, n, p):
        p16 = p.astype(BF16)
        acc = None
        for i in range(PAGES_PER_BLOCK):
            vt = vt16_ref[0, n * PAGES_PER_BLOCK + i, hh]
            part = jnp.dot(vt, p16[i * PAGE_SIZE:(i + 1) * PAGE_SIZE, :], preferred_element_type=F32)
            acc = part if acc is None else acc + part
        return acc

    @pl.when(qb == 0)
    def _():
        for hh in range(hps):
            krs_ref[hh] = key_f * slopes_ref[grp0 * hps + hh]

    for hh in range(hps):
        sub = hh % HEADS_PER_GROUP
        qh = jnp.where(row_head == sub, qt_ref[0, lanes_of(hh), :], 0.0)
        q16_ref[hh] = (qh * (SCALE * LOG2E)).astype(BF16)

        gate = jnp.dot(kmean_ref[0, :, lanes_of(hh)], qh, preferred_element_type=F32, precision=HIGHEST)
        cnt = jnp.zeros((nb, blk), F32)
        for mth in range(nb):
            other = gate[mth:mth + 1, :]
            ahead = (other > gate) | ((other == gate) & (blk_id > mth))
            cnt = cnt + jnp.where(ahead & (mth < qb), 1.0, 0.0)
        bias_ref[hh] = jnp.where((cnt < MOBA_TOPK) & past, 0.0, NEG_INF)

        s = jnp.where(key_i <= qry_i, scores(hh, qb), NEG_INF)
        s_ref[hh, qb] = s
        m_ref[hh] = jnp.max(s, axis=0, keepdims=True)

    def block_const(hh, n):
        rel = ((n - qb) * blk + zero_row).astype(F32)
        return slopes_ref[grp0 * hps + hh] * rel + bias_ref[hh, pl.ds(n, 1), :]

    def pass1(n, carry):
        for hh in range(hps):
            s = scores(hh, n)
            s_ref[hh, n] = s
            m_ref[hh] = jnp.maximum(m_ref[hh], jnp.max(s, axis=0, keepdims=True) + block_const(hh, n))
        return carry

    lax.fori_loop(0, qb, pass1, 0)

    def block_out(hh, n, c):
        p = jnp.exp2(s_ref[hh, n] - (m_ref[hh] - c))
        return jnp.sum(p, axis=0, keepdims=True), pv(hh, n, p)

    for hh in range(hps):
        l_ref[hh], acc_ref[hh] = block_out(hh, qb, 0.0)

    def pass2(n, carry):
        for hh in range(hps):
            l_part, acc_part = block_out(hh, n, block_const(hh, n))
            l_ref[hh] = l_ref[hh] + l_part
            acc_ref[hh] = acc_ref[hh] + acc_part
        return carry

    lax.fori_loop(0, qb, pass2, 0)
    out_t = jnp.concatenate([acc_ref[hh] / l_ref[hh] for hh in range(hps)], axis=0)
    o_ref[0] = out_t.T


def _moba_prompt(qt3, k16, vt16, kmean, slopes):
    b, s, _ = k16.shape
    hps = MOBA_HEADS_PER_STEP
    width = hps * HEAD_DIM
    n_blocks = s // MOBA_BLOCK
    return pl.pallas_call(
        _moba_prompt_kernel,
        grid=(b, N_HEADS // hps, n_blocks),
        in_specs=[pl.BlockSpec(memory_space=pltpu.SMEM),
                  pl.BlockSpec((1, width, MOBA_BLOCK), lambda i, g, j: (i, g, j)),
                  pl.BlockSpec((1, s, width), lambda i, g, j: (i, 0, g)),
                  pl.BlockSpec((1, s // PAGE_SIZE, hps, HEAD_DIM, PAGE_SIZE),
                               lambda i, g, j: (i, 0, g, 0, 0)),
                  pl.BlockSpec((1, n_blocks, width), lambda i, g, j: (i, 0, g))],
        out_specs=pl.BlockSpec((1, MOBA_BLOCK, width), lambda i, g, j: (i, j, g)),
        out_shape=jax.ShapeDtypeStruct((b, s, ATTN_WIDTH), F32),
        scratch_shapes=[pltpu.VMEM((hps, LANES, MOBA_BLOCK), BF16),
                        pltpu.VMEM((hps, n_blocks, MOBA_BLOCK), F32),
                        pltpu.VMEM((hps, MOBA_BLOCK, MOBA_BLOCK), F32),
                        pltpu.VMEM((hps, n_blocks, MOBA_BLOCK, MOBA_BLOCK), F32),
                        pltpu.VMEM((hps, 1, MOBA_BLOCK), F32),
                        pltpu.VMEM((hps, 1, MOBA_BLOCK), F32),
                        pltpu.VMEM((hps, HEAD_DIM, MOBA_BLOCK), F32)],
        compiler_params=pltpu.CompilerParams(
            dimension_semantics=("arbitrary", "arbitrary", "arbitrary"), vmem_limit_bytes=VMEM_LIMIT),
        name="moba_prompt",
    )(slopes, qt3, k16, vt16, kmean)


TOPK_SEQS_PER_STEP = 4


def _topk_kernel(q_ref, kmt_ref, o_ref):
    ns, t, nbp = q_ref.shape[0], q_ref.shape[1], kmt_ref.shape[3]
    rows = ns * N_HEADS * t
    gates = []
    for s in range(ns):
        q = q_ref[s]
        for h in range(N_HEADS):
            gates.append(jnp.dot(q[:, h * HEAD_DIM:(h + 1) * HEAD_DIM], kmt_ref[s, h],
                                 preferred_element_type=F32, precision=HIGHEST))
    g = jnp.concatenate(gates, axis=0)
    lane = lax.broadcasted_iota(jnp.int32, (rows, nbp), 1).astype(F32)
    out_lane = lax.broadcasted_iota(jnp.int32, (rows, LANES), 1)
    res = jnp.zeros((rows, LANES), F32)
    for r in range(MOBA_TOPK):
        mx = jnp.max(g, axis=1, keepdims=True)
        idx = jnp.min(jnp.where(g == mx, lane, float(nbp)), axis=1, keepdims=True)
        res = jnp.where(out_lane == r, idx, res)
        g = jnp.where(lane == idx, -jnp.inf, g)
    o_ref[...] = res.astype(jnp.int32).reshape(ns, N_HEADS, t, LANES)


def _topk_blocks(q3, kmean_t):
    n_seq, t, _ = q3.shape
    nbp = kmean_t.shape[3]
    ns = TOPK_SEQS_PER_STEP
    return pl.pallas_call(
        _topk_kernel,
        grid=(n_seq // ns,),
        in_specs=[pl.BlockSpec((ns, t, ATTN_WIDTH), lambda b: (b, 0, 0)),
                  pl.BlockSpec((ns, N_HEADS, HEAD_DIM, nbp), lambda b: (b, 0, 0, 0))],
        out_specs=pl.BlockSpec((ns, N_HEADS, t, LANES), lambda b: (b, 0, 0, 0)),
        out_shape=jax.ShapeDtypeStruct((n_seq, N_HEADS, t, LANES), jnp.int32),
        name="topk_blocks",
    )(q3, kmean_t)


def _sample_attention_stream(step, n_steps, past_len, n_pages, idx_ref, pt_ref, slopes_ref,
                             qt_ref, knt_ref, vnt_ref, pool_k_ref, pool_v_ref, o_ref,
                             kbuf_ref, vbuf_ref, sem_ref):
    t = qt_ref.shape[3]
    slabs = MOBA_TOPK * PAGES_PER_BLOCK
    n_k = t * slabs

    def slab_copies(page, head, slot, n):
        return (pltpu.make_async_copy(pool_k_ref.at[page, head], kbuf_ref.at[slot, n], sem_ref.at[slot]),
                pltpu.make_async_copy(pool_v_ref.at[page, head], vbuf_ref.at[slot, n], sem_ref.at[slot]))

    def start_gather(seq, head, slot):
        for n in range(n_k):
            tq, rem = divmod(n, slabs)
            j, i = divmod(rem, PAGES_PER_BLOCK)
            blk_idx = idx_ref[((seq * N_HEADS + head) * t + tq) * MOBA_TOPK + j]
            page = pt_ref[seq * n_pages + blk_idx * PAGES_PER_BLOCK + i]
            for cp in slab_copies(page, head, slot, n):
                cp.start()

    n_slots = kbuf_ref.shape[0]
    ahead = n_slots - 1
    assert ahead < N_HEADS

    def ring_step(h):
        unit = step * N_HEADS + h
        slot = lax.rem(unit, n_slots)
        if h == 0:
            @pl.when(step == 0)
            def _():
                for a in range(ahead):
                    start_gather(0, a, a)
        nxt_slot = lax.rem(unit + ahead, n_slots)
        if h + ahead < N_HEADS:
            start_gather(step, h + ahead, nxt_slot)
        else:
            @pl.when(step + 1 < n_steps)
            def _():
                start_gather(step + 1, h + ahead - N_HEADS, nxt_slot)
        for n in range(n_k):
            for cp in slab_copies(0, 0, slot, n):
                cp.wait()

        slope = slopes_ref[h]
        qt = qt_ref[0, h] * SCALE
        knt = knt_ref[0, h]
        vnt = vnt_ref[0, h]
        srow = lax.broadcasted_iota(jnp.int32, (slabs, PAGE_SIZE), 0)
        lane = lax.broadcasted_iota(jnp.int32, (slabs, PAGE_SIZE), 1)
        own_pos = lax.broadcasted_iota(jnp.int32, (1, t), 1)
        out_lane = lax.broadcasted_iota(jnp.int32, (HEAD_DIM, t), 1)
        out = jnp.zeros((HEAD_DIM, t), F32)
        for tq in range(t):
            q_col = qt[:, tq:tq + 1]
            base = ((step * N_HEADS + h) * t + tq) * MOBA_TOPK
            s_rows = []
            blk_of_row = jnp.zeros((slabs, PAGE_SIZE), jnp.int32)
            for j in range(MOBA_TOPK):
                blk_of_row = jnp.where(srow // PAGES_PER_BLOCK == j, idx_ref[base + j], blk_of_row)
                for i in range(PAGES_PER_BLOCK):
                    kt = kbuf_ref[slot, (tq * MOBA_TOPK + j) * PAGES_PER_BLOCK + i]
                    s_rows.append(jnp.sum(kt * q_col, axis=0, keepdims=True))
            s_sel = jnp.concatenate(s_rows, axis=0)
            k_pos = blk_of_row * MOBA_BLOCK + (srow % PAGES_PER_BLOCK) * PAGE_SIZE + lane
            s_sel = s_sel - slope * ((past_len + tq) - k_pos).astype(F32)
            rel_own = tq - own_pos
            s_own = jnp.sum(knt * q_col, axis=0, keepdims=True)
            s_own = jnp.where(rel_own >= 0, s_own - slope * rel_own.astype(F32), NEG_INF)
            m = jnp.maximum(jnp.max(jnp.max(s_sel, axis=1, keepdims=True), axis=0, keepdims=True),
                            jnp.max(s_own, axis=1, keepdims=True))
            p_sel = jnp.exp(s_sel - m)
            p_own = jnp.exp(s_own - m)
            l = (jnp.sum(jnp.sum(p_sel, axis=1, keepdims=True), axis=0, keepdims=True)
                 + jnp.sum(p_own, axis=1, keepdims=True))
            acc = jnp.zeros((HEAD_DIM, PAGE_SIZE), F32)
            for r in range(slabs):
                acc = acc + vbuf_ref[slot, tq * slabs + r] * p_sel[r:r + 1, :]
            o_col = (jnp.sum(acc, axis=1, keepdims=True) + jnp.sum(vnt * p_own, axis=1, keepdims=True)) / l
            out = jnp.where(out_lane == tq, o_col, out)
        o_ref[0, h] = out

    return ring_step


def _split_ada(ada_rows):
    a = ada_rows.reshape(ada_rows.shape[0], 3, 3, 1, D_MODEL)
    return {"sh0": a[:, 0, 0], "sc0": a[:, 0, 1], "g0": a[:, 0, 2],
            "sh1": a[:, 1, 0], "sc1": a[:, 1, 1], "g1": a[:, 1, 2],
            "sh2": a[:, 2, 0], "sc2": a[:, 2, 1], "g2": a[:, 2, 2]}


def _heads(t2, n_seq, s):
    return t2.reshape(n_seq, s, N_HEADS, HEAD_DIM)


def kernel(x_prompt, x_sample, cache_k, cache_v, state_conv, page_table, c_prompt, c_sample,
           w_ada, b_ada, ffn1_wg, ffn1_wu, ffn1_wd, w_in, w_dw, b_dw, gn_g, gn_b,
           beta_attn, beta_conv, w_out, ffn2_wg, ffn2_wu, ffn2_wd, ln_g, ln_b):
    depth = w_ada.shape[0]
    alpha = (2.0 * depth) ** 0.25
    batch, seq, _ = x_prompt.shape
    dec_batch, dec_seq, _ = x_sample.shape
    n_pages = page_table.shape[1]
    past_len = n_pages * PAGE_SIZE
    assert past_len % MOBA_BLOCK == 0 and dec_seq <= MOBA_BLOCK
    alibi = 2.0 ** (-8.0 * np.arange(1, N_HEADS + 1) / N_HEADS)
    slopes = jnp.asarray(alibi, F32)
    slopes_log2 = jnp.asarray(alibi * LOG2E, F32)
    page_table_flat = page_table.reshape(-1)

    y_p, y_s = x_prompt, x_sample
    outs = {name: [] for name in ("kp", "vp", "cp", "ks", "vs", "cs")}
    for l in range(depth):
        ada = _ada(jnp.concatenate([c_prompt, c_sample], axis=0), w_ada[l], b_ada[l])
        ada_p, ada_s = _split_ada(ada[:batch]), _split_ada(ada[batch:])
        lng = [ln_g[l, i].reshape(1, D_MODEL) for i in range(3)]
        lnb = [ln_b[l, i].reshape(1, D_MODEL) for i in range(3)]
        w1 = (ffn1_wg[l].astype(BF16), ffn1_wu[l].astype(BF16), ffn1_wd[l].astype(BF16))
        w2 = (ffn2_wg[l].astype(BF16), ffn2_wu[l].astype(BF16), ffn2_wd[l].astype(BF16))
        win16, wo16 = w_in[l].astype(BF16), w_out[l].astype(BF16)
        beta_a = beta_attn[l].reshape(1, ATTN_WIDTH)
        beta_c = beta_conv[l].reshape(1, CONV_CH)

        tile_p = dict(alpha=alpha, nseq_blk=1, rows_blk=512)
        tile_s = dict(alpha=alpha, nseq_blk=dec_batch, rows_blk=dec_seq)
        pool_kt = jnp.swapaxes(cache_k[l], -1, -2)
        pool_vt = jnp.swapaxes(cache_v[l], -1, -2)

        x1_s, q, k, v, glu_s = _stage_a(y_s, ada_s, lng[0], lnb[0], *w1, win16, paged=False, **tile_s)
        q4, k4, v4 = (_heads(t2, dec_batch, dec_seq) for t2 in (q, k, v))

        x1, qt, k16, kmean, ktp, vtp, vt16, glu, kmean_t = _stage_a(
            y_p, ada_p, lng[0], lnb[0], *w1, win16, paged=True, page_table_flat=page_table_flat,
            pool_kt=pool_kt, cache_seqs=dec_batch, **tile_p)
        idx = _topk_blocks(q.reshape(dec_batch, dec_seq, ATTN_WIDTH), kmean_t)[..., :MOBA_TOPK]
        attn = _moba_prompt(qt, k16.reshape(batch, seq, ATTN_WIDTH), vt16, kmean, slopes_log2)
        glu3 = glu.reshape(batch, seq, CONV_CH)
        conv = _conv_branch(jnp.zeros((batch, HIST, CONV_CH), F32), glu3, w_dw[l], b_dw[l],
                            gn_g[l], gn_b[l], nb=1, ch=256)
        to_t = lambda a4: a4.transpose(0, 2, 3, 1)
        sample = dict(qt=to_t(q4), knt=to_t(k4), vnt=to_t(v4), pool_kt=pool_kt, pool_vt=pool_vt,
                      idx_flat=idx.reshape(-1), page_table_flat=page_table_flat, slopes=slopes,
                      past_len=past_len, n_pages=n_pages)
        y_p, attn_t = _stage_c(x1, attn.reshape(batch * seq, ATTN_WIDTH),
                               conv.reshape(batch * seq, CONV_CH), ada_p, lng[1], lnb[1], lng[2], lnb[2],
                               beta_a, beta_c, wo16, *w2, sample=sample, **tile_p)
        outs["kp"].append(jnp.swapaxes(ktp, -1, -2))
        outs["vp"].append(jnp.swapaxes(vtp, -1, -2))
        outs["cp"].append(glu3[:, seq - HIST:, :])

        x1, glu = x1_s, glu_s
        attn = attn_t.transpose(0, 3, 1, 2).reshape(dec_batch * dec_seq, ATTN_WIDTH)
        glu3 = glu.reshape(dec_batch, dec_seq, CONV_CH)
        hist = state_conv[l]
        conv = _conv_branch(hist, glu3, w_dw[l], b_dw[l], gn_g[l], gn_b[l], nb=dec_batch, ch=dec_seq)
        y_s = _stage_c(x1, attn, conv.reshape(dec_batch * dec_seq, CONV_CH),
                       ada_s, lng[1], lnb[1], lng[2], lnb[2], beta_a, beta_c, wo16, *w2, **tile_s)
        outs["ks"].append(k4.transpose(0, 2, 1, 3))
        outs["vs"].append(v4.transpose(0, 2, 1, 3))
        outs["cs"].append(jnp.concatenate([hist, glu3], axis=1)[:, dec_seq:, :])

    stack = lambda name: jnp.stack(outs[name], 0)
    return (y_p, y_s, stack("kp"), stack("vp"), stack("cp"), stack("ks"), stack("vs"), stack("cs"))
```

```python
import functools

import numpy as np
import jax
import jax.numpy as jnp
from jax import lax
from jax.experimental import pallas as pl
from jax.experimental.pallas import tpu as pltpu

F32 = jnp.float32
BF16 = jnp.bfloat16
HIGHEST = lax.Precision.HIGHEST

D_MODEL = 1024
D_FF = 2816
N_HEADS = 8
HEAD_DIM = 64
ATTN_WIDTH = N_HEADS * HEAD_DIM
CONV_CH = 512
W_IN_COLS = 3 * ATTN_WIDTH + 2 * CONV_CH
CONV_K = 31
HIST = CONV_K - 1
GN_GROUPS = 8
MOBA_BLOCK = 256
MOBA_TOPK = 3
PAGE_SIZE = 128
PAGES_PER_BLOCK = MOBA_BLOCK // PAGE_SIZE
LN_EPS = 1e-5
NEG_INF = -1e30
SCALE = HEAD_DIM ** -0.5

LANES = 128
SUBLANES = 8
HEADS_PER_GROUP = LANES // HEAD_DIM
CONV_PAD_ROWS = 32
LOG2E = 1.4426950408889634
VMEM_LIMIT = 60 * 1024 * 1024

FF_CHUNKS = ((0, 1024), (1024, 2048), (2048, D_FF))
PROMPT_ROWS_PER_STEP = 512
CONV_ROWS_PER_STEP = 256


def _silu(x):
    return x / (1.0 + jnp.exp(-x))


def _sigmoid(x):
    return 1.0 / (1.0 + jnp.exp(-x))


def _layernorm(t, g, b):
    mu = jnp.mean(t, axis=-1, keepdims=True)
    d = t - mu
    var = jnp.mean(d * d, axis=-1, keepdims=True)
    return d * lax.rsqrt(var + LN_EPS) * g + b


def _ffn(h, wg_ref, wu_ref, wd_ref, chunks=FF_CHUNKS, before_up=None, before_down=None):
    acc = None
    for ci, (lo, hi) in enumerate(chunks):
        if before_up is not None:
            before_up(ci)
        g = jnp.dot(h, wg_ref[:, lo:hi], preferred_element_type=F32)
        u = jnp.dot(h, wu_ref[:, lo:hi], preferred_element_type=F32)
        a = (_silu(g) * u).astype(BF16)
        if before_down is not None:
            before_down(ci)
        y = jnp.dot(a, wd_ref[lo:hi, :], preferred_element_type=F32)
        acc = y if acc is None else acc + y
    return acc


def _split_bf16(t):
    hi = t.astype(BF16)
    return hi, (t - hi.astype(F32)).astype(BF16)


def _ada_kernel(c_ref, w_ref, b_ref, o_ref):
    a_hi, a_lo = _split_bf16(_silu(c_ref[...]))
    w_hi, w_lo = _split_bf16(w_ref[...])
    o_ref[...] = (jnp.dot(a_hi, w_hi, preferred_element_type=F32)
                  + jnp.dot(a_lo, w_hi, preferred_element_type=F32)
                  + jnp.dot(a_hi, w_lo, preferred_element_type=F32)) + b_ref[...]


def _ada(c_all, w_ada, b_ada):
    n = c_all.shape[0]
    tn = 1024
    return pl.pallas_call(
        _ada_kernel,
        grid=(w_ada.shape[1] // tn,),
        in_specs=[
            pl.BlockSpec((n, D_MODEL), lambda j: (0, 0)),
            pl.BlockSpec((D_MODEL, tn), lambda j: (0, j)),
            pl.BlockSpec((1, tn), lambda j: (0, j)),
        ],
        out_specs=pl.BlockSpec((n, tn), lambda j: (0, j)),
        out_shape=jax.ShapeDtypeStruct((n, w_ada.shape[1]), F32),
        name="ada",
    )(c_all, w_ada, b_ada.reshape(1, -1))


STREAM_CHUNK_PAGES = 16
STREAM_SLOTS = 3
STREAM_FF_CHUNKS = ((0, 512), (512, 1024), (1024, 1536), (1536, 2048), (2048, 2560), (2560, D_FF))
STREAM_HOOKS = (("up", 0), ("up", 1), ("down", 1), ("up", 2), ("up", 3), ("up", 4), ("down", 4), ("up", 5))
STREAM_CHUNKS_PER_STEP = len(STREAM_HOOKS)
W_IN_PARTS = ((0, ATTN_WIDTH), (ATTN_WIDTH, 2 * ATTN_WIDTH), (2 * ATTN_WIDTH, 3 * ATTN_WIDTH),
              (3 * ATTN_WIDTH, W_IN_COLS))


def _block_mean_stream(step, n_steps, pt_ref, pool_ref, kmt_ref, ring_ref, sem_ref):
    cps = STREAM_CHUNKS_PER_STEP
    ahead = STREAM_SLOTS - 1
    blocks_per_chunk = STREAM_CHUNK_PAGES // PAGES_PER_BLOCK
    n_cols = kmt_ref.shape[3]
    assert cps * blocks_per_chunk == n_cols and ahead < cps

    def page_copy(page, slot, i):
        return pltpu.make_async_copy(pool_ref.at[page], ring_ref.at[slot, i], sem_ref.at[slot])

    def start_chunk(g, slot):
        for i in range(STREAM_CHUNK_PAGES):
            page_copy(pt_ref[g * STREAM_CHUNK_PAGES + i], slot, i).start()

    def ring_step(c):
        g = step * cps + c
        slot = lax.rem(g, STREAM_SLOTS)
        if c == 0:
            @pl.when(step == 0)
            def _():
                for a in range(ahead):
                    start_chunk(a, a)
        nxt_slot = lax.rem(g + ahead, STREAM_SLOTS)
        if c + ahead < cps:
            start_chunk(g + ahead, nxt_slot)
        else:
            @pl.when(step + 1 < n_steps)
            def _():
                start_chunk(g + ahead, nxt_slot)
        for i in range(STREAM_CHUNK_PAGES):
            page_copy(0, slot, i).wait()
        col_id = lax.broadcasted_iota(jnp.int32, (HEAD_DIM, n_cols), 1)
        cols = slice(c * blocks_per_chunk, (c + 1) * blocks_per_chunk)
        for h in range(N_HEADS):
            acc = jnp.zeros((HEAD_DIM, n_cols), F32)
            for jb in range(blocks_per_chunk):
                tot = ring_ref[slot, jb * PAGES_PER_BLOCK, h]
                for i in range(1, PAGES_PER_BLOCK):
                    tot = tot + ring_ref[slot, jb * PAGES_PER_BLOCK + i, h]
                mean = jnp.sum(tot, axis=1, keepdims=True) * (1.0 / MOBA_BLOCK)
                acc = jnp.where(col_id == c * blocks_per_chunk + jb, mean, acc)
            kmt_ref[0, h, :, cols] = acc[:, cols]

    return ring_step


def _first_ffn(alpha, x_ref, sh0_ref, sc0_ref, g0_ref, lng_ref, lnb_ref, wg_ref, wu_ref, wd_ref,
               ff_chunks, before_up=None, before_down=None):
    nseq, rows, d = x_ref.shape
    x = x_ref[...]
    h0 = (x * (1.0 + sc0_ref[...]) + sh0_ref[...]).reshape(nseq * rows, d).astype(BF16)
    y = _ffn(h0, wg_ref, wu_ref, wd_ref, ff_chunks, before_up, before_down).reshape(nseq, rows, d)
    return _layernorm(alpha * x + g0_ref[...] * (0.5 * y), lng_ref[...], lnb_ref[...])


def _project(x1, sh1_ref, sc1_ref, win_ref):
    nseq, rows, d = x1.shape
    h1 = (x1 * (1.0 + sc1_ref[...]) + sh1_ref[...]).reshape(nseq * rows, d).astype(BF16)
    q, k, v, ug = (jnp.dot(h1, win_ref[:, lo:hi], preferred_element_type=F32) for lo, hi in W_IN_PARTS)
    return q, k, v, ug[:, 0:CONV_CH] * _sigmoid(ug[:, CONV_CH:])


def _stage_a_sample_kernel(alpha, x_ref, sh0_ref, sc0_ref, g0_ref, sh1_ref, sc1_ref, lng_ref, lnb_ref,
                           wg_ref, wu_ref, wd_ref, win_ref, x1_ref, q_ref, k_ref, v_ref, glu_ref):
    x1 = _first_ffn(alpha, x_ref, sh0_ref, sc0_ref, g0_ref, lng_ref, lnb_ref, wg_ref, wu_ref, wd_ref,
                    FF_CHUNKS)
    x1_ref[...] = x1
    q_ref[...], k_ref[...], v_ref[...], glu_ref[...] = _project(x1, sh1_ref, sc1_ref, win_ref)


def _stage_a_prompt_kernel(alpha, x_ref, sh0_ref, sc0_ref, g0_ref, lng_ref, lnb_ref, wg_ref, wu_ref, wd_ref,
                           pt_ref, pool_ref, x1_ref, kmt_ref, ring_ref, sem_ref):
    step = pl.program_id(0) * pl.num_programs(1) + pl.program_id(1)
    ring_step = _block_mean_stream(step, pl.num_programs(0) * pl.num_programs(1),
                                   pt_ref, pool_ref, kmt_ref, ring_ref, sem_ref)

    def hook(kind):
        return lambda ci: ring_step(STREAM_HOOKS.index((kind, ci))) if (kind, ci) in STREAM_HOOKS else None

    x1_ref[...] = _first_ffn(alpha, x_ref, sh0_ref, sc0_ref, g0_ref, lng_ref, lnb_ref, wg_ref, wu_ref, wd_ref,
                             STREAM_FF_CHUNKS, hook("up"), hook("down"))


def _stage_b_kernel(x1_ref, sh1_ref, sc1_ref, win_ref,
                    qt_ref, k16_ref, kmean_ref, ktp_ref, vtp_ref, vt16_ref, glu_ref):
    q, k, v, glu = _project(x1_ref[...], sh1_ref, sc1_ref, win_ref)
    m = q.shape[0]
    qt_ref[0] = q.T
    k16_ref[...] = k.astype(BF16)
    glu_ref[...] = glu
    blocks_per_tile = m // MOBA_BLOCK
    means = [jnp.sum(k[n * MOBA_BLOCK:(n + 1) * MOBA_BLOCK, :], axis=0, keepdims=True)
             * (1.0 / MOBA_BLOCK) for n in range(blocks_per_tile)]
    tile = pl.program_id(1)
    for jj in range(kmean_ref.shape[1] // blocks_per_tile):
        @pl.when(tile == jj)
        def _():
            for n in range(blocks_per_tile):
                kmean_ref[0, jj * blocks_per_tile + n:jj * blocks_per_tile + n + 1, :] = means[n]
    kt = k.T
    vt = v.T
    for p in range(m // PAGE_SIZE):
        for h in range(N_HEADS):
            rs = slice(h * HEAD_DIM, (h + 1) * HEAD_DIM)
            cs = slice(p * PAGE_SIZE, (p + 1) * PAGE_SIZE)
            ktp_ref[0, p, h] = kt[rs, cs]
            vtp_ref[0, p, h] = vt[rs, cs]
            vt16_ref[0, p, h] = vt[rs, cs].astype(BF16)


def _const_spec(shape):
    return pl.BlockSpec(shape, lambda i, j: (0,) * len(shape), pipeline_mode=pl.Buffered(1))


def _stage_a_sample(x, ada, ln_g, ln_b, wg, wu, wd, win, *, alpha):
    n_seq, s, _ = x.shape
    n_tok = n_seq * s
    x_spec = pl.BlockSpec((n_seq, s, D_MODEL), lambda i, j: (0, 0, 0))
    a_spec = pl.BlockSpec((n_seq, 1, D_MODEL), lambda i, j: (0, 0, 0))
    flat_spec = pl.BlockSpec((n_tok, ATTN_WIDTH), lambda i, j: (0, 0))
    flat_shape = jax.ShapeDtypeStruct((n_tok, ATTN_WIDTH), F32)
    return pl.pallas_call(
        functools.partial(_stage_a_sample_kernel, alpha),
        grid=(1, 1),
        in_specs=[x_spec, a_spec, a_spec, a_spec, a_spec, a_spec,
                  _const_spec((1, D_MODEL)), _const_spec((1, D_MODEL)),
                  _const_spec((D_MODEL, D_FF)), _const_spec((D_MODEL, D_FF)),
                  _const_spec((D_FF, D_MODEL)), _const_spec((D_MODEL, W_IN_COLS))],
        out_specs=[x_spec, flat_spec, flat_spec, flat_spec, flat_spec],
        out_shape=[jax.ShapeDtypeStruct((n_seq, s, D_MODEL), F32)] + [flat_shape] * 4,
        compiler_params=pltpu.CompilerParams(
            dimension_semantics=("arbitrary", "arbitrary"), vmem_limit_bytes=VMEM_LIMIT),
        name="stage_a_sample",
    )(x, ada["sh0"], ada["sc0"], ada["g0"], ada["sh1"], ada["sc1"], ln_g, ln_b, wg, wu, wd, win)


def _stage_a_prompt(x, ada, ln_g, ln_b, wg, wu, wd, page_table_flat, pool_kt, cache_seqs, *, alpha, rows_blk):
    n_seq, s, _ = x.shape
    tiles_per_seq = s // rows_blk
    x_spec = pl.BlockSpec((1, rows_blk, D_MODEL), lambda i, j: (i, j, 0))
    a_spec = pl.BlockSpec((1, 1, D_MODEL), lambda i, j: (i, 0, 0))
    n_steps = n_seq * tiles_per_seq
    pages_per_seq = page_table_flat.shape[0] // cache_seqs
    assert cache_seqs == n_steps and pages_per_seq == STREAM_CHUNKS_PER_STEP * STREAM_CHUNK_PAGES
    blocks_per_seq = pages_per_seq // PAGES_PER_BLOCK
    return pl.pallas_call(
        functools.partial(_stage_a_prompt_kernel, alpha),
        grid=(n_seq, tiles_per_seq),
        in_specs=[x_spec, a_spec, a_spec, a_spec,
                  _const_spec((1, D_MODEL)), _const_spec((1, D_MODEL)),
                  _const_spec((D_MODEL, D_FF)), _const_spec((D_MODEL, D_FF)), _const_spec((D_FF, D_MODEL)),
                  pl.BlockSpec(memory_space=pltpu.SMEM), pl.BlockSpec(memory_space=pl.ANY)],
        out_specs=[x_spec, pl.BlockSpec((1, N_HEADS, HEAD_DIM, blocks_per_seq),
                                        lambda i, j: (i * tiles_per_seq + j, 0, 0, 0))],
        out_shape=[jax.ShapeDtypeStruct((n_seq, s, D_MODEL), F32),
                   jax.ShapeDtypeStruct((cache_seqs, N_HEADS, HEAD_DIM, blocks_per_seq), F32)],
        scratch_shapes=[pltpu.VMEM((STREAM_SLOTS, STREAM_CHUNK_PAGES, N_HEADS, HEAD_DIM, PAGE_SIZE), F32),
                        pltpu.SemaphoreType.DMA((STREAM_SLOTS,))],
        compiler_params=pltpu.CompilerParams(
            dimension_semantics=("arbitrary", "arbitrary"), vmem_limit_bytes=VMEM_LIMIT),
        name="stage_a_prompt",
    )(x, ada["sh0"], ada["sc0"], ada["g0"], ln_g, ln_b, wg, wu, wd, page_table_flat, pool_kt)


def _stage_b(x1, ada, win, *, rows_blk):
    n_seq, s, _ = x1.shape
    tiles_per_seq = s // rows_blk
    assert rows_blk % MOBA_BLOCK == 0 and MOBA_BLOCK % PAGE_SIZE == 0
    n_tok = n_seq * s
    x_spec = pl.BlockSpec((1, rows_blk, D_MODEL), lambda i, j: (i, j, 0))
    a_spec = pl.BlockSpec((1, 1, D_MODEL), lambda i, j: (i, 0, 0))
    flat_spec = pl.BlockSpec((rows_blk, ATTN_WIDTH), lambda i, j: (i * tiles_per_seq + j, 0))
    page_dims = (n_seq, s // PAGE_SIZE, N_HEADS, HEAD_DIM, PAGE_SIZE)
    page_spec = pl.BlockSpec((1, rows_blk // PAGE_SIZE, N_HEADS, HEAD_DIM, PAGE_SIZE),
                             lambda i, j: (i, j, 0, 0, 0))
    return pl.pallas_call(
        _stage_b_kernel,
        grid=(n_seq, tiles_per_seq),
        in_specs=[x_spec, a_spec, a_spec, _const_spec((D_MODEL, W_IN_COLS))],
        out_specs=[pl.BlockSpec((1, ATTN_WIDTH, rows_blk), lambda i, j: (i, 0, j)),
                   flat_spec,
                   pl.BlockSpec((1, s // MOBA_BLOCK, ATTN_WIDTH), lambda i, j: (i, 0, 0)),
                   page_spec, page_spec, page_spec, flat_spec],
        out_shape=[jax.ShapeDtypeStruct((n_seq, ATTN_WIDTH, s), F32),
                   jax.ShapeDtypeStruct((n_tok, ATTN_WIDTH), BF16),
                   jax.ShapeDtypeStruct((n_seq, s // MOBA_BLOCK, ATTN_WIDTH), F32),
                   jax.ShapeDtypeStruct(page_dims, F32), jax.ShapeDtypeStruct(page_dims, F32),
                   jax.ShapeDtypeStruct(page_dims, BF16),
                   jax.ShapeDtypeStruct((n_tok, CONV_CH), F32)],
        compiler_params=pltpu.CompilerParams(
            dimension_semantics=("arbitrary", "arbitrary"), vmem_limit_bytes=VMEM_LIMIT),
        name="stage_b",
    )(x1, ada["sh1"], ada["sc1"], win)


SAMPLE_GATHER_SLOTS = 3


def _stage_c_kernel(alpha, sample_cfg, *refs):
    (x1_ref, attn_ref, conv_ref, g1_ref, sh2_ref, sc2_ref, g2_ref, lng1_ref, lnb1_ref, lng2_ref, lnb2_ref,
     ba_ref, bc_ref, wo_ref, wg_ref, wu_ref, wd_ref) = refs[:17]
    if sample_cfg is not None:
        past_len, n_pages = sample_cfg
        y_ref, so_ref = refs[25:27]
        step = pl.program_id(0) * pl.num_programs(1) + pl.program_id(1)
        ring_step = _sample_attention_stream(step, pl.num_programs(0) * pl.num_programs(1), past_len,
                                             n_pages, *refs[17:25], so_ref, *refs[27:30])
        ff_chunks = STREAM_FF_CHUNKS
        last = len(ff_chunks) - 1
        assert len(ff_chunks) + 2 == N_HEADS
        before_up = lambda ci: ring_step(ci + 1)
        before_down = lambda ci: ring_step(N_HEADS - 1) if ci == last else None
        ring_step(0)
    else:
        y_ref = refs[17]
        before_up, before_down, ff_chunks = None, None, FF_CHUNKS
    nseq, rows, d = x1_ref.shape
    m = nseq * rows
    a = (attn_ref[...] * ba_ref[...]).astype(BF16)
    c = (conv_ref[...] * bc_ref[...]).astype(BF16)
    mix = (jnp.dot(a, wo_ref[0:ATTN_WIDTH, :], preferred_element_type=F32)
           + jnp.dot(c, wo_ref[ATTN_WIDTH:, :], preferred_element_type=F32)).reshape(nseq, rows, d)
    x2 = _layernorm(alpha * x1_ref[...] + g1_ref[...] * mix, lng1_ref[...], lnb1_ref[...])
    h2 = (x2 * (1.0 + sc2_ref[...]) + sh2_ref[...]).reshape(m, d).astype(BF16)
    y = _ffn(h2, wg_ref, wu_ref, wd_ref, ff_chunks, before_up, before_down).reshape(nseq, rows, d)
    y_ref[...] = _layernorm(alpha * x2 + g2_ref[...] * (0.5 * y), lng2_ref[...], lnb2_ref[...])


def _stage_c(x1, attn, conv, ada, ln_g1, ln_b1, ln_g2, ln_b2, beta_a, beta_c, wo, wg, wu, wd,
             *, alpha, nseq_blk, rows_blk, sample=None):
    n_seq, s, _ = x1.shape
    tiles_per_seq = s // rows_blk
    assert nseq_blk == 1 or (nseq_blk == n_seq and tiles_per_seq == 1)
    m_blk = nseq_blk * rows_blk
    x_spec = pl.BlockSpec((nseq_blk, rows_blk, D_MODEL), lambda i, j: (i, j, 0))
    a_spec = pl.BlockSpec((nseq_blk, 1, D_MODEL), lambda i, j: (i, 0, 0))
    flat = pl.BlockSpec((m_blk, ATTN_WIDTH), lambda i, j: (i * tiles_per_seq + j, 0))
    out_specs = [x_spec]
    out_shape = [jax.ShapeDtypeStruct((n_seq, s, D_MODEL), F32)]
    extra_in, extra_in_specs, scratch, cfg = [], [], [], None
    if sample is not None:
        seqs, _, _, t = sample["qt"].shape
        assert seqs == n_seq * tiles_per_seq
        n_k = t * MOBA_TOPK * PAGES_PER_BLOCK
        cfg = (sample["past_len"], sample["n_pages"])
        smem = pl.BlockSpec(memory_space=pltpu.SMEM)
        hbm = pl.BlockSpec(memory_space=pl.ANY)
        seq_spec = pl.BlockSpec((1, N_HEADS, HEAD_DIM, t), lambda i, j: (i * tiles_per_seq + j, 0, 0, 0))
        extra_in = [sample["idx_flat"], sample["page_table_flat"], sample["slopes"],
                    sample["qt"], sample["knt"], sample["vnt"], sample["pool_kt"], sample["pool_vt"]]
        extra_in_specs = [smem, smem, smem, seq_spec, seq_spec, seq_spec, hbm, hbm]
        out_specs.append(seq_spec)
        out_shape.append(jax.ShapeDtypeStruct((seqs, N_HEADS, HEAD_DIM, t), F32))
        scratch = [pltpu.VMEM((SAMPLE_GATHER_SLOTS, n_k, HEAD_DIM, PAGE_SIZE), F32),
                   pltpu.VMEM((SAMPLE_GATHER_SLOTS, n_k, HEAD_DIM, PAGE_SIZE), F32),
                   pltpu.SemaphoreType.DMA((SAMPLE_GATHER_SLOTS,))]
    res = pl.pallas_call(
        functools.partial(_stage_c_kernel, alpha, cfg),
        grid=(n_seq // nseq_blk, tiles_per_seq),
        in_specs=[x_spec, flat, flat, a_spec, a_spec, a_spec, a_spec,
                  _const_spec((1, D_MODEL)), _const_spec((1, D_MODEL)),
                  _const_spec((1, D_MODEL)), _const_spec((1, D_MODEL)),
                  _const_spec((1, ATTN_WIDTH)), _const_spec((1, CONV_CH)),
                  _const_spec((D_MODEL, D_MODEL)),
                  _const_spec((D_MODEL, D_FF)), _const_spec((D_MODEL, D_FF)),
                  _const_spec((D_FF, D_MODEL))] + extra_in_specs,
        out_specs=out_specs,
        out_shape=out_shape,
        scratch_shapes=scratch,
        compiler_params=pltpu.CompilerParams(
            dimension_semantics=("arbitrary", "arbitrary"), vmem_limit_bytes=VMEM_LIMIT),
        name="stage_c",
    )(x1, attn, conv, ada["g1"], ada["sh2"], ada["sc2"], ada["g2"],
      ln_g1, ln_b1, ln_g2, ln_b2, beta_a, beta_c, wo, wg, wu, wd, *extra_in)
    return res if sample is not None else res[0]


def _conv_rows(buf_ref, halo, cur, w_ref, bdw_ref, gng_ref, gnb_ref, gavg_ref):
    nb, ch, c = cur.shape
    pad = CONV_PAD_ROWS
    buf_ref[:, 0:pad - HIST, :] = jnp.zeros((nb, pad - HIST, c), F32)
    buf_ref[:, pad - HIST:pad, :] = halo
    buf_ref[:, pad:pad + ch, :] = cur
    buf_ref[:, pad + ch:, :] = jnp.zeros((nb, SUBLANES, c), F32)
    acc = None
    for r in range(SUBLANES):
        part = None
        for a in range((pad + SUBLANES) // SUBLANES):
            j = SUBLANES * a + r - (pad - HIST)
            if 0 <= j < CONV_K:
                term = buf_ref[:, SUBLANES * a:SUBLANES * a + ch + SUBLANES, :] * w_ref[j:j + 1, :]
                part = term if part is None else part + term
        shifted = part[:, r:r + ch, :]
        acc = shifted if acc is None else acc + shifted
    y = (acc + bdw_ref[...]).reshape(nb * ch, c)
    gavg = gavg_ref[...]

    def group_mean(t):
        hi, lo = _split_bf16(t)
        return (jnp.dot(hi, gavg, preferred_element_type=F32)
                + jnp.dot(lo, gavg, preferred_element_type=F32))

    mu = group_mean(y)
    dlt = y - mu
    var = group_mean(dlt * dlt)
    z = dlt * lax.rsqrt(var + LN_EPS) * gng_ref[...] + gnb_ref[...]
    return _silu(z)


def _conv_kernel(n_chunks, hist_ref, prev_ref, cur_ref, w_ref, bdw_ref, gng_ref, gnb_ref, gavg_ref,
                 o_ref, buf_ref):
    nb, ch, c = cur_ref.shape
    if n_chunks == 1:
        halo = hist_ref[...]
    else:
        halo = jnp.where(pl.program_id(1) == 0, hist_ref[...], prev_ref[:, ch - HIST:, :])
    out = _conv_rows(buf_ref, halo, cur_ref[...], w_ref, bdw_ref, gng_ref, gnb_ref, gavg_ref)
    o_ref[...] = out.reshape(nb, ch, c)


def _group_average_matrix(c):
    grp = np.arange(c) // (c // GN_GROUPS)
    return jnp.asarray((grp[:, None] == grp[None, :]).astype(np.float32) / (c // GN_GROUPS), BF16)


def _conv_branch(hist, glu3, w_dw, b_dw, gn_g, gn_b, *, nb, ch):
    n_seq, s, c = glu3.shape
    n_chunks = s // ch
    assert ch >= HIST or n_chunks == 1
    gavg = _group_average_matrix(c)
    cur_spec = pl.BlockSpec((nb, ch, c), lambda i, j: (i, j, 0))
    prev_spec = pl.BlockSpec((nb, ch, c), lambda i, j: (i, jnp.maximum(j - 1, 0), 0))
    return pl.pallas_call(
        functools.partial(_conv_kernel, n_chunks),
        grid=(n_seq // nb, n_chunks),
        in_specs=[pl.BlockSpec((nb, HIST, c), lambda i, j: (i, 0, 0)), prev_spec, cur_spec,
                  pl.BlockSpec((CONV_K, c), lambda i, j: (0, 0)),
                  pl.BlockSpec((1, c), lambda i, j: (0, 0)),
                  pl.BlockSpec((1, c), lambda i, j: (0, 0)),
                  pl.BlockSpec((1, c), lambda i, j: (0, 0)),
                  pl.BlockSpec((c, c), lambda i, j: (0, 0))],
        out_specs=cur_spec,
        out_shape=jax.ShapeDtypeStruct((n_seq, s, c), F32),
        scratch_shapes=[pltpu.VMEM((nb, CONV_PAD_ROWS + ch + SUBLANES, c), F32)],
        compiler_params=pltpu.CompilerParams(dimension_semantics=("arbitrary", "arbitrary")),
        name="conv_branch",
    )(hist, glu3, glu3, w_dw, b_dw.reshape(1, c), gn_g.reshape(1, c), gn_b.reshape(1, c), gavg)


MOBA_HEADS_PER_STEP = 8


def _moba_prompt_kernel(slopes_ref, qt_ref, k16_ref, vt16_ref, kmean_ref, o_ref,
                        q16_ref, bias_ref, krs_ref, s_ref, m_ref, l_ref, acc_ref):
    grp0 = pl.program_id(1)
    qb = pl.program_id(2)
    blk = MOBA_BLOCK
    nb = kmean_ref.shape[1]
    hps = MOBA_HEADS_PER_STEP
    row_head = lax.broadcasted_iota(jnp.int32, (LANES, blk), 0) // HEAD_DIM
    key_i = lax.broadcasted_iota(jnp.int32, (blk, blk), 0)
    qry_i = lax.broadcasted_iota(jnp.int32, (blk, blk), 1)
    key_f = key_i.astype(F32)
    blk_id = lax.broadcasted_iota(jnp.int32, (nb, blk), 0)
    past = blk_id < qb
    zero_row = jnp.zeros((1, blk), jnp.int32)

    def lanes_of(hh):
        g = hh // HEADS_PER_GROUP
        return slice(g * LANES, (g + 1) * LANES)

    def scores(hh, n):
        kb = k16_ref[0, pl.ds(pl.multiple_of(n * blk, blk), blk), lanes_of(hh)]
        return jnp.dot(kb, q16_ref[hh], preferred_element_type=F32) + krs_ref[hh]

    def pv(hh, n, p):
        p16 = p.astype(BF16)
        acc = None
        for i in range(PAGES_PER_BLOCK):
            vt = vt16_ref[0, n * PAGES_PER_BLOCK + i, hh]
            part = jnp.dot(vt, p16[i * PAGE_SIZE:(i + 1) * PAGE_SIZE, :], preferred_element_type=F32)
            acc = part if acc is None else acc + part
        return acc

    @pl.when(qb == 0)
    def _():
        for hh in range(hps):
            krs_ref[hh] = key_f * slopes_ref[grp0 * hps + hh]

    for hh in range(hps):
        sub = hh % HEADS_PER_GROUP
        qh = jnp.where(row_head == sub, qt_ref[0, lanes_of(hh), :], 0.0)
        q16_ref[hh] = (qh * (SCALE * LOG2E)).astype(BF16)

        gate = jnp.dot(kmean_ref[0, :, lanes_of(hh)], qh, preferred_element_type=F32, precision=HIGHEST)
        cnt = jnp.zeros((nb, blk), F32)
        for mth in range(nb):
            other = gate[mth:mth + 1, :]
            ahead = (other > gate) | ((other == gate) & (blk_id > mth))
            cnt = cnt + jnp.where(ahead & (mth < qb), 1.0, 0.0)
        bias_ref[hh] = jnp.where((cnt < MOBA_TOPK) & past, 0.0, NEG_INF)

        s = jnp.where(key_i <= qry_i, scores(hh, qb), NEG_INF)
        s_ref[hh, qb] = s
        m_ref[hh] = jnp.max(s, axis=0, keepdims=True)

    def block_const(hh, n):
        rel = ((n - qb) * blk + zero_row).astype(F32)
        return slopes_ref[grp0 * hps + hh] * rel + bias_ref[hh, pl.ds(n, 1), :]

    def pass1(n, carry):
        for hh in range(hps):
            s = scores(hh, n)
            s_ref[hh, n] = s
            m_ref[hh] = jnp.maximum(m_ref[hh], jnp.max(s, axis=0, keepdims=True) + block_const(hh, n))
        return carry

    lax.fori_loop(0, qb, pass1, 0)

    def block_out(hh, n, c):
        p = jnp.exp2(s_ref[hh, n] - (m_ref[hh] - c))
        return jnp.sum(p, axis=0, keepdims=True), pv(hh, n, p)

    for hh in range(hps):
        l_ref[hh], acc_ref[hh] = block_out(hh, qb, 0.0)

    def pass2(n, carry):
        for hh in range(hps):
            l_part, acc_part = block_out(hh, n, block_const(hh, n))
            l_ref[hh] = l_ref[hh] + l_part
            acc_ref[hh] = acc_ref[hh] + acc_part
        return carry

    lax.fori_loop(0, qb, pass2, 0)
    out_t = jnp.concatenate([acc_ref[hh] / l_ref[hh] for hh in range(hps)], axis=0)
    o_ref[0] = out_t.T


def _moba_prompt(qt3, k16, vt16, kmean, slopes):
    b, s, _ = k16.shape
    hps = MOBA_HEADS_PER_STEP
    width = hps * HEAD_DIM
    n_blocks = s // MOBA_BLOCK
    return pl.pallas_call(
        _moba_prompt_kernel,
        grid=(b, N_HEADS // hps, n_blocks),
        in_specs=[pl.BlockSpec(memory_space=pltpu.SMEM),
                  pl.BlockSpec((1, width, MOBA_BLOCK), lambda i, g, j: (i, g, j)),
                  pl.BlockSpec((1, s, width), lambda i, g, j: (i, 0, g)),
                  pl.BlockSpec((1, s // PAGE_SIZE, hps, HEAD_DIM, PAGE_SIZE),
                               lambda i, g, j: (i, 0, g, 0, 0)),
                  pl.BlockSpec((1, n_blocks, width), lambda i, g, j: (i, 0, g))],
        out_specs=pl.BlockSpec((1, MOBA_BLOCK, width), lambda i, g, j: (i, j, g)),
        out_shape=jax.ShapeDtypeStruct((b, s, ATTN_WIDTH), F32),
        scratch_shapes=[pltpu.VMEM((hps, LANES, MOBA_BLOCK), BF16),
                        pltpu.VMEM((hps, n_blocks, MOBA_BLOCK), F32),
                        pltpu.VMEM((hps, MOBA_BLOCK, MOBA_BLOCK), F32),
                        pltpu.VMEM((hps, n_blocks, MOBA_BLOCK, MOBA_BLOCK), F32),
                        pltpu.VMEM((hps, 1, MOBA_BLOCK), F32),
                        pltpu.VMEM((hps, 1, MOBA_BLOCK), F32),
                        pltpu.VMEM((hps, HEAD_DIM, MOBA_BLOCK), F32)],
        compiler_params=pltpu.CompilerParams(
            dimension_semantics=("arbitrary", "arbitrary", "arbitrary"), vmem_limit_bytes=VMEM_LIMIT),
        name="moba_prompt",
    )(slopes, qt3, k16, vt16, kmean)


TOPK_SEQS_PER_STEP = 4


def _topk_kernel(q_ref, kmt_ref, o_ref):
    ns, t, nbp = q_ref.shape[0], q_ref.shape[1], kmt_ref.shape[3]
    rows = ns * N_HEADS * t
    gates = []
    for s in range(ns):
        q = q_ref[s]
        for h in range(N_HEADS):
            gates.append(jnp.dot(q[:, h * HEAD_DIM:(h + 1) * HEAD_DIM], kmt_ref[s, h],
                                 preferred_element_type=F32, precision=HIGHEST))
    g = jnp.concatenate(gates, axis=0)
    lane = lax.broadcasted_iota(jnp.int32, (rows, nbp), 1).astype(F32)
    out_lane = lax.broadcasted_iota(jnp.int32, (rows, LANES), 1)
    res = jnp.zeros((rows, LANES), F32)
    for r in range(MOBA_TOPK):
        mx = jnp.max(g, axis=1, keepdims=True)
        idx = jnp.min(jnp.where(g == mx, lane, float(nbp)), axis=1, keepdims=True)
        res = jnp.where(out_lane == r, idx, res)
        g = jnp.where(lane == idx, -jnp.inf, g)
    o_ref[...] = res.astype(jnp.int32).reshape(ns, N_HEADS, t, LANES)


def _topk_blocks(q3, kmean_t):
    n_seq, t, _ = q3.shape
    nbp = kmean_t.shape[3]
    ns = TOPK_SEQS_PER_STEP
    return pl.pallas_call(
        _topk_kernel,
        grid=(n_seq // ns,),
        in_specs=[pl.BlockSpec((ns, t, ATTN_WIDTH), lambda b: (b, 0, 0)),
                  pl.BlockSpec((ns, N_HEADS, HEAD_DIM, nbp), lambda b: (b, 0, 0, 0))],
        out_specs=pl.BlockSpec((ns, N_HEADS, t, LANES), lambda b: (b, 0, 0, 0)),
        out_shape=jax.ShapeDtypeStruct((n_seq, N_HEADS, t, LANES), jnp.int32),
        name="topk_blocks",
    )(q3, kmean_t)


def _sample_attention_stream(step, n_steps, past_len, n_pages, idx_ref, pt_ref, slopes_ref,
                             qt_ref, knt_ref, vnt_ref, pool_k_ref, pool_v_ref, o_ref,
                             kbuf_ref, vbuf_ref, sem_ref):
    t = qt_ref.shape[3]
    slabs = MOBA_TOPK * PAGES_PER_BLOCK
    n_k = t * slabs

    def slab_copies(page, head, slot, n):
        return (pltpu.make_async_copy(pool_k_ref.at[page, head], kbuf_ref.at[slot, n], sem_ref.at[slot]),
                pltpu.make_async_copy(pool_v_ref.at[page, head], vbuf_ref.at[slot, n], sem_ref.at[slot]))

    def start_gather(seq, head, slot):
        for n in range(n_k):
            tq, rem = divmod(n, slabs)
            j, i = divmod(rem, PAGES_PER_BLOCK)
            blk_idx = idx_ref[((seq * N_HEADS + head) * t + tq) * MOBA_TOPK + j]
            page = pt_ref[seq * n_pages + blk_idx * PAGES_PER_BLOCK + i]
            for cp in slab_copies(page, head, slot, n):
                cp.start()

    n_slots = kbuf_ref.shape[0]
    ahead = n_slots - 1
    assert ahead < N_HEADS

    def ring_step(h):
        unit = step * N_HEADS + h
        slot = lax.rem(unit, n_slots)
        if h == 0:
            @pl.when(step == 0)
            def _():
                for a in range(ahead):
                    start_gather(0, a, a)
        nxt_slot = lax.rem(unit + ahead, n_slots)
        if h + ahead < N_HEADS:
            start_gather(step, h + ahead, nxt_slot)
        else:
            @pl.when(step + 1 < n_steps)
            def _():
                start_gather(step + 1, h + ahead - N_HEADS, nxt_slot)
        for n in range(n_k):
            for cp in slab_copies(0, 0, slot, n):
                cp.wait()

        slope = slopes_ref[h]
        qt = qt_ref[0, h] * SCALE
        knt = knt_ref[0, h]
        vnt = vnt_ref[0, h]
        srow = lax.broadcasted_iota(jnp.int32, (slabs, PAGE_SIZE), 0)
        lane = lax.broadcasted_iota(jnp.int32, (slabs, PAGE_SIZE), 1)
        own_pos = lax.broadcasted_iota(jnp.int32, (1, t), 1)
        out_lane = lax.broadcasted_iota(jnp.int32, (HEAD_DIM, t), 1)
        out = jnp.zeros((HEAD_DIM, t), F32)
        for tq in range(t):
            q_col = qt[:, tq:tq + 1]
            base = ((step * N_HEADS + h) * t + tq) * MOBA_TOPK
            s_rows = []
            blk_of_row = jnp.zeros((slabs, PAGE_SIZE), jnp.int32)
            for j in range(MOBA_TOPK):
                blk_of_row = jnp.where(srow // PAGES_PER_BLOCK == j, idx_ref[base + j], blk_of_row)
                for i in range(PAGES_PER_BLOCK):
                    kt = kbuf_ref[slot, (tq * MOBA_TOPK + j) * PAGES_PER_BLOCK + i]
                    s_rows.append(jnp.sum(kt * q_col, axis=0, keepdims=True))
            s_sel = jnp.concatenate(s_rows, axis=0)
            k_pos = blk_of_row * MOBA_BLOCK + (srow % PAGES_PER_BLOCK) * PAGE_SIZE + lane
            s_sel = s_sel - slope * ((past_len + tq) - k_pos).astype(F32)
            rel_own = tq - own_pos
            s_own = jnp.sum(knt * q_col, axis=0, keepdims=True)
            s_own = jnp.where(rel_own >= 0, s_own - slope * rel_own.astype(F32), NEG_INF)
            m = jnp.maximum(jnp.max(jnp.max(s_sel, axis=1, keepdims=True), axis=0, keepdims=True),
                            jnp.max(s_own, axis=1, keepdims=True))
            p_sel = jnp.exp(s_sel - m)
            p_own = jnp.exp(s_own - m)
            l = (jnp.sum(jnp.sum(p_sel, axis=1, keepdims=True), axis=0, keepdims=True)
                 + jnp.sum(p_own, axis=1, keepdims=True))
            acc = jnp.zeros((HEAD_DIM, PAGE_SIZE), F32)
            for r in range(slabs):
                acc = acc + vbuf_ref[slot, tq * slabs + r] * p_sel[r:r + 1, :]
            o_col = (jnp.sum(acc, axis=1, keepdims=True) + jnp.sum(vnt * p_own, axis=1, keepdims=True)) / l
            out = jnp.where(out_lane == tq, o_col, out)
        o_ref[0, h] = out

    return ring_step


def _split_ada(ada_rows):
    a = ada_rows.reshape(ada_rows.shape[0], 3, 3, 1, D_MODEL)
    return {"sh0": a[:, 0, 0], "sc0": a[:, 0, 1], "g0": a[:, 0, 2],
            "sh1": a[:, 1, 0], "sc1": a[:, 1, 1], "g1": a[:, 1, 2],
            "sh2": a[:, 2, 0], "sc2": a[:, 2, 1], "g2": a[:, 2, 2]}


def _heads(t2, n_seq, s):
    return t2.reshape(n_seq, s, N_HEADS, HEAD_DIM)


def kernel(x_prompt, x_sample, cache_k, cache_v, state_conv, page_table, c_prompt, c_sample,
           w_ada, b_ada, ffn1_wg, ffn1_wu, ffn1_wd, w_in, w_dw, b_dw, gn_g, gn_b,
           beta_attn, beta_conv, w_out, ffn2_wg, ffn2_wu, ffn2_wd, ln_g, ln_b):
    depth = w_ada.shape[0]
    alpha = (2.0 * depth) ** 0.25
    batch, seq, _ = x_prompt.shape
    dec_batch, dec_seq, _ = x_sample.shape
    n_pages = page_table.shape[1]
    past_len = n_pages * PAGE_SIZE
    assert past_len % MOBA_BLOCK == 0 and dec_seq <= MOBA_BLOCK
    alibi = 2.0 ** (-8.0 * np.arange(1, N_HEADS + 1) / N_HEADS)
    slopes = jnp.asarray(alibi, F32)
    slopes_log2 = jnp.asarray(alibi * LOG2E, F32)
    page_table_flat = page_table.reshape(-1)

    y_p, y_s = x_prompt, x_sample
    outs = {name: [] for name in ("kp", "vp", "cp", "ks", "vs", "cs")}
    for l in range(depth):
        ada = _ada(jnp.concatenate([c_prompt, c_sample], axis=0), w_ada[l], b_ada[l])
        ada_p, ada_s = _split_ada(ada[:batch]), _split_ada(ada[batch:])
        lng = [ln_g[l, i].reshape(1, D_MODEL) for i in range(3)]
        lnb = [ln_b[l, i].reshape(1, D_MODEL) for i in range(3)]
        w1 = (ffn1_wg[l].astype(BF16), ffn1_wu[l].astype(BF16), ffn1_wd[l].astype(BF16))
        w2 = (ffn2_wg[l].astype(BF16), ffn2_wu[l].astype(BF16), ffn2_wd[l].astype(BF16))
        win16, wo16 = w_in[l].astype(BF16), w_out[l].astype(BF16)
        beta_a = beta_attn[l].reshape(1, ATTN_WIDTH)
        beta_c = beta_conv[l].reshape(1, CONV_CH)

        tile_p = dict(alpha=alpha, nseq_blk=1, rows_blk=PROMPT_ROWS_PER_STEP)
        tile_s = dict(alpha=alpha, nseq_blk=dec_batch, rows_blk=dec_seq)
        pool_kt = jnp.swapaxes(cache_k[l], -1, -2)
        pool_vt = jnp.swapaxes(cache_v[l], -1, -2)

        x1_s, q, k, v, glu_s = _stage_a_sample(y_s, ada_s, lng[0], lnb[0], *w1, win16, alpha=alpha)
        q4, k4, v4 = (_heads(t2, dec_batch, dec_seq) for t2 in (q, k, v))

        x1, kmean_t = _stage_a_prompt(y_p, ada_p, lng[0], lnb[0], *w1, page_table_flat, pool_kt, dec_batch,
                                      alpha=alpha, rows_blk=PROMPT_ROWS_PER_STEP)
        qt, k16, kmean, ktp, vtp, vt16, glu = _stage_b(x1, ada_p, win16, rows_blk=PROMPT_ROWS_PER_STEP)
        idx = _topk_blocks(q.reshape(dec_batch, dec_seq, ATTN_WIDTH), kmean_t)[..., :MOBA_TOPK]
        attn = _moba_prompt(qt, k16.reshape(batch, seq, ATTN_WIDTH), vt16, kmean, slopes_log2)
        glu3 = glu.reshape(batch, seq, CONV_CH)
        conv = _conv_branch(jnp.zeros((batch, HIST, CONV_CH), F32), glu3, w_dw[l], b_dw[l],
                            gn_g[l], gn_b[l], nb=1, ch=CONV_ROWS_PER_STEP)
        to_t = lambda a4: a4.transpose(0, 2, 3, 1)
        sample = dict(qt=to_t(q4), knt=to_t(k4), vnt=to_t(v4), pool_kt=pool_kt, pool_vt=pool_vt,
                      idx_flat=idx.reshape(-1), page_table_flat=page_table_flat, slopes=slopes,
                      past_len=past_len, n_pages=n_pages)
        y_p, attn_t = _stage_c(x1, attn.reshape(batch * seq, ATTN_WIDTH),
                               conv.reshape(batch * seq, CONV_CH), ada_p, lng[1], lnb[1], lng[2], lnb[2],
                               beta_a, beta_c, wo16, *w2, sample=sample, **tile_p)
        outs["kp"].append(jnp.swapaxes(ktp, -1, -2))
        outs["vp"].append(jnp.swapaxes(vtp, -1, -2))
        outs["cp"].append(glu3[:, seq - HIST:, :])

        x1, glu = x1_s, glu_s
        attn = attn_t.transpose(0, 3, 1, 2).reshape(dec_batch * dec_seq, ATTN_WIDTH)
        glu3 = glu.reshape(dec_batch, dec_seq, CONV_CH)
        hist = state_conv[l]
        conv = _conv_branch(hist, glu3, w_dw[l], b_dw[l], gn_g[l], gn_b[l], nb=dec_batch, ch=dec_seq)
        y_s = _stage_c(x1, attn, conv.reshape(dec_batch * dec_seq, CONV_CH),
                       ada_s, lng[1], lnb[1], lng[2], lnb[2], beta_a, beta_c, wo16, *w2, **tile_s)
        outs["ks"].append(k4.transpose(0, 2, 1, 3))
        outs["vs"].append(v4.transpose(0, 2, 1, 3))
        outs["cs"].append(jnp.concatenate([hist, glu3], axis=1)[:, dec_seq:, :])

    stack = lambda name: jnp.stack(outs[name], 0)
    return (y_p, y_s, stack("kp"), stack("vp"), stack("cp"), stack("ks"), stack("vs"), stack("cs"))
```

```python
import functools

import numpy as np
import jax
import jax.numpy as jnp
from jax import lax
from jax.experimental import pallas as pl
from jax.experimental.pallas import tpu as pltpu

F32 = jnp.float32
BF16 = jnp.bfloat16
HIGHEST = lax.Precision.HIGHEST

D_MODEL = 1024
D_FF = 2816
N_HEADS = 8
HEAD_DIM = 64
ATTN_WIDTH = N_HEADS * HEAD_DIM
CONV_CH = 512
W_IN_COLS = 3 * ATTN_WIDTH + 2 * CONV_CH
CONV_K = 31
HIST = CONV_K - 1
GN_GROUPS = 8
MOBA_BLOCK = 256
MOBA_TOPK = 3
PAGE_SIZE = 128
PAGES_PER_BLOCK = MOBA_BLOCK // PAGE_SIZE
LN_EPS = 1e-5
NEG_INF = -1e30
SCALE = HEAD_DIM ** -0.5

LANES = 128
SUBLANES = 8
HEADS_PER_GROUP = LANES // HEAD_DIM
CONV_PAD_ROWS = 32
LOG2E = 1.4426950408889634
VMEM_LIMIT = 60 * 1024 * 1024

FF_CHUNKS = ((0, 1024), (1024, 2048), (2048, D_FF))
PROMPT_ROWS_PER_STEP = 512
CONV_ROWS_PER_STEP = 256


def _silu(x):
    return x / (1.0 + jnp.exp(-x))


def _sigmoid(x):
    return 1.0 / (1.0 + jnp.exp(-x))


def _layernorm(t, g, b):
    mu = jnp.mean(t, axis=-1, keepdims=True)
    d = t - mu
    var = jnp.mean(d * d, axis=-1, keepdims=True)
    return d * lax.rsqrt(var + LN_EPS) * g + b


def _ffn(h, wg_ref, wu_ref, wd_ref, chunks=FF_CHUNKS, before_up=None, before_down=None):
    acc = None
    for ci, (lo, hi) in enumerate(chunks):
        if before_up is not None:
            before_up(ci)
        g = jnp.dot(h, wg_ref[:, lo:hi], preferred_element_type=F32)
        u = jnp.dot(h, wu_ref[:, lo:hi], preferred_element_type=F32)
        a = (_silu(g) * u).astype(BF16)
        if before_down is not None:
            before_down(ci)
        y = jnp.dot(a, wd_ref[lo:hi, :], preferred_element_type=F32)
        acc = y if acc is None else acc + y
    return acc


def _split_bf16(t):
    hi = t.astype(BF16)
    return hi, (t - hi.astype(F32)).astype(BF16)


def _ada_kernel(c_ref, w_ref, b_ref, o_ref):
    a_hi, a_lo = _split_bf16(_silu(c_ref[...]))
    w_hi, w_lo = _split_bf16(w_ref[...])
    o_ref[...] = (jnp.dot(a_hi, w_hi, preferred_element_type=F32)
                  + jnp.dot(a_lo, w_hi, preferred_element_type=F32)
                  + jnp.dot(a_hi, w_lo, preferred_element_type=F32)) + b_ref[...]


def _ada(c_all, w_ada, b_ada):
    n = c_all.shape[0]
    tn = 1024
    return pl.pallas_call(
        _ada_kernel,
        grid=(w_ada.shape[1] // tn,),
        in_specs=[
            pl.BlockSpec((n, D_MODEL), lambda j: (0, 0)),
            pl.BlockSpec((D_MODEL, tn), lambda j: (0, j)),
            pl.BlockSpec((1, tn), lambda j: (0, j)),
        ],
        out_specs=pl.BlockSpec((n, tn), lambda j: (0, j)),
        out_shape=jax.ShapeDtypeStruct((n, w_ada.shape[1]), F32),
        name="ada",
    )(c_all, w_ada, b_ada.reshape(1, -1))


STREAM_CHUNK_PAGES = 16
STREAM_SLOTS = 3
STREAM_FF_CHUNKS = ((0, 512), (512, 1024), (1024, 1536), (1536, 2048), (2048, 2560), (2560, D_FF))
W_IN_PARTS = ((0, ATTN_WIDTH), (ATTN_WIDTH, 2 * ATTN_WIDTH), (2 * ATTN_WIDTH, 3 * ATTN_WIDTH),
              (3 * ATTN_WIDTH, W_IN_COLS))
STREAM_CHUNKS_FFN = len(STREAM_FF_CHUNKS)
STREAM_PROJECT_PARTS = (0, 2)
STREAM_CHUNKS_PER_SEQ = STREAM_CHUNKS_FFN + len(STREAM_PROJECT_PARTS)


def _block_mean_stream(step, n_steps, first, count, pt_ref, pool_ref, kmt_ref, ring_ref, sem_ref):
    ahead = STREAM_SLOTS - 1
    blocks_per_chunk = STREAM_CHUNK_PAGES // PAGES_PER_BLOCK
    n_cols = kmt_ref.shape[3]
    assert count * blocks_per_chunk == n_cols and ahead <= count

    def page_copy(page, slot, i):
        return pltpu.make_async_copy(pool_ref.at[page], ring_ref.at[slot, i], sem_ref.at[slot])

    def start_chunk(for_step, k, slot):
        g = for_step * STREAM_CHUNKS_PER_SEQ + first + k
        for i in range(STREAM_CHUNK_PAGES):
            page_copy(pt_ref[g * STREAM_CHUNK_PAGES + i], slot, i).start()

    def ring_step(k):
        seq_no = step * count + k
        slot = lax.rem(seq_no, STREAM_SLOTS)
        if k == 0:
            @pl.when(step == 0)
            def _():
                for a in range(ahead):
                    start_chunk(0, a, a)
        nxt_slot = lax.rem(seq_no + ahead, STREAM_SLOTS)
        if k + ahead < count:
            start_chunk(step, k + ahead, nxt_slot)
        else:
            @pl.when(step + 1 < n_steps)
            def _():
                start_chunk(step + 1, k + ahead - count, nxt_slot)
        for i in range(STREAM_CHUNK_PAGES):
            page_copy(0, slot, i).wait()
        col_id = lax.broadcasted_iota(jnp.int32, (HEAD_DIM, n_cols), 1)
        cols = slice(k * blocks_per_chunk, (k + 1) * blocks_per_chunk)
        for h in range(N_HEADS):
            acc = jnp.zeros((HEAD_DIM, n_cols), F32)
            for jb in range(blocks_per_chunk):
                tot = ring_ref[slot, jb * PAGES_PER_BLOCK, h]
                for i in range(1, PAGES_PER_BLOCK):
                    tot = tot + ring_ref[slot, jb * PAGES_PER_BLOCK + i, h]
                mean = jnp.sum(tot, axis=1, keepdims=True) * (1.0 / MOBA_BLOCK)
                acc = jnp.where(col_id == k * blocks_per_chunk + jb, mean, acc)
            kmt_ref[0, h, :, cols] = acc[:, cols]

    return ring_step


def _first_ffn(alpha, x_ref, sh0_ref, sc0_ref, g0_ref, lng_ref, lnb_ref, wg_ref, wu_ref, wd_ref,
               ff_chunks, before_up=None, before_down=None):
    nseq, rows, d = x_ref.shape
    x = x_ref[...]
    h0 = (x * (1.0 + sc0_ref[...]) + sh0_ref[...]).reshape(nseq * rows, d).astype(BF16)
    y = _ffn(h0, wg_ref, wu_ref, wd_ref, ff_chunks, before_up, before_down).reshape(nseq, rows, d)
    return _layernorm(alpha * x + g0_ref[...] * (0.5 * y), lng_ref[...], lnb_ref[...])


def _project(x1, sh1_ref, sc1_ref, win_ref, before_part=None):
    nseq, rows, d = x1.shape
    h1 = (x1 * (1.0 + sc1_ref[...]) + sh1_ref[...]).reshape(nseq * rows, d).astype(BF16)
    parts = []
    for pi, (lo, hi) in enumerate(W_IN_PARTS):
        if before_part is not None:
            before_part(pi)
        parts.append(jnp.dot(h1, win_ref[:, lo:hi], preferred_element_type=F32))
    q, k, v, ug = parts
    return q, k, v, ug[:, 0:CONV_CH] * _sigmoid(ug[:, CONV_CH:])


def _stage_a_sample_kernel(alpha, x_ref, sh0_ref, sc0_ref, g0_ref, sh1_ref, sc1_ref, lng_ref, lnb_ref,
                           wg_ref, wu_ref, wd_ref, win_ref, x1_ref, q_ref, k_ref, v_ref, glu_ref):
    x1 = _first_ffn(alpha, x_ref, sh0_ref, sc0_ref, g0_ref, lng_ref, lnb_ref, wg_ref, wu_ref, wd_ref,
                    FF_CHUNKS)
    x1_ref[...] = x1
    q_ref[...], k_ref[...], v_ref[...], glu_ref[...] = _project(x1, sh1_ref, sc1_ref, win_ref)


def _stage_a_prompt_kernel(alpha, x_ref, sh0_ref, sc0_ref, g0_ref, lng_ref, lnb_ref, wg_ref, wu_ref, wd_ref,
                           pt_ref, pool_ref, x1_ref, kmt_ref, ring_ref, sem_ref):
    step = pl.program_id(0) * pl.num_programs(1) + pl.program_id(1)
    ring_step = _block_mean_stream(step, pl.num_programs(0) * pl.num_programs(1), 0, STREAM_CHUNKS_FFN,
                                   pt_ref, pool_ref, kmt_ref, ring_ref, sem_ref)
    x1_ref[...] = _first_ffn(alpha, x_ref, sh0_ref, sc0_ref, g0_ref, lng_ref, lnb_ref, wg_ref, wu_ref, wd_ref,
                             STREAM_FF_CHUNKS, ring_step)


def _stage_b_kernel(x1_ref, sh1_ref, sc1_ref, win_ref, pt_ref, pool_ref,
                    qt_ref, k16_ref, kmean_ref, ktp_ref, vtp_ref, vt16_ref, glu_ref, kmt_ref,
                    ring_ref, sem_ref):
    step = pl.program_id(0) * pl.num_programs(1) + pl.program_id(1)
    ring_step = _block_mean_stream(step, pl.num_programs(0) * pl.num_programs(1), STREAM_CHUNKS_FFN,
                                   len(STREAM_PROJECT_PARTS), pt_ref, pool_ref, kmt_ref, ring_ref, sem_ref)
    before_part = lambda pi: (ring_step(STREAM_PROJECT_PARTS.index(pi))
                              if pi in STREAM_PROJECT_PARTS else None)
    q, k, v, glu = _project(x1_ref[...], sh1_ref, sc1_ref, win_ref, before_part)
    m = q.shape[0]
    qt_ref[0] = q.T
    k16_ref[...] = k.astype(BF16)
    glu_ref[...] = glu
    blocks_per_tile = m // MOBA_BLOCK
    means = [jnp.sum(k[n * MOBA_BLOCK:(n + 1) * MOBA_BLOCK, :], axis=0, keepdims=True)
             * (1.0 / MOBA_BLOCK) for n in range(blocks_per_tile)]
    tile = pl.program_id(1)
    for jj in range(kmean_ref.shape[1] // blocks_per_tile):
        @pl.when(tile == jj)
        def _():
            for n in range(blocks_per_tile):
                kmean_ref[0, jj * blocks_per_tile + n:jj * blocks_per_tile + n + 1, :] = means[n]
    kt = k.T
    vt = v.T
    for p in range(m // PAGE_SIZE):
        for h in range(N_HEADS):
            rs = slice(h * HEAD_DIM, (h + 1) * HEAD_DIM)
            cs = slice(p * PAGE_SIZE, (p + 1) * PAGE_SIZE)
            ktp_ref[0, p, h] = kt[rs, cs]
            vtp_ref[0, p, h] = vt[rs, cs]
            vt16_ref[0, p, h] = vt[rs, cs].astype(BF16)


def _const_spec(shape):
    return pl.BlockSpec(shape, lambda i, j: (0,) * len(shape), pipeline_mode=pl.Buffered(1))


def _stage_a_sample(x, ada, ln_g, ln_b, wg, wu, wd, win, *, alpha):
    n_seq, s, _ = x.shape
    n_tok = n_seq * s
    x_spec = pl.BlockSpec((n_seq, s, D_MODEL), lambda i, j: (0, 0, 0))
    a_spec = pl.BlockSpec((n_seq, 1, D_MODEL), lambda i, j: (0, 0, 0))
    flat_spec = pl.BlockSpec((n_tok, ATTN_WIDTH), lambda i, j: (0, 0))
    flat_shape = jax.ShapeDtypeStruct((n_tok, ATTN_WIDTH), F32)
    return pl.pallas_call(
        functools.partial(_stage_a_sample_kernel, alpha),
        grid=(1, 1),
        in_specs=[x_spec, a_spec, a_spec, a_spec, a_spec, a_spec,
                  _const_spec((1, D_MODEL)), _const_spec((1, D_MODEL)),
                  _const_spec((D_MODEL, D_FF)), _const_spec((D_MODEL, D_FF)),
                  _const_spec((D_FF, D_MODEL)), _const_spec((D_MODEL, W_IN_COLS))],
        out_specs=[x_spec, flat_spec, flat_spec, flat_spec, flat_spec],
        out_shape=[jax.ShapeDtypeStruct((n_seq, s, D_MODEL), F32)] + [flat_shape] * 4,
        compiler_params=pltpu.CompilerParams(
            dimension_semantics=("arbitrary", "arbitrary"), vmem_limit_bytes=VMEM_LIMIT),
        name="stage_a_sample",
    )(x, ada["sh0"], ada["sc0"], ada["g0"], ada["sh1"], ada["sc1"], ln_g, ln_b, wg, wu, wd, win)


def _stage_a_prompt(x, ada, ln_g, ln_b, wg, wu, wd, page_table_flat, pool_kt, cache_seqs, *, alpha, rows_blk):
    n_seq, s, _ = x.shape
    tiles_per_seq = s // rows_blk
    x_spec = pl.BlockSpec((1, rows_blk, D_MODEL), lambda i, j: (i, j, 0))
    a_spec = pl.BlockSpec((1, 1, D_MODEL), lambda i, j: (i, 0, 0))
    n_steps = n_seq * tiles_per_seq
    pages_per_seq = page_table_flat.shape[0] // cache_seqs
    assert cache_seqs == n_steps and pages_per_seq == STREAM_CHUNKS_PER_SEQ * STREAM_CHUNK_PAGES
    blocks_per_seq = STREAM_CHUNKS_FFN * STREAM_CHUNK_PAGES // PAGES_PER_BLOCK
    return pl.pallas_call(
        functools.partial(_stage_a_prompt_kernel, alpha),
        grid=(n_seq, tiles_per_seq),
        in_specs=[x_spec, a_spec, a_spec, a_spec,
                  _const_spec((1, D_MODEL)), _const_spec((1, D_MODEL)),
                  _const_spec((D_MODEL, D_FF)), _const_spec((D_MODEL, D_FF)), _const_spec((D_FF, D_MODEL)),
                  pl.BlockSpec(memory_space=pltpu.SMEM), pl.BlockSpec(memory_space=pl.ANY)],
        out_specs=[x_spec, pl.BlockSpec((1, N_HEADS, HEAD_DIM, blocks_per_seq),
                                        lambda i, j: (i * tiles_per_seq + j, 0, 0, 0))],
        out_shape=[jax.ShapeDtypeStruct((n_seq, s, D_MODEL), F32),
                   jax.ShapeDtypeStruct((cache_seqs, N_HEADS, HEAD_DIM, blocks_per_seq), F32)],
        scratch_shapes=[pltpu.VMEM((STREAM_SLOTS, STREAM_CHUNK_PAGES, N_HEADS, HEAD_DIM, PAGE_SIZE), F32),
                        pltpu.SemaphoreType.DMA((STREAM_SLOTS,))],
        compiler_params=pltpu.CompilerParams(
            dimension_semantics=("arbitrary", "arbitrary"), vmem_limit_bytes=VMEM_LIMIT),
        name="stage_a_prompt",
    )(x, ada["sh0"], ada["sc0"], ada["g0"], ln_g, ln_b, wg, wu, wd, page_table_flat, pool_kt)


def _stage_b(x1, ada, win, page_table_flat, pool_kt, cache_seqs, *, rows_blk):
    n_seq, s, _ = x1.shape
    tiles_per_seq = s // rows_blk
    assert rows_blk % MOBA_BLOCK == 0 and MOBA_BLOCK % PAGE_SIZE == 0
    assert cache_seqs == n_seq * tiles_per_seq
    cache_blocks = len(STREAM_PROJECT_PARTS) * STREAM_CHUNK_PAGES // PAGES_PER_BLOCK
    n_tok = n_seq * s
    x_spec = pl.BlockSpec((1, rows_blk, D_MODEL), lambda i, j: (i, j, 0))
    a_spec = pl.BlockSpec((1, 1, D_MODEL), lambda i, j: (i, 0, 0))
    flat_spec = pl.BlockSpec((rows_blk, ATTN_WIDTH), lambda i, j: (i * tiles_per_seq + j, 0))
    page_dims = (n_seq, s // PAGE_SIZE, N_HEADS, HEAD_DIM, PAGE_SIZE)
    page_spec = pl.BlockSpec((1, rows_blk // PAGE_SIZE, N_HEADS, HEAD_DIM, PAGE_SIZE),
                             lambda i, j: (i, j, 0, 0, 0))
    return pl.pallas_call(
        _stage_b_kernel,
        grid=(n_seq, tiles_per_seq),
        in_specs=[x_spec, a_spec, a_spec, _const_spec((D_MODEL, W_IN_COLS)),
                  pl.BlockSpec(memory_space=pltpu.SMEM), pl.BlockSpec(memory_space=pl.ANY)],
        out_specs=[pl.BlockSpec((1, ATTN_WIDTH, rows_blk), lambda i, j: (i, 0, j)),
                   flat_spec,
                   pl.BlockSpec((1, s // MOBA_BLOCK, ATTN_WIDTH), lambda i, j: (i, 0, 0)),
                   page_spec, page_spec, page_spec, flat_spec,
                   pl.BlockSpec((1, N_HEADS, HEAD_DIM, cache_blocks),
                                lambda i, j: (i * tiles_per_seq + j, 0, 0, 0))],
        out_shape=[jax.ShapeDtypeStruct((n_seq, ATTN_WIDTH, s), F32),
                   jax.ShapeDtypeStruct((n_tok, ATTN_WIDTH), BF16),
                   jax.ShapeDtypeStruct((n_seq, s // MOBA_BLOCK, ATTN_WIDTH), F32),
                   jax.ShapeDtypeStruct(page_dims, F32), jax.ShapeDtypeStruct(page_dims, F32),
                   jax.ShapeDtypeStruct(page_dims, BF16),
                   jax.ShapeDtypeStruct((n_tok, CONV_CH), F32),
                   jax.ShapeDtypeStruct((cache_seqs, N_HEADS, HEAD_DIM, cache_blocks), F32)],
        scratch_shapes=[pltpu.VMEM((STREAM_SLOTS, STREAM_CHUNK_PAGES, N_HEADS, HEAD_DIM, PAGE_SIZE), F32),
                        pltpu.SemaphoreType.DMA((STREAM_SLOTS,))],
        compiler_params=pltpu.CompilerParams(
            dimension_semantics=("arbitrary", "arbitrary"), vmem_limit_bytes=VMEM_LIMIT),
        name="stage_b",
    )(x1, ada["sh1"], ada["sc1"], win, page_table_flat, pool_kt)


SAMPLE_GATHER_SLOTS = 3


def _stage_c_kernel(alpha, sample_cfg, *refs):
    (x1_ref, attn_ref, conv_ref, g1_ref, sh2_ref, sc2_ref, g2_ref, lng1_ref, lnb1_ref, lng2_ref, lnb2_ref,
     ba_ref, bc_ref, wo_ref, wg_ref, wu_ref, wd_ref) = refs[:17]
    if sample_cfg is not None:
        past_len, n_pages = sample_cfg
        y_ref, so_ref = refs[25:27]
        step = pl.program_id(0) * pl.num_programs(1) + pl.program_id(1)
        ring_step = _sample_attention_stream(step, pl.num_programs(0) * pl.num_programs(1), past_len,
                                             n_pages, *refs[17:25], so_ref, *refs[27:30])
        ff_chunks = STREAM_FF_CHUNKS
        last = len(ff_chunks) - 1
        assert len(ff_chunks) + 2 == N_HEADS
        before_up = lambda ci: ring_step(ci + 1)
        before_down = lambda ci: ring_step(N_HEADS - 1) if ci == last else None
        ring_step(0)
    else:
        y_ref = refs[17]
        before_up, before_down, ff_chunks = None, None, FF_CHUNKS
    nseq, rows, d = x1_ref.shape
    m = nseq * rows
    a = (attn_ref[...] * ba_ref[...]).astype(BF16)
    c = (conv_ref[...] * bc_ref[...]).astype(BF16)
    mix = (jnp.dot(a, wo_ref[0:ATTN_WIDTH, :], preferred_element_type=F32)
           + jnp.dot(c, wo_ref[ATTN_WIDTH:, :], preferred_element_type=F32)).reshape(nseq, rows, d)
    x2 = _layernorm(alpha * x1_ref[...] + g1_ref[...] * mix, lng1_ref[...], lnb1_ref[...])
    h2 = (x2 * (1.0 + sc2_ref[...]) + sh2_ref[...]).reshape(m, d).astype(BF16)
    y = _ffn(h2, wg_ref, wu_ref, wd_ref, ff_chunks, before_up, before_down).reshape(nseq, rows, d)
    y_ref[...] = _layernorm(alpha * x2 + g2_ref[...] * (0.5 * y), lng2_ref[...], lnb2_ref[...])


def _stage_c(x1, attn, conv, ada, ln_g1, ln_b1, ln_g2, ln_b2, beta_a, beta_c, wo, wg, wu, wd,
             *, alpha, nseq_blk, rows_blk, sample=None):
    n_seq, s, _ = x1.shape
    tiles_per_seq = s // rows_blk
    assert nseq_blk == 1 or (nseq_blk == n_seq and tiles_per_seq == 1)
    m_blk = nseq_blk * rows_blk
    x_spec = pl.BlockSpec((nseq_blk, rows_blk, D_MODEL), lambda i, j: (i, j, 0))
    a_spec = pl.BlockSpec((nseq_blk, 1, D_MODEL), lambda i, j: (i, 0, 0))
    flat = pl.BlockSpec((m_blk, ATTN_WIDTH), lambda i, j: (i * tiles_per_seq + j, 0))
    out_specs = [x_spec]
    out_shape = [jax.ShapeDtypeStruct((n_seq, s, D_MODEL), F32)]
    extra_in, extra_in_specs, scratch, cfg = [], [], [], None
    if sample is not None:
        seqs, _, _, t = sample["qt"].shape
        assert seqs == n_seq * tiles_per_seq
        n_k = t * MOBA_TOPK * PAGES_PER_BLOCK
        cfg = (sample["past_len"], sample["n_pages"])
        smem = pl.BlockSpec(memory_space=pltpu.SMEM)
        hbm = pl.BlockSpec(memory_space=pl.ANY)
        seq_spec = pl.BlockSpec((1, N_HEADS, HEAD_DIM, t), lambda i, j: (i * tiles_per_seq + j, 0, 0, 0))
        extra_in = [sample["idx_flat"], sample["page_table_flat"], sample["slopes"],
                    sample["qt"], sample["knt"], sample["vnt"], sample["pool_kt"], sample["pool_vt"]]
        extra_in_specs = [smem, smem, smem, seq_spec, seq_spec, seq_spec, hbm, hbm]
        out_specs.append(seq_spec)
        out_shape.append(jax.ShapeDtypeStruct((seqs, N_HEADS, HEAD_DIM, t), F32))
        scratch = [pltpu.VMEM((SAMPLE_GATHER_SLOTS, n_k, HEAD_DIM, PAGE_SIZE), F32),
                   pltpu.VMEM((SAMPLE_GATHER_SLOTS, n_k, HEAD_DIM, PAGE_SIZE), F32),
                   pltpu.SemaphoreType.DMA((SAMPLE_GATHER_SLOTS,))]
    res = pl.pallas_call(
        functools.partial(_stage_c_kernel, alpha, cfg),
        grid=(n_seq // nseq_blk, tiles_per_seq),
        in_specs=[x_spec, flat, flat, a_spec, a_spec, a_spec, a_spec,
                  _const_spec((1, D_MODEL)), _const_spec((1, D_MODEL)),
                  _const_spec((1, D_MODEL)), _const_spec((1, D_MODEL)),
                  _const_spec((1, ATTN_WIDTH)), _const_spec((1, CONV_CH)),
                  _const_spec((D_MODEL, D_MODEL)),
                  _const_spec((D_MODEL, D_FF)), _const_spec((D_MODEL, D_FF)),
                  _const_spec((D_FF, D_MODEL))] + extra_in_specs,
        out_specs=out_specs,
        out_shape=out_shape,
        scratch_shapes=scratch,
        compiler_params=pltpu.CompilerParams(
            dimension_semantics=("arbitrary", "arbitrary"), vmem_limit_bytes=VMEM_LIMIT),
        name="stage_c",
    )(x1, attn, conv, ada["g1"], ada["sh2"], ada["sc2"], ada["g2"],
      ln_g1, ln_b1, ln_g2, ln_b2, beta_a, beta_c, wo, wg, wu, wd, *extra_in)
    return res if sample is not None else res[0]


def _conv_rows(buf_ref, halo, cur, w_ref, bdw_ref, gng_ref, gnb_ref, gavg_ref):
    nb, ch, c = cur.shape
    pad = CONV_PAD_ROWS
    buf_ref[:, 0:pad - HIST, :] = jnp.zeros((nb, pad - HIST, c), F32)
    buf_ref[:, pad - HIST:pad, :] = halo
    buf_ref[:, pad:pad + ch, :] = cur
    buf_ref[:, pad + ch:, :] = jnp.zeros((nb, SUBLANES, c), F32)
    acc = None
    for r in range(SUBLANES):
        part = None
        for a in range((pad + SUBLANES) // SUBLANES):
            j = SUBLANES * a + r - (pad - HIST)
            if 0 <= j < CONV_K:
                term = buf_ref[:, SUBLANES * a:SUBLANES * a + ch + SUBLANES, :] * w_ref[j:j + 1, :]
                part = term if part is None else part + term
        shifted = part[:, r:r + ch, :]
        acc = shifted if acc is None else acc + shifted
    y = (acc + bdw_ref[...]).reshape(nb * ch, c)
    gavg = gavg_ref[...]

    def group_mean(t):
        hi, lo = _split_bf16(t)
        return (jnp.dot(hi, gavg, preferred_element_type=F32)
                + jnp.dot(lo, gavg, preferred_element_type=F32))

    mu = group_mean(y)
    dlt = y - mu
    var = group_mean(dlt * dlt)
    z = dlt * lax.rsqrt(var + LN_EPS) * gng_ref[...] + gnb_ref[...]
    return _silu(z)


def _conv_kernel(n_chunks, hist_ref, prev_ref, cur_ref, w_ref, bdw_ref, gng_ref, gnb_ref, gavg_ref,
                 o_ref, buf_ref):
    nb, ch, c = cur_ref.shape
    if n_chunks == 1:
        halo = hist_ref[...]
    else:
        halo = jnp.where(pl.program_id(1) == 0, hist_ref[...], prev_ref[:, ch - HIST:, :])
    out = _conv_rows(buf_ref, halo, cur_ref[...], w_ref, bdw_ref, gng_ref, gnb_ref, gavg_ref)
    o_ref[...] = out.reshape(nb, ch, c)


def _group_average_matrix(c):
    grp = np.arange(c) // (c // GN_GROUPS)
    return jnp.asarray((grp[:, None] == grp[None, :]).astype(np.float32) / (c // GN_GROUPS), BF16)


def _conv_branch(hist, glu3, w_dw, b_dw, gn_g, gn_b, *, nb, ch):
    n_seq, s, c = glu3.shape
    n_chunks = s // ch
    assert ch >= HIST or n_chunks == 1
    gavg = _group_average_matrix(c)
    cur_spec = pl.BlockSpec((nb, ch, c), lambda i, j: (i, j, 0))
    prev_spec = pl.BlockSpec((nb, ch, c), lambda i, j: (i, jnp.maximum(j - 1, 0), 0))
    return pl.pallas_call(
        functools.partial(_conv_kernel, n_chunks),
        grid=(n_seq // nb, n_chunks),
        in_specs=[pl.BlockSpec((nb, HIST, c), lambda i, j: (i, 0, 0)), prev_spec, cur_spec,
                  pl.BlockSpec((CONV_K, c), lambda i, j: (0, 0)),
                  pl.BlockSpec((1, c), lambda i, j: (0, 0)),
                  pl.BlockSpec((1, c), lambda i, j: (0, 0)),
                  pl.BlockSpec((1, c), lambda i, j: (0, 0)),
                  pl.BlockSpec((c, c), lambda i, j: (0, 0))],
        out_specs=cur_spec,
        out_shape=jax.ShapeDtypeStruct((n_seq, s, c), F32),
        scratch_shapes=[pltpu.VMEM((nb, CONV_PAD_ROWS + ch + SUBLANES, c), F32)],
        compiler_params=pltpu.CompilerParams(dimension_semantics=("arbitrary", "arbitrary")),
        name="conv_branch",
    )(hist, glu3, glu3, w_dw, b_dw.reshape(1, c), gn_g.reshape(1, c), gn_b.reshape(1, c), gavg)


MOBA_HEADS_PER_STEP = 8


def _moba_prompt_kernel(slopes_ref, qt_ref, k16_ref, vt16_ref, kmean_ref, o_ref,
                        q16_ref, bias_ref, krs_ref, s_ref, m_ref, l_ref, acc_ref):
    grp0 = pl.program_id(1)
    qb = pl.program_id(2)
    blk = MOBA_BLOCK
    nb = kmean_ref.shape[1]
    hps = MOBA_HEADS_PER_STEP
    row_head = lax.broadcasted_iota(jnp.int32, (LANES, blk), 0) // HEAD_DIM
    key_i = lax.broadcasted_iota(jnp.int32, (blk, blk), 0)
    qry_i = lax.broadcasted_iota(jnp.int32, (blk, blk), 1)
    key_f = key_i.astype(F32)
    blk_id = lax.broadcasted_iota(jnp.int32, (nb, blk), 0)
    past = blk_id < qb
    zero_row = jnp.zeros((1, blk), jnp.int32)

    def lanes_of(hh):
        g = hh // HEADS_PER_GROUP
        return slice(g * LANES, (g + 1) * LANES)

    def scores(hh, n):
        kb = k16_ref[0, pl.ds(pl.multiple_of(n * blk, blk), blk), lanes_of(hh)]
        return jnp.dot(kb, q16_ref[hh], preferred_element_type=F32) + krs_ref[hh]

    def pv(hh, n, p):
        p16 = p.astype(BF16)
        acc = None
        for i in range(PAGES_PER_BLOCK):
            vt = vt16_ref[0, n * PAGES_PER_BLOCK + i, hh]
            part = jnp.dot(vt, p16[i * PAGE_SIZE:(i + 1) * PAGE_SIZE, :], preferred_element_type=F32)
            acc = part if acc is None else acc + part
        return acc

    @pl.when(qb == 0)
    def _():
        for hh in range(hps):
            krs_ref[hh] = key_f * slopes_ref[grp0 * hps + hh]

    for hh in range(hps):
        sub = hh % HEADS_PER_GROUP
        qh = jnp.where(row_head == sub, qt_ref[0, lanes_of(hh), :], 0.0)
        q16_ref[hh] = (qh * (SCALE * LOG2E)).astype(BF16)

        gate = jnp.dot(kmean_ref[0, :, lanes_of(hh)], qh, preferred_element_type=F32, precision=HIGHEST)
        cnt = jnp.zeros((nb, blk), F32)
        for mth in range(nb):
            other = gate[mth:mth + 1, :]
            ahead = (other > gate) | ((other == gate) & (blk_id > mth))
            cnt = cnt + jnp.where(ahead & (mth < qb), 1.0, 0.0)
        bias_ref[hh] = jnp.where((cnt < MOBA_TOPK) & past, 0.0, NEG_INF)

        s = jnp.where(key_i <= qry_i, scores(hh, qb), NEG_INF)
        s_ref[hh, qb] = s
        m_ref[hh] = jnp.max(s, axis=0, keepdims=True)

    def block_const(hh, n):
        rel = ((n - qb) * blk + zero_row).astype(F32)
        return slopes_ref[grp0 * hps + hh] * rel + bias_ref[hh, pl.ds(n, 1), :]

    def pass1(n, carry):
        for hh in range(hps):
            s = scores(hh, n)
            s_ref[hh, n] = s
            m_ref[hh] = jnp.maximum(m_ref[hh], jnp.max(s, axis=0, keepdims=True) + block_const(hh, n))
        return carry

    lax.fori_loop(0, qb, pass1, 0)

    def block_out(hh, n, c):
        p = jnp.exp2(s_ref[hh, n] - (m_ref[hh] - c))
        return jnp.sum(p, axis=0, keepdims=True), pv(hh, n, p)

    for hh in range(hps):
        l_ref[hh], acc_ref[hh] = block_out(hh, qb, 0.0)

    def pass2(n, carry):
        for hh in range(hps):
            l_part, acc_part = block_out(hh, n, block_const(hh, n))
            l_ref[hh] = l_ref[hh] + l_part
            acc_ref[hh] = acc_ref[hh] + acc_part
        return carry

    lax.fori_loop(0, qb, pass2, 0)
    out_t = jnp.concatenate([acc_ref[hh] / l_ref[hh] for hh in range(hps)], axis=0)
    o_ref[0] = out_t.T


def _moba_prompt(qt3, k16, vt16, kmean, slopes):
    b, s, _ = k16.shape
    hps = MOBA_HEADS_PER_STEP
    width = hps * HEAD_DIM
    n_blocks = s // MOBA_BLOCK
    return pl.pallas_call(
        _moba_prompt_kernel,
        grid=(b, N_HEADS // hps, n_blocks),
        in_specs=[pl.BlockSpec(memory_space=pltpu.SMEM),
                  pl.BlockSpec((1, width, MOBA_BLOCK), lambda i, g, j: (i, g, j)),
                  pl.BlockSpec((1, s, width), lambda i, g, j: (i, 0, g)),
                  pl.BlockSpec((1, s // PAGE_SIZE, hps, HEAD_DIM, PAGE_SIZE),
                               lambda i, g, j: (i, 0, g, 0, 0)),
                  pl.BlockSpec((1, n_blocks, width), lambda i, g, j: (i, 0, g))],
        out_specs=pl.BlockSpec((1, MOBA_BLOCK, width), lambda i, g, j: (i, j, g)),
        out_shape=jax.ShapeDtypeStruct((b, s, ATTN_WIDTH), F32),
        scratch_shapes=[pltpu.VMEM((hps, LANES, MOBA_BLOCK), BF16),
                        pltpu.VMEM((hps, n_blocks, MOBA_BLOCK), F32),
                        pltpu.VMEM((hps, MOBA_BLOCK, MOBA_BLOCK), F32),
                        pltpu.VMEM((hps, n_blocks, MOBA_BLOCK, MOBA_BLOCK), F32),
                        pltpu.VMEM((hps, 1, MOBA_BLOCK), F32),
                        pltpu.VMEM((hps, 1, MOBA_BLOCK), F32),
                        pltpu.VMEM((hps, HEAD_DIM, MOBA_BLOCK), F32)],
        compiler_params=pltpu.CompilerParams(
            dimension_semantics=("arbitrary", "arbitrary", "arbitrary"), vmem_limit_bytes=VMEM_LIMIT),
        name="moba_prompt",
    )(slopes, qt3, k16, vt16, kmean)


TOPK_SEQS_PER_STEP = 4


def _topk_kernel(q_ref, kmt_ref, o_ref):
    ns, t, nbp = q_ref.shape[0], q_ref.shape[1], kmt_ref.shape[3]
    rows = ns * N_HEADS * t
    gates = []
    for s in range(ns):
        q = q_ref[s]
        for h in range(N_HEADS):
            gates.append(jnp.dot(q[:, h * HEAD_DIM:(h + 1) * HEAD_DIM], kmt_ref[s, h],
                                 preferred_element_type=F32, precision=HIGHEST))
    g = jnp.concatenate(gates, axis=0)
    lane = lax.broadcasted_iota(jnp.int32, (rows, nbp), 1).astype(F32)
    out_lane = lax.broadcasted_iota(jnp.int32, (rows, LANES), 1)
    res = jnp.zeros((rows, LANES), F32)
    for r in range(MOBA_TOPK):
        mx = jnp.max(g, axis=1, keepdims=True)
        idx = jnp.min(jnp.where(g == mx, lane, float(nbp)), axis=1, keepdims=True)
        res = jnp.where(out_lane == r, idx, res)
        g = jnp.where(lane == idx, -jnp.inf, g)
    o_ref[...] = res.astype(jnp.int32).reshape(ns, N_HEADS, t, LANES)


def _topk_blocks(q3, kmean_t):
    n_seq, t, _ = q3.shape
    nbp = kmean_t.shape[3]
    ns = TOPK_SEQS_PER_STEP
    return pl.pallas_call(
        _topk_kernel,
        grid=(n_seq // ns,),
        in_specs=[pl.BlockSpec((ns, t, ATTN_WIDTH), lambda b: (b, 0, 0)),
                  pl.BlockSpec((ns, N_HEADS, HEAD_DIM, nbp), lambda b: (b, 0, 0, 0))],
        out_specs=pl.BlockSpec((ns, N_HEADS, t, LANES), lambda b: (b, 0, 0, 0)),
        out_shape=jax.ShapeDtypeStruct((n_seq, N_HEADS, t, LANES), jnp.int32),
        name="topk_blocks",
    )(q3, kmean_t)


def _sample_attention_stream(step, n_steps, past_len, n_pages, idx_ref, pt_ref, slopes_ref,
                             qt_ref, knt_ref, vnt_ref, pool_k_ref, pool_v_ref, o_ref,
                             kbuf_ref, vbuf_ref, sem_ref):
    t = qt_ref.shape[3]
    slabs = MOBA_TOPK * PAGES_PER_BLOCK
    n_k = t * slabs

    def slab_copies(page, head, slot, n):
        return (pltpu.make_async_copy(pool_k_ref.at[page, head], kbuf_ref.at[slot, n], sem_ref.at[slot]),
                pltpu.make_async_copy(pool_v_ref.at[page, head], vbuf_ref.at[slot, n], sem_ref.at[slot]))

    def start_gather(seq, head, slot):
        for n in range(n_k):
            tq, rem = divmod(n, slabs)
            j, i = divmod(rem, PAGES_PER_BLOCK)
            blk_idx = idx_ref[((seq * N_HEADS + head) * t + tq) * MOBA_TOPK + j]
            page = pt_ref[seq * n_pages + blk_idx * PAGES_PER_BLOCK + i]
            for cp in slab_copies(page, head, slot, n):
                cp.start()

    n_slots = kbuf_ref.shape[0]
    ahead = n_slots - 1
    assert ahead < N_HEADS

    def ring_step(h):
        unit = step * N_HEADS + h
        slot = lax.rem(unit, n_slots)
        if h == 0:
            @pl.when(step == 0)
            def _():
                for a in range(ahead):
                    start_gather(0, a, a)
        nxt_slot = lax.rem(unit + ahead, n_slots)
        if h + ahead < N_HEADS:
            start_gather(step, h + ahead, nxt_slot)
        else:
            @pl.when(step + 1 < n_steps)
            def _():
                start_gather(step + 1, h + ahead - N_HEADS, nxt_slot)
        for n in range(n_k):
            for cp in slab_copies(0, 0, slot, n):
                cp.wait()

        slope = slopes_ref[h]
        qt = qt_ref[0, h] * SCALE
        knt = knt_ref[0, h]
        vnt = vnt_ref[0, h]
        srow = lax.broadcasted_iota(jnp.int32, (slabs, PAGE_SIZE), 0)
        lane = lax.broadcasted_iota(jnp.int32, (slabs, PAGE_SIZE), 1)
        own_pos = lax.broadcasted_iota(jnp.int32, (1, t), 1)
        out_lane = lax.broadcasted_iota(jnp.int32, (HEAD_DIM, t), 1)
        out = jnp.zeros((HEAD_DIM, t), F32)
        for tq in range(t):
            q_col = qt[:, tq:tq + 1]
            base = ((step * N_HEADS + h) * t + tq) * MOBA_TOPK
            s_rows = []
            blk_of_row = jnp.zeros((slabs, PAGE_SIZE), jnp.int32)
            for j in range(MOBA_TOPK):
                blk_of_row = jnp.where(srow // PAGES_PER_BLOCK == j, idx_ref[base + j], blk_of_row)
                for i in range(PAGES_PER_BLOCK):
                    kt = kbuf_ref[slot, (tq * MOBA_TOPK + j) * PAGES_PER_BLOCK + i]
                    s_rows.append(jnp.sum(kt * q_col, axis=0, keepdims=True))
            s_sel = jnp.concatenate(s_rows, axis=0)
            k_pos = blk_of_row * MOBA_BLOCK + (srow % PAGES_PER_BLOCK) * PAGE_SIZE + lane
            s_sel = s_sel - slope * ((past_len + tq) - k_pos).astype(F32)
            rel_own = tq - own_pos
            s_own = jnp.sum(knt * q_col, axis=0, keepdims=True)
            s_own = jnp.where(rel_own >= 0, s_own - slope * rel_own.astype(F32), NEG_INF)
            m = jnp.maximum(jnp.max(jnp.max(s_sel, axis=1, keepdims=True), axis=0, keepdims=True),
                            jnp.max(s_own, axis=1, keepdims=True))
            p_sel = jnp.exp(s_sel - m)
            p_own = jnp.exp(s_own - m)
            l = (jnp.sum(jnp.sum(p_sel, axis=1, keepdims=True), axis=0, keepdims=True)
                 + jnp.sum(p_own, axis=1, keepdims=True))
            acc = jnp.zeros((HEAD_DIM, PAGE_SIZE), F32)
            for r in range(slabs):
                acc = acc + vbuf_ref[slot, tq * slabs + r] * p_sel[r:r + 1, :]
            o_col = (jnp.sum(acc, axis=1, keepdims=True) + jnp.sum(vnt * p_own, axis=1, keepdims=True)) / l
            out = jnp.where(out_lane == tq, o_col, out)
        o_ref[0, h] = out

    return ring_step


def _split_ada(ada_rows):
    a = ada_rows.reshape(ada_rows.shape[0], 3, 3, 1, D_MODEL)
    return {"sh0": a[:, 0, 0], "sc0": a[:, 0, 1], "g0": a[:, 0, 2],
            "sh1": a[:, 1, 0], "sc1": a[:, 1, 1], "g1": a[:, 1, 2],
            "sh2": a[:, 2, 0], "sc2": a[:, 2, 1], "g2": a[:, 2, 2]}


def _heads(t2, n_seq, s):
    return t2.reshape(n_seq, s, N_HEADS, HEAD_DIM)


def kernel(x_prompt, x_sample, cache_k, cache_v, state_conv, page_table, c_prompt, c_sample,
           w_ada, b_ada, ffn1_wg, ffn1_wu, ffn1_wd, w_in, w_dw, b_dw, gn_g, gn_b,
           beta_attn, beta_conv, w_out, ffn2_wg, ffn2_wu, ffn2_wd, ln_g, ln_b):
    depth = w_ada.shape[0]
    alpha = (2.0 * depth) ** 0.25
    batch, seq, _ = x_prompt.shape
    dec_batch, dec_seq, _ = x_sample.shape
    n_pages = page_table.shape[1]
    past_len = n_pages * PAGE_SIZE
    assert past_len % MOBA_BLOCK == 0 and dec_seq <= MOBA_BLOCK
    alibi = 2.0 ** (-8.0 * np.arange(1, N_HEADS + 1) / N_HEADS)
    slopes = jnp.asarray(alibi, F32)
    slopes_log2 = jnp.asarray(alibi * LOG2E, F32)
    page_table_flat = page_table.reshape(-1)

    y_p, y_s = x_prompt, x_sample
    outs = {name: [] for name in ("kp", "vp", "cp", "ks", "vs", "cs")}
    for l in range(depth):
        ada = _ada(jnp.concatenate([c_prompt, c_sample], axis=0), w_ada[l], b_ada[l])
        ada_p, ada_s = _split_ada(ada[:batch]), _split_ada(ada[batch:])
        lng = [ln_g[l, i].reshape(1, D_MODEL) for i in range(3)]
        lnb = [ln_b[l, i].reshape(1, D_MODEL) for i in range(3)]
        w1 = (ffn1_wg[l].astype(BF16), ffn1_wu[l].astype(BF16), ffn1_wd[l].astype(BF16))
        w2 = (ffn2_wg[l].astype(BF16), ffn2_wu[l].astype(BF16), ffn2_wd[l].astype(BF16))
        win16, wo16 = w_in[l].astype(BF16), w_out[l].astype(BF16)
        beta_a = beta_attn[l].reshape(1, ATTN_WIDTH)
        beta_c = beta_conv[l].reshape(1, CONV_CH)

        tile_p = dict(alpha=alpha, nseq_blk=1, rows_blk=PROMPT_ROWS_PER_STEP)
        tile_s = dict(alpha=alpha, nseq_blk=dec_batch, rows_blk=dec_seq)
        pool_kt = jnp.swapaxes(cache_k[l], -1, -2)
        pool_vt = jnp.swapaxes(cache_v[l], -1, -2)

        x1_s, q, k, v, glu_s = _stage_a_sample(y_s, ada_s, lng[0], lnb[0], *w1, win16, alpha=alpha)
        q4, k4, v4 = (_heads(t2, dec_batch, dec_seq) for t2 in (q, k, v))

        x1, kmt_a = _stage_a_prompt(y_p, ada_p, lng[0], lnb[0], *w1, page_table_flat, pool_kt, dec_batch,
                                    alpha=alpha, rows_blk=PROMPT_ROWS_PER_STEP)
        qt, k16, kmean, ktp, vtp, vt16, glu, kmt_b = _stage_b(x1, ada_p, win16, page_table_flat, pool_kt,
                                                              dec_batch, rows_blk=PROMPT_ROWS_PER_STEP)
        kmean_t = jnp.concatenate([kmt_a, kmt_b], axis=-1)
        idx = _topk_blocks(q.reshape(dec_batch, dec_seq, ATTN_WIDTH), kmean_t)[..., :MOBA_TOPK]
        attn = _moba_prompt(qt, k16.reshape(batch, seq, ATTN_WIDTH), vt16, kmean, slopes_log2)
        glu3 = glu.reshape(batch, seq, CONV_CH)
        conv = _conv_branch(jnp.zeros((batch, HIST, CONV_CH), F32), glu3, w_dw[l], b_dw[l],
                            gn_g[l], gn_b[l], nb=1, ch=CONV_ROWS_PER_STEP)
        to_t = lambda a4: a4.transpose(0, 2, 3, 1)
        sample = dict(qt=to_t(q4), knt=to_t(k4), vnt=to_t(v4), pool_kt=pool_kt, pool_vt=pool_vt,
                      idx_flat=idx.reshape(-1), page_table_flat=page_table_flat, slopes=slopes,
                      past_len=past_len, n_pages=n_pages)
        y_p, attn_t = _stage_c(x1, attn.reshape(batch * seq, ATTN_WIDTH),
                               conv.reshape(batch * seq, CONV_CH), ada_p, lng[1], lnb[1], lng[2], lnb[2],
                               beta_a, beta_c, wo16, *w2, sample=sample, **tile_p)
        outs["kp"].append(jnp.swapaxes(ktp, -1, -2))
        outs["vp"].append(jnp.swapaxes(vtp, -1, -2))
        outs["cp"].append(glu3[:, seq - HIST:, :])

        x1, glu = x1_s, glu_s
        attn = attn_t.transpose(0, 3, 1, 2).reshape(dec_batch * dec_seq, ATTN_WIDTH)
        glu3 = glu.reshape(dec_batch, dec_seq, CONV_CH)
        hist = state_conv[l]
        conv = _conv_branch(hist, glu3, w_dw[l], b_dw[l], gn_g[l], gn_b[l], nb=dec_batch, ch=dec_seq)
        y_s = _stage_c(x1, attn, conv.reshape(dec_batch * dec_seq, CONV_CH),
                       ada_s, lng[1], lnb[1], lng[2], lnb[2], beta_a, beta_c, wo16, *w2, **tile_s)
        outs["ks"].append(k4.transpose(0, 2, 1, 3))
        outs["vs"].append(v4.transpose(0, 2, 1, 3))
        outs["cs"].append(jnp.concatenate([hist, glu3], axis=1)[:, dec_seq:, :])

    stack = lambda name: jnp.stack(outs[name], 0)
    return (y_p, y_s, stack("kp"), stack("vp"), stack("cp"), stack("ks"), stack("vs"), stack("cs"))
```

```python
import functools

import numpy as np
import jax
import jax.numpy as jnp
from jax import lax
from jax.experimental import pallas as pl
from jax.experimental.pallas import tpu as pltpu

F32 = jnp.float32
BF16 = jnp.bfloat16
HIGHEST = lax.Precision.HIGHEST

D_MODEL = 1024
D_FF = 2816
N_HEADS = 8
HEAD_DIM = 64
ATTN_WIDTH = N_HEADS * HEAD_DIM
CONV_CH = 512
W_IN_COLS = 3 * ATTN_WIDTH + 2 * CONV_CH
CONV_K = 31
HIST = CONV_K - 1
GN_GROUPS = 8
MOBA_BLOCK = 256
MOBA_TOPK = 3
PAGE_SIZE = 128
PAGES_PER_BLOCK = MOBA_BLOCK // PAGE_SIZE
LN_EPS = 1e-5
NEG_INF = -1e30
SCALE = HEAD_DIM ** -0.5

LANES = 128
SUBLANES = 8
HEADS_PER_GROUP = LANES // HEAD_DIM
CONV_PAD_ROWS = 32
LOG2E = 1.4426950408889634
VMEM_LIMIT = 60 * 1024 * 1024

FF_CHUNKS = ((0, 1024), (1024, 2048), (2048, D_FF))
PROMPT_ROWS_PER_STEP = 512
CONV_ROWS_PER_STEP = 256


def _silu(x):
    return x / (1.0 + jnp.exp(-x))


def _sigmoid(x):
    return 1.0 / (1.0 + jnp.exp(-x))


def _layernorm(t, g, b):
    mu = jnp.mean(t, axis=-1, keepdims=True)
    d = t - mu
    var = jnp.mean(d * d, axis=-1, keepdims=True)
    return d * lax.rsqrt(var + LN_EPS) * g + b


def _ffn(h, wg_ref, wu_ref, wd_ref, chunks=FF_CHUNKS, before_up=None, before_down=None):
    acc = None
    for ci, (lo, hi) in enumerate(chunks):
        if before_up is not None:
            before_up(ci)
        g = jnp.dot(h, wg_ref[:, lo:hi], preferred_element_type=F32)
        u = jnp.dot(h, wu_ref[:, lo:hi], preferred_element_type=F32)
        a = (_silu(g) * u).astype(BF16)
        if before_down is not None:
            before_down(ci)
        y = jnp.dot(a, wd_ref[lo:hi, :], preferred_element_type=F32)
        acc = y if acc is None else acc + y
    return acc


def _split_bf16(t):
    hi = t.astype(BF16)
    return hi, (t - hi.astype(F32)).astype(BF16)


def _ada_kernel(c_ref, w_ref, b_ref, o_ref):
    a_hi, a_lo = _split_bf16(_silu(c_ref[...]))
    w_hi, w_lo = _split_bf16(w_ref[...])
    o_ref[...] = (jnp.dot(a_hi, w_hi, preferred_element_type=F32)
                  + jnp.dot(a_lo, w_hi, preferred_element_type=F32)
                  + jnp.dot(a_hi, w_lo, preferred_element_type=F32)) + b_ref[...]


def _ada(c_all, w_ada, b_ada):
    n = c_all.shape[0]
    tn = 1024
    return pl.pallas_call(
        _ada_kernel,
        grid=(w_ada.shape[1] // tn,),
        in_specs=[
            pl.BlockSpec((n, D_MODEL), lambda j: (0, 0)),
            pl.BlockSpec((D_MODEL, tn), lambda j: (0, j)),
            pl.BlockSpec((1, tn), lambda j: (0, j)),
        ],
        out_specs=pl.BlockSpec((n, tn), lambda j: (0, j)),
        out_shape=jax.ShapeDtypeStruct((n, w_ada.shape[1]), F32),
        name="ada",
    )(c_all, w_ada, b_ada.reshape(1, -1))


STREAM_CHUNK_PAGES = 16
STREAM_SLOTS = 3
STREAM_FF_CHUNKS = ((0, 512), (512, 1024), (1024, 1536), (1536, 2048), (2048, 2560), (2560, D_FF))
W_IN_PARTS = ((0, ATTN_WIDTH), (ATTN_WIDTH, 2 * ATTN_WIDTH), (2 * ATTN_WIDTH, 3 * ATTN_WIDTH),
              (3 * ATTN_WIDTH, W_IN_COLS))
STREAM_HOOKS = (("up", 0), ("up", 1), ("down", 1), ("up", 2), ("up", 3), ("up", 4), ("down", 4), ("up", 5))
STREAM_CHUNKS_PER_SEQ = len(STREAM_HOOKS)


def _block_mean_stream(step, n_steps, first, count, pt_ref, pool_ref, kmt_ref, ring_ref, sem_ref):
    ahead = STREAM_SLOTS - 1
    blocks_per_chunk = STREAM_CHUNK_PAGES // PAGES_PER_BLOCK
    n_cols = kmt_ref.shape[3]
    assert count * blocks_per_chunk == n_cols and ahead <= count

    def page_copy(page, slot, i):
        return pltpu.make_async_copy(pool_ref.at[page], ring_ref.at[slot, i], sem_ref.at[slot])

    def start_chunk(for_step, k, slot):
        g = for_step * STREAM_CHUNKS_PER_SEQ + first + k
        for i in range(STREAM_CHUNK_PAGES):
            page_copy(pt_ref[g * STREAM_CHUNK_PAGES + i], slot, i).start()

    def ring_step(k):
        seq_no = step * count + k
        slot = lax.rem(seq_no, STREAM_SLOTS)
        if k == 0:
            @pl.when(step == 0)
            def _():
                for a in range(ahead):
                    start_chunk(0, a, a)
        nxt_slot = lax.rem(seq_no + ahead, STREAM_SLOTS)
        if k + ahead < count:
            start_chunk(step, k + ahead, nxt_slot)
        else:
            @pl.when(step + 1 < n_steps)
            def _():
                start_chunk(step + 1, k + ahead - count, nxt_slot)
        for i in range(STREAM_CHUNK_PAGES):
            page_copy(0, slot, i).wait()
        col_id = lax.broadcasted_iota(jnp.int32, (HEAD_DIM, n_cols), 1)
        cols = slice(k * blocks_per_chunk, (k + 1) * blocks_per_chunk)
        for h in range(N_HEADS):
            acc = jnp.zeros((HEAD_DIM, n_cols), F32)
            for jb in range(blocks_per_chunk):
                tot = ring_ref[slot, jb * PAGES_PER_BLOCK, h]
                for i in range(1, PAGES_PER_BLOCK):
                    tot = tot + ring_ref[slot, jb * PAGES_PER_BLOCK + i, h]
                mean = jnp.sum(tot, axis=1, keepdims=True) * (1.0 / MOBA_BLOCK)
                acc = jnp.where(col_id == k * blocks_per_chunk + jb, mean, acc)
            kmt_ref[0, h, :, cols] = acc[:, cols]

    return ring_step


def _first_ffn(alpha, x_ref, sh0_ref, sc0_ref, g0_ref, lng_ref, lnb_ref, wg_ref, wu_ref, wd_ref,
               ff_chunks, before_up=None, before_down=None):
    nseq, rows, d = x_ref.shape
    x = x_ref[...]
    h0 = (x * (1.0 + sc0_ref[...]) + sh0_ref[...]).reshape(nseq * rows, d).astype(BF16)
    y = _ffn(h0, wg_ref, wu_ref, wd_ref, ff_chunks, before_up, before_down).reshape(nseq, rows, d)
    return _layernorm(alpha * x + g0_ref[...] * (0.5 * y), lng_ref[...], lnb_ref[...])


def _project(x1, sh1_ref, sc1_ref, win_ref, before_part=None):
    nseq, rows, d = x1.shape
    h1 = (x1 * (1.0 + sc1_ref[...]) + sh1_ref[...]).reshape(nseq * rows, d).astype(BF16)
    parts = []
    for pi, (lo, hi) in enumerate(W_IN_PARTS):
        if before_part is not None:
            before_part(pi)
        parts.append(jnp.dot(h1, win_ref[:, lo:hi], preferred_element_type=F32))
    q, k, v, ug = parts
    return q, k, v, ug[:, 0:CONV_CH] * _sigmoid(ug[:, CONV_CH:])


def _stage_a_sample_kernel(alpha, x_ref, sh0_ref, sc0_ref, g0_ref, sh1_ref, sc1_ref, lng_ref, lnb_ref,
                           wg_ref, wu_ref, wd_ref, win_ref, x1_ref, q_ref, k_ref, v_ref, glu_ref):
    x1 = _first_ffn(alpha, x_ref, sh0_ref, sc0_ref, g0_ref, lng_ref, lnb_ref, wg_ref, wu_ref, wd_ref,
                    FF_CHUNKS)
    x1_ref[...] = x1
    q_ref[...], k_ref[...], v_ref[...], glu_ref[...] = _project(x1, sh1_ref, sc1_ref, win_ref)


def _stage_a_prompt_kernel(alpha, x_ref, sh0_ref, sc0_ref, g0_ref, lng_ref, lnb_ref, wg_ref, wu_ref, wd_ref,
                           pt_ref, pool_ref, x1_ref, kmt_ref, ring_ref, sem_ref):
    step = pl.program_id(0) * pl.num_programs(1) + pl.program_id(1)
    ring_step = _block_mean_stream(step, pl.num_programs(0) * pl.num_programs(1), 0, STREAM_CHUNKS_PER_SEQ,
                                   pt_ref, pool_ref, kmt_ref, ring_ref, sem_ref)

    def hook(kind):
        return lambda ci: ring_step(STREAM_HOOKS.index((kind, ci))) if (kind, ci) in STREAM_HOOKS else None

    x1_ref[...] = _first_ffn(alpha, x_ref, sh0_ref, sc0_ref, g0_ref, lng_ref, lnb_ref, wg_ref, wu_ref, wd_ref,
                             STREAM_FF_CHUNKS, hook("up"), hook("down"))


def _stage_b_kernel(x1_ref, sh1_ref, sc1_ref, win_ref, w_ref, bdw_ref, gng_ref, gnb_ref, gavg_ref,
                    qt_ref, k16_ref, kmean_ref, ktp_ref, vtp_ref, vt16_ref, conv_ref, tail_ref,
                    halo_ref, cbuf_ref):
    q, k, v, glu = _project(x1_ref[...], sh1_ref, sc1_ref, win_ref)
    m = q.shape[0]
    qt_ref[0] = q.T
    k16_ref[...] = k.astype(BF16)
    first_halo = jnp.where(pl.program_id(1) == 0, 0.0, halo_ref[:, 0:HIST, :])
    for lo in range(0, m, CONV_ROWS_PER_STEP):
        halo = first_halo if lo == 0 else glu[lo - HIST:lo, :].reshape(1, HIST, CONV_CH)
        cur = glu[lo:lo + CONV_ROWS_PER_STEP, :].reshape(1, CONV_ROWS_PER_STEP, CONV_CH)
        conv_ref[lo:lo + CONV_ROWS_PER_STEP, :] = _conv_rows(cbuf_ref, halo, cur, w_ref, bdw_ref,
                                                             gng_ref, gnb_ref, gavg_ref)
    halo_ref[:, 0:HIST, :] = glu[m - HIST:m, :].reshape(1, HIST, CONV_CH)
    tail_ref[...] = glu[m - CONV_PAD_ROWS:m, :].reshape(1, CONV_PAD_ROWS, CONV_CH)
    blocks_per_tile = m // MOBA_BLOCK
    means = [jnp.sum(k[n * MOBA_BLOCK:(n + 1) * MOBA_BLOCK, :], axis=0, keepdims=True)
             * (1.0 / MOBA_BLOCK) for n in range(blocks_per_tile)]
    tile = pl.program_id(1)
    for jj in range(kmean_ref.shape[1] // blocks_per_tile):
        @pl.when(tile == jj)
        def _():
            for n in range(blocks_per_tile):
                kmean_ref[0, jj * blocks_per_tile + n:jj * blocks_per_tile + n + 1, :] = means[n]
    kt = k.T
    vt = v.T
    for p in range(m // PAGE_SIZE):
        for h in range(N_HEADS):
            rs = slice(h * HEAD_DIM, (h + 1) * HEAD_DIM)
            cs = slice(p * PAGE_SIZE, (p + 1) * PAGE_SIZE)
            ktp_ref[0, p, h] = kt[rs, cs]
            vtp_ref[0, p, h] = vt[rs, cs]
            vt16_ref[0, p, h] = vt[rs, cs].astype(BF16)


def _const_spec(shape):
    return pl.BlockSpec(shape, lambda i, j: (0,) * len(shape), pipeline_mode=pl.Buffered(1))


def _stage_a_sample(x, ada, ln_g, ln_b, wg, wu, wd, win, *, alpha):
    n_seq, s, _ = x.shape
    n_tok = n_seq * s
    x_spec = pl.BlockSpec((n_seq, s, D_MODEL), lambda i, j: (0, 0, 0))
    a_spec = pl.BlockSpec((n_seq, 1, D_MODEL), lambda i, j: (0, 0, 0))
    flat_spec = pl.BlockSpec((n_tok, ATTN_WIDTH), lambda i, j: (0, 0))
    flat_shape = jax.ShapeDtypeStruct((n_tok, ATTN_WIDTH), F32)
    return pl.pallas_call(
        functools.partial(_stage_a_sample_kernel, alpha),
        grid=(1, 1),
        in_specs=[x_spec, a_spec, a_spec, a_spec, a_spec, a_spec,
                  _const_spec((1, D_MODEL)), _const_spec((1, D_MODEL)),
                  _const_spec((D_MODEL, D_FF)), _const_spec((D_MODEL, D_FF)),
                  _const_spec((D_FF, D_MODEL)), _const_spec((D_MODEL, W_IN_COLS))],
        out_specs=[x_spec, flat_spec, flat_spec, flat_spec, flat_spec],
        out_shape=[jax.ShapeDtypeStruct((n_seq, s, D_MODEL), F32)] + [flat_shape] * 4,
        compiler_params=pltpu.CompilerParams(
            dimension_semantics=("arbitrary", "arbitrary"), vmem_limit_bytes=VMEM_LIMIT),
        name="stage_a_sample",
    )(x, ada["sh0"], ada["sc0"], ada["g0"], ada["sh1"], ada["sc1"], ln_g, ln_b, wg, wu, wd, win)


def _stage_a_prompt(x, ada, ln_g, ln_b, wg, wu, wd, page_table_flat, pool_kt, cache_seqs, *, alpha, rows_blk):
    n_seq, s, _ = x.shape
    tiles_per_seq = s // rows_blk
    x_spec = pl.BlockSpec((1, rows_blk, D_MODEL), lambda i, j: (i, j, 0))
    a_spec = pl.BlockSpec((1, 1, D_MODEL), lambda i, j: (i, 0, 0))
    n_steps = n_seq * tiles_per_seq
    pages_per_seq = page_table_flat.shape[0] // cache_seqs
    assert cache_seqs == n_steps and pages_per_seq == STREAM_CHUNKS_PER_SEQ * STREAM_CHUNK_PAGES
    blocks_per_seq = pages_per_seq // PAGES_PER_BLOCK
    return pl.pallas_call(
        functools.partial(_stage_a_prompt_kernel, alpha),
        grid=(n_seq, tiles_per_seq),
        in_specs=[x_spec, a_spec, a_spec, a_spec,
                  _const_spec((1, D_MODEL)), _const_spec((1, D_MODEL)),
                  _const_spec((D_MODEL, D_FF)), _const_spec((D_MODEL, D_FF)), _const_spec((D_FF, D_MODEL)),
                  pl.BlockSpec(memory_space=pltpu.SMEM), pl.BlockSpec(memory_space=pl.ANY)],
        out_specs=[x_spec, pl.BlockSpec((1, N_HEADS, HEAD_DIM, blocks_per_seq),
                                        lambda i, j: (i * tiles_per_seq + j, 0, 0, 0))],
        out_shape=[jax.ShapeDtypeStruct((n_seq, s, D_MODEL), F32),
                   jax.ShapeDtypeStruct((cache_seqs, N_HEADS, HEAD_DIM, blocks_per_seq), F32)],
        scratch_shapes=[pltpu.VMEM((STREAM_SLOTS, STREAM_CHUNK_PAGES, N_HEADS, HEAD_DIM, PAGE_SIZE), F32),
                        pltpu.SemaphoreType.DMA((STREAM_SLOTS,))],
        compiler_params=pltpu.CompilerParams(
            dimension_semantics=("arbitrary", "arbitrary"), vmem_limit_bytes=VMEM_LIMIT),
        name="stage_a_prompt",
    )(x, ada["sh0"], ada["sc0"], ada["g0"], ln_g, ln_b, wg, wu, wd, page_table_flat, pool_kt)


def _stage_b(x1, ada, win, w_dw, b_dw, gn_g, gn_b, *, rows_blk):
    n_seq, s, _ = x1.shape
    tiles_per_seq = s // rows_blk
    assert rows_blk % MOBA_BLOCK == 0 and MOBA_BLOCK % PAGE_SIZE == 0
    assert rows_blk % CONV_ROWS_PER_STEP == 0 and CONV_ROWS_PER_STEP >= CONV_PAD_ROWS
    n_tok = n_seq * s
    x_spec = pl.BlockSpec((1, rows_blk, D_MODEL), lambda i, j: (i, j, 0))
    a_spec = pl.BlockSpec((1, 1, D_MODEL), lambda i, j: (i, 0, 0))
    flat_spec = pl.BlockSpec((rows_blk, ATTN_WIDTH), lambda i, j: (i * tiles_per_seq + j, 0))
    page_dims = (n_seq, s // PAGE_SIZE, N_HEADS, HEAD_DIM, PAGE_SIZE)
    page_spec = pl.BlockSpec((1, rows_blk // PAGE_SIZE, N_HEADS, HEAD_DIM, PAGE_SIZE),
                             lambda i, j: (i, j, 0, 0, 0))
    return pl.pallas_call(
        _stage_b_kernel,
        grid=(n_seq, tiles_per_seq),
        in_specs=[x_spec, a_spec, a_spec, _const_spec((D_MODEL, W_IN_COLS)),
                  _const_spec((CONV_K, CONV_CH)), _const_spec((1, CONV_CH)), _const_spec((1, CONV_CH)),
                  _const_spec((1, CONV_CH)), _const_spec((CONV_CH, CONV_CH))],
        out_specs=[pl.BlockSpec((1, ATTN_WIDTH, rows_blk), lambda i, j: (i, 0, j)),
                   flat_spec,
                   pl.BlockSpec((1, s // MOBA_BLOCK, ATTN_WIDTH), lambda i, j: (i, 0, 0)),
                   page_spec, page_spec, page_spec, flat_spec,
                   pl.BlockSpec((1, CONV_PAD_ROWS, CONV_CH), lambda i, j: (i, 0, 0))],
        out_shape=[jax.ShapeDtypeStruct((n_seq, ATTN_WIDTH, s), F32),
                   jax.ShapeDtypeStruct((n_tok, ATTN_WIDTH), BF16),
                   jax.ShapeDtypeStruct((n_seq, s // MOBA_BLOCK, ATTN_WIDTH), F32),
                   jax.ShapeDtypeStruct(page_dims, F32), jax.ShapeDtypeStruct(page_dims, F32),
                   jax.ShapeDtypeStruct(page_dims, BF16),
                   jax.ShapeDtypeStruct((n_tok, CONV_CH), F32),
                   jax.ShapeDtypeStruct((n_seq, CONV_PAD_ROWS, CONV_CH), F32)],
        scratch_shapes=[pltpu.VMEM((1, CONV_PAD_ROWS, CONV_CH), F32),
                        pltpu.VMEM((1, CONV_PAD_ROWS + CONV_ROWS_PER_STEP + SUBLANES, CONV_CH), F32)],
        compiler_params=pltpu.CompilerParams(
            dimension_semantics=("arbitrary", "arbitrary"), vmem_limit_bytes=VMEM_LIMIT),
        name="stage_b",
    )(x1, ada["sh1"], ada["sc1"], win, w_dw, b_dw.reshape(1, CONV_CH), gn_g.reshape(1, CONV_CH),
      gn_b.reshape(1, CONV_CH), _group_average_matrix(CONV_CH))


SAMPLE_GATHER_SLOTS = 3


def _stage_c_kernel(alpha, sample_cfg, *refs):
    (x1_ref, attn_ref, conv_ref, g1_ref, sh2_ref, sc2_ref, g2_ref, lng1_ref, lnb1_ref, lng2_ref, lnb2_ref,
     ba_ref, bc_ref, wo_ref, wg_ref, wu_ref, wd_ref) = refs[:17]
    if sample_cfg is not None:
        past_len, n_pages = sample_cfg
        y_ref, so_ref = refs[25:27]
        step = pl.program_id(0) * pl.num_programs(1) + pl.program_id(1)
        ring_step = _sample_attention_stream(step, pl.num_programs(0) * pl.num_programs(1), past_len,
                                             n_pages, *refs[17:25], so_ref, *refs[27:30])
        ff_chunks = STREAM_FF_CHUNKS
        last = len(ff_chunks) - 1
        assert len(ff_chunks) + 2 == N_HEADS
        before_up = lambda ci: ring_step(ci + 1)
        before_down = lambda ci: ring_step(N_HEADS - 1) if ci == last else None
        ring_step(0)
    else:
        y_ref = refs[17]
        before_up, before_down, ff_chunks = None, None, FF_CHUNKS
    nseq, rows, d = x1_ref.shape
    m = nseq * rows
    a = (attn_ref[...] * ba_ref[...]).astype(BF16)
    c = (conv_ref[...] * bc_ref[...]).astype(BF16)
    mix = (jnp.dot(a, wo_ref[0:ATTN_WIDTH, :], preferred_element_type=F32)
           + jnp.dot(c, wo_ref[ATTN_WIDTH:, :], preferred_element_type=F32)).reshape(nseq, rows, d)
    x2 = _layernorm(alpha * x1_ref[...] + g1_ref[...] * mix, lng1_ref[...], lnb1_ref[...])
    h2 = (x2 * (1.0 + sc2_ref[...]) + sh2_ref[...]).reshape(m, d).astype(BF16)
    y = _ffn(h2, wg_ref, wu_ref, wd_ref, ff_chunks, before_up, before_down).reshape(nseq, rows, d)
    y_ref[...] = _layernorm(alpha * x2 + g2_ref[...] * (0.5 * y), lng2_ref[...], lnb2_ref[...])


def _stage_c(x1, attn, conv, ada, ln_g1, ln_b1, ln_g2, ln_b2, beta_a, beta_c, wo, wg, wu, wd,
             *, alpha, nseq_blk, rows_blk, sample=None):
    n_seq, s, _ = x1.shape
    tiles_per_seq = s // rows_blk
    assert nseq_blk == 1 or (nseq_blk == n_seq and tiles_per_seq == 1)
    m_blk = nseq_blk * rows_blk
    x_spec = pl.BlockSpec((nseq_blk, rows_blk, D_MODEL), lambda i, j: (i, j, 0))
    a_spec = pl.BlockSpec((nseq_blk, 1, D_MODEL), lambda i, j: (i, 0, 0))
    flat = pl.BlockSpec((m_blk, ATTN_WIDTH), lambda i, j: (i * tiles_per_seq + j, 0))
    out_specs = [x_spec]
    out_shape = [jax.ShapeDtypeStruct((n_seq, s, D_MODEL), F32)]
    extra_in, extra_in_specs, scratch, cfg = [], [], [], None
    if sample is not None:
        seqs, _, _, t = sample["qt"].shape
        assert seqs == n_seq * tiles_per_seq
        n_k = t * MOBA_TOPK * PAGES_PER_BLOCK
        cfg = (sample["past_len"], sample["n_pages"])
        smem = pl.BlockSpec(memory_space=pltpu.SMEM)
        hbm = pl.BlockSpec(memory_space=pl.ANY)
        seq_spec = pl.BlockSpec((1, N_HEADS, HEAD_DIM, t), lambda i, j: (i * tiles_per_seq + j, 0, 0, 0))
        extra_in = [sample["idx_flat"], sample["page_table_flat"], sample["slopes"],
                    sample["qt"], sample["knt"], sample["vnt"], sample["pool_kt"], sample["pool_vt"]]
        extra_in_specs = [smem, smem, smem, seq_spec, seq_spec, seq_spec, hbm, hbm]
        out_specs.append(seq_spec)
        out_shape.append(jax.ShapeDtypeStruct((seqs, N_HEADS, HEAD_DIM, t), F32))
        scratch = [pltpu.VMEM((SAMPLE_GATHER_SLOTS, n_k, HEAD_DIM, PAGE_SIZE), F32),
                   pltpu.VMEM((SAMPLE_GATHER_SLOTS, n_k, HEAD_DIM, PAGE_SIZE), F32),
                   pltpu.SemaphoreType.DMA((SAMPLE_GATHER_SLOTS,))]
    res = pl.pallas_call(
        functools.partial(_stage_c_kernel, alpha, cfg),
        grid=(n_seq // nseq_blk, tiles_per_seq),
        in_specs=[x_spec, flat, flat, a_spec, a_spec, a_spec, a_spec,
                  _const_spec((1, D_MODEL)), _const_spec((1, D_MODEL)),
                  _const_spec((1, D_MODEL)), _const_spec((1, D_MODEL)),
                  _const_spec((1, ATTN_WIDTH)), _const_spec((1, CONV_CH)),
                  _const_spec((D_MODEL, D_MODEL)),
                  _const_spec((D_MODEL, D_FF)), _const_spec((D_MODEL, D_FF)),
                  _const_spec((D_FF, D_MODEL))] + extra_in_specs,
        out_specs=out_specs,
        out_shape=out_shape,
        scratch_shapes=scratch,
        compiler_params=pltpu.CompilerParams(
            dimension_semantics=("arbitrary", "arbitrary"), vmem_limit_bytes=VMEM_LIMIT),
        name="stage_c",
    )(x1, attn, conv, ada["g1"], ada["sh2"], ada["sc2"], ada["g2"],
      ln_g1, ln_b1, ln_g2, ln_b2, beta_a, beta_c, wo, wg, wu, wd, *extra_in)
    return res if sample is not None else res[0]


def _conv_rows(buf_ref, halo, cur, w_ref, bdw_ref, gng_ref, gnb_ref, gavg_ref):
    nb, ch, c = cur.shape
    pad = CONV_PAD_ROWS
    buf_ref[:, 0:pad - HIST, :] = jnp.zeros((nb, pad - HIST, c), F32)
    buf_ref[:, pad - HIST:pad, :] = halo
    buf_ref[:, pad:pad + ch, :] = cur
    buf_ref[:, pad + ch:, :] = jnp.zeros((nb, SUBLANES, c), F32)
    acc = None
    for r in range(SUBLANES):
        part = None
        for a in range((pad + SUBLANES) // SUBLANES):
            j = SUBLANES * a + r - (pad - HIST)
            if 0 <= j < CONV_K:
                term = buf_ref[:, SUBLANES * a:SUBLANES * a + ch + SUBLANES, :] * w_ref[j:j + 1, :]
                part = term if part is None else part + term
        shifted = part[:, r:r + ch, :]
        acc = shifted if acc is None else acc + shifted
    y = (acc + bdw_ref[...]).reshape(nb * ch, c)
    gavg = gavg_ref[...]

    def group_mean(t):
        hi, lo = _split_bf16(t)
        return (jnp.dot(hi, gavg, preferred_element_type=F32)
                + jnp.dot(lo, gavg, preferred_element_type=F32))

    mu = group_mean(y)
    dlt = y - mu
    var = group_mean(dlt * dlt)
    z = dlt * lax.rsqrt(var + LN_EPS) * gng_ref[...] + gnb_ref[...]
    return _silu(z)


def _conv_kernel(n_chunks, hist_ref, prev_ref, cur_ref, w_ref, bdw_ref, gng_ref, gnb_ref, gavg_ref,
                 o_ref, buf_ref):
    nb, ch, c = cur_ref.shape
    if n_chunks == 1:
        halo = hist_ref[...]
    else:
        halo = jnp.where(pl.program_id(1) == 0, hist_ref[...], prev_ref[:, ch - HIST:, :])
    out = _conv_rows(buf_ref, halo, cur_ref[...], w_ref, bdw_ref, gng_ref, gnb_ref, gavg_ref)
    o_ref[...] = out.reshape(nb, ch, c)


def _group_average_matrix(c):
    grp = np.arange(c) // (c // GN_GROUPS)
    return jnp.asarray((grp[:, None] == grp[None, :]).astype(np.float32) / (c // GN_GROUPS), BF16)


def _conv_branch(hist, glu3, w_dw, b_dw, gn_g, gn_b, *, nb, ch):
    n_seq, s, c = glu3.shape
    n_chunks = s // ch
    assert ch >= HIST or n_chunks == 1
    gavg = _group_average_matrix(c)
    cur_spec = pl.BlockSpec((nb, ch, c), lambda i, j: (i, j, 0))
    prev_spec = pl.BlockSpec((nb, ch, c), lambda i, j: (i, jnp.maximum(j - 1, 0), 0))
    return pl.pallas_call(
        functools.partial(_conv_kernel, n_chunks),
        grid=(n_seq // nb, n_chunks),
        in_specs=[pl.BlockSpec((nb, HIST, c), lambda i, j: (i, 0, 0)), prev_spec, cur_spec,
                  pl.BlockSpec((CONV_K, c), lambda i, j: (0, 0)),
                  pl.BlockSpec((1, c), lambda i, j: (0, 0)),
                  pl.BlockSpec((1, c), lambda i, j: (0, 0)),
                  pl.BlockSpec((1, c), lambda i, j: (0, 0)),
                  pl.BlockSpec((c, c), lambda i, j: (0, 0))],
        out_specs=cur_spec,
        out_shape=jax.ShapeDtypeStruct((n_seq, s, c), F32),
        scratch_shapes=[pltpu.VMEM((nb, CONV_PAD_ROWS + ch + SUBLANES, c), F32)],
        compiler_params=pltpu.CompilerParams(dimension_semantics=("arbitrary", "arbitrary")),
        name="conv_branch",
    )(hist, glu3, glu3, w_dw, b_dw.reshape(1, c), gn_g.reshape(1, c), gn_b.reshape(1, c), gavg)


MOBA_HEADS_PER_STEP = 8


def _moba_prompt_kernel(slopes_ref, qt_ref, k16_ref, vt16_ref, kmean_ref, o_ref,
                        q16_ref, bias_ref, krs_ref, s_ref, m_ref, l_ref, acc_ref):
    grp0 = pl.program_id(1)
    qb = pl.program_id(2)
    blk = MOBA_BLOCK
    nb = kmean_ref.shape[1]
    hps = MOBA_HEADS_PER_STEP
    row_head = lax.broadcasted_iota(jnp.int32, (LANES, blk), 0) // HEAD_DIM
    key_i = lax.broadcasted_iota(jnp.int32, (blk, blk), 0)
    qry_i = lax.broadcasted_iota(jnp.int32, (blk, blk), 1)
    key_f = key_i.astype(F32)
    blk_id = lax.broadcasted_iota(jnp.int32, (nb, blk), 0)
    past = blk_id < qb
    zero_row = jnp.zeros((1, blk), jnp.int32)

    def lanes_of(hh):
        g = hh // HEADS_PER_GROUP
        return slice(g * LANES, (g + 1) * LANES)

    def scores(hh, n):
        kb = k16_ref[0, pl.ds(pl.multiple_of(n * blk, blk), blk), lanes_of(hh)]
        return jnp.dot(kb, q16_ref[hh], preferred_element_type=F32) + krs_ref[hh]

    def pv(hh, n, p):
        p16 = p.astype(BF16)
        acc = None
        for i in range(PAGES_PER_BLOCK):
            vt = vt16_ref[0, n * PAGES_PER_BLOCK + i, hh]
            part = jnp.dot(vt, p16[i * PAGE_SIZE:(i + 1) * PAGE_SIZE, :], preferred_element_type=F32)
            acc = part if acc is None else acc + part
        return acc

    @pl.when(qb == 0)
    def _():
        for hh in range(hps):
            krs_ref[hh] = key_f * slopes_ref[grp0 * hps + hh]

    for hh in range(hps):
        sub = hh % HEADS_PER_GROUP
        qh = jnp.where(row_head == sub, qt_ref[0, lanes_of(hh), :], 0.0)
        q16_ref[hh] = (qh * (SCALE * LOG2E)).astype(BF16)

        gate = jnp.dot(kmean_ref[0, :, lanes_of(hh)], qh, preferred_element_type=F32, precision=HIGHEST)
        cnt = jnp.zeros((nb, blk), F32)
        for mth in range(nb):
            other = gate[mth:mth + 1, :]
            ahead = (other > gate) | ((other == gate) & (blk_id > mth))
            cnt = cnt + jnp.where(ahead & (mth < qb), 1.0, 0.0)
        bias_ref[hh] = jnp.where((cnt < MOBA_TOPK) & past, 0.0, NEG_INF)

        s = jnp.where(key_i <= qry_i, scores(hh, qb), NEG_INF)
        s_ref[hh, qb] = s
        m_ref[hh] = jnp.max(s, axis=0, keepdims=True)

    def block_const(hh, n):
        rel = ((n - qb) * blk + zero_row).astype(F32)
        return slopes_ref[grp0 * hps + hh] * rel + bias_ref[hh, pl.ds(n, 1), :]

    def pass1(n, carry):
        for hh in range(hps):
            s = scores(hh, n)
            s_ref[hh, n] = s
            m_ref[hh] = jnp.maximum(m_ref[hh], jnp.max(s, axis=0, keepdims=True) + block_const(hh, n))
        return carry

    lax.fori_loop(0, qb, pass1, 0)

    def block_out(hh, n, c):
        p = jnp.exp2(s_ref[hh, n] - (m_ref[hh] - c))
        return jnp.sum(p, axis=0, keepdims=True), pv(hh, n, p)

    for hh in range(hps):
        l_ref[hh], acc_ref[hh] = block_out(hh, qb, 0.0)

    def pass2(n, carry):
        for hh in range(hps):
            l_part, acc_part = block_out(hh, n, block_const(hh, n))
            l_ref[hh] = l_ref[hh] + l_part
            acc_ref[hh] = acc_ref[hh] + acc_part
        return carry

    lax.fori_loop(0, qb, pass2, 0)
    out_t = jnp.concatenate([acc_ref[hh] / l_ref[hh] for hh in range(hps)], axis=0)
    o_ref[0] = out_t.T


def _moba_prompt(qt3, k16, vt16, kmean, slopes):
    b, s, _ = k16.shape
    hps = MOBA_HEADS_PER_STEP
    width = hps * HEAD_DIM
    n_blocks = s // MOBA_BLOCK
    return pl.pallas_call(
        _moba_prompt_kernel,
        grid=(b, N_HEADS // hps, n_blocks),
        in_specs=[pl.BlockSpec(memory_space=pltpu.SMEM),
                  pl.BlockSpec((1, width, MOBA_BLOCK), lambda i, g, j: (i, g, j)),
                  pl.BlockSpec((1, s, width), lambda i, g, j: (i, 0, g)),
                  pl.BlockSpec((1, s // PAGE_SIZE, hps, HEAD_DIM, PAGE_SIZE),
                               lambda i, g, j: (i, 0, g, 0, 0)),
                  pl.BlockSpec((1, n_blocks, width), lambda i, g, j: (i, 0, g))],
        out_specs=pl.BlockSpec((1, MOBA_BLOCK, width), lambda i, g, j: (i, j, g)),
        out_shape=jax.ShapeDtypeStruct((b, s, ATTN_WIDTH), F32),
        scratch_shapes=[pltpu.VMEM((hps, LANES, MOBA_BLOCK), BF16),
                        pltpu.VMEM((hps, n_blocks, MOBA_BLOCK), F32),
                        pltpu.VMEM((hps, MOBA_BLOCK, MOBA_BLOCK), F32),
                        pltpu.VMEM((hps, n_blocks, MOBA_BLOCK, MOBA_BLOCK), F32),
                        pltpu.VMEM((hps, 1, MOBA_BLOCK), F32),
                        pltpu.VMEM((hps, 1, MOBA_BLOCK), F32),
                        pltpu.VMEM((hps, HEAD_DIM, MOBA_BLOCK), F32)],
        compiler_params=pltpu.CompilerParams(
            dimension_semantics=("arbitrary", "arbitrary", "arbitrary"), vmem_limit_bytes=VMEM_LIMIT),
        name="moba_prompt",
    )(slopes, qt3, k16, vt16, kmean)


TOPK_SEQS_PER_STEP = 4


def _topk_kernel(q_ref, kmt_ref, o_ref):
    ns, t, nbp = q_ref.shape[0], q_ref.shape[1], kmt_ref.shape[3]
    rows = ns * N_HEADS * t
    gates = []
    for s in range(ns):
        q = q_ref[s]
        for h in range(N_HEADS):
            gates.append(jnp.dot(q[:, h * HEAD_DIM:(h + 1) * HEAD_DIM], kmt_ref[s, h],
                                 preferred_element_type=F32, precision=HIGHEST))
    g = jnp.concatenate(gates, axis=0)
    lane = lax.broadcasted_iota(jnp.int32, (rows, nbp), 1).astype(F32)
    out_lane = lax.broadcasted_iota(jnp.int32, (rows, LANES), 1)
    res = jnp.zeros((rows, LANES), F32)
    for r in range(MOBA_TOPK):
        mx = jnp.max(g, axis=1, keepdims=True)
        idx = jnp.min(jnp.where(g == mx, lane, float(nbp)), axis=1, keepdims=True)
        res = jnp.where(out_lane == r, idx, res)
        g = jnp.where(lane == idx, -jnp.inf, g)
    o_ref[...] = res.astype(jnp.int32).reshape(ns, N_HEADS, t, LANES)


def _topk_blocks(q3, kmean_t):
    n_seq, t, _ = q3.shape
    nbp = kmean_t.shape[3]
    ns = TOPK_SEQS_PER_STEP
    return pl.pallas_call(
        _topk_kernel,
        grid=(n_seq // ns,),
        in_specs=[pl.BlockSpec((ns, t, ATTN_WIDTH), lambda b: (b, 0, 0)),
                  pl.BlockSpec((ns, N_HEADS, HEAD_DIM, nbp), lambda b: (b, 0, 0, 0))],
        out_specs=pl.BlockSpec((ns, N_HEADS, t, LANES), lambda b: (b, 0, 0, 0)),
        out_shape=jax.ShapeDtypeStruct((n_seq, N_HEADS, t, LANES), jnp.int32),
        name="topk_blocks",
    )(q3, kmean_t)


def _sample_attention_stream(step, n_steps, past_len, n_pages, idx_ref, pt_ref, slopes_ref,
                             qt_ref, knt_ref, vnt_ref, pool_k_ref, pool_v_ref, o_ref,
                             kbuf_ref, vbuf_ref, sem_ref):
    t = qt_ref.shape[3]
    slabs = MOBA_TOPK * PAGES_PER_BLOCK
    n_k = t * slabs

    def slab_copies(page, head, slot, n):
        return (pltpu.make_async_copy(pool_k_ref.at[page, head], kbuf_ref.at[slot, n], sem_ref.at[slot]),
                pltpu.make_async_copy(pool_v_ref.at[page, head], vbuf_ref.at[slot, n], sem_ref.at[slot]))

    def start_gather(seq, head, slot):
        for n in range(n_k):
            tq, rem = divmod(n, slabs)
            j, i = divmod(rem, PAGES_PER_BLOCK)
            blk_idx = idx_ref[((seq * N_HEADS + head) * t + tq) * MOBA_TOPK + j]
            page = pt_ref[seq * n_pages + blk_idx * PAGES_PER_BLOCK + i]
            for cp in slab_copies(page, head, slot, n):
                cp.start()

    n_slots = kbuf_ref.shape[0]
    ahead = n_slots - 1
    assert ahead < N_HEADS

    def ring_step(h):
        unit = step * N_HEADS + h
        slot = lax.rem(unit, n_slots)
        if h == 0:
            @pl.when(step == 0)
            def _():
                for a in range(ahead):
                    start_gather(0, a, a)
        nxt_slot = lax.rem(unit + ahead, n_slots)
        if h + ahead < N_HEADS:
            start_gather(step, h + ahead, nxt_slot)
        else:
            @pl.when(step + 1 < n_steps)
            def _():
                start_gather(step + 1, h + ahead - N_HEADS, nxt_slot)
        for n in range(n_k):
            for cp in slab_copies(0, 0, slot, n):
                cp.wait()

        slope = slopes_ref[h]
        qt = qt_ref[0, h] * SCALE
        knt = knt_ref[0, h]
        vnt = vnt_ref[0, h]
        srow = lax.broadcasted_iota(jnp.int32, (slabs, PAGE_SIZE), 0)
        lane = lax.broadcasted_iota(jnp.int32, (slabs, PAGE_SIZE), 1)
        own_pos = lax.broadcasted_iota(jnp.int32, (1, t), 1)
        out_lane = lax.broadcasted_iota(jnp.int32, (HEAD_DIM, t), 1)
        out = jnp.zeros((HEAD_DIM, t), F32)
        for tq in range(t):
            q_col = qt[:, tq:tq + 1]
            base = ((step * N_HEADS + h) * t + tq) * MOBA_TOPK
            s_rows = []
            blk_of_row = jnp.zeros((slabs, PAGE_SIZE), jnp.int32)
            for j in range(MOBA_TOPK):
                blk_of_row = jnp.where(srow // PAGES_PER_BLOCK == j, idx_ref[base + j], blk_of_row)
                for i in range(PAGES_PER_BLOCK):
                    kt = kbuf_ref[slot, (tq * MOBA_TOPK + j) * PAGES_PER_BLOCK + i]
                    s_rows.append(jnp.sum(kt * q_col, axis=0, keepdims=True))
            s_sel = jnp.concatenate(s_rows, axis=0)
            k_pos = blk_of_row * MOBA_BLOCK + (srow % PAGES_PER_BLOCK) * PAGE_SIZE + lane
            s_sel = s_sel - slope * ((past_len + tq) - k_pos).astype(F32)
            rel_own = tq - own_pos
            s_own = jnp.sum(knt * q_col, axis=0, keepdims=True)
            s_own = jnp.where(rel_own >= 0, s_own - slope * rel_own.astype(F32), NEG_INF)
            m = jnp.maximum(jnp.max(jnp.max(s_sel, axis=1, keepdims=True), axis=0, keepdims=True),
                            jnp.max(s_own, axis=1, keepdims=True))
            p_sel = jnp.exp(s_sel - m)
            p_own = jnp.exp(s_own - m)
            l = (jnp.sum(jnp.sum(p_sel, axis=1, keepdims=True), axis=0, keepdims=True)
                 + jnp.sum(p_own, axis=1, keepdims=True))
            acc = jnp.zeros((HEAD_DIM, PAGE_SIZE), F32)
            for r in range(slabs):
                acc = acc + vbuf_ref[slot, tq * slabs + r] * p_sel[r:r + 1, :]
            o_col = (jnp.sum(acc, axis=1, keepdims=True) + jnp.sum(vnt * p_own, axis=1, keepdims=True)) / l
            out = jnp.where(out_lane == tq, o_col, out)
        o_ref[0, h] = out

    return ring_step


def _split_ada(ada_rows):
    a = ada_rows.reshape(ada_rows.shape[0], 3, 3, 1, D_MODEL)
    return {"sh0": a[:, 0, 0], "sc0": a[:, 0, 1], "g0": a[:, 0, 2],
            "sh1": a[:, 1, 0], "sc1": a[:, 1, 1], "g1": a[:, 1, 2],
            "sh2": a[:, 2, 0], "sc2": a[:, 2, 1], "g2": a[:, 2, 2]}


def _heads(t2, n_seq, s):
    return t2.reshape(n_seq, s, N_HEADS, HEAD_DIM)


def kernel(x_prompt, x_sample, cache_k, cache_v, state_conv, page_table, c_prompt, c_sample,
           w_ada, b_ada, ffn1_wg, ffn1_wu, ffn1_wd, w_in, w_dw, b_dw, gn_g, gn_b,
           beta_attn, beta_conv, w_out, ffn2_wg, ffn2_wu, ffn2_wd, ln_g, ln_b):
    depth = w_ada.shape[0]
    alpha = (2.0 * depth) ** 0.25
    batch, seq, _ = x_prompt.shape
    dec_batch, dec_seq, _ = x_sample.shape
    n_pages = page_table.shape[1]
    past_len = n_pages * PAGE_SIZE
    assert past_len % MOBA_BLOCK == 0 and dec_seq <= MOBA_BLOCK
    alibi = 2.0 ** (-8.0 * np.arange(1, N_HEADS + 1) / N_HEADS)
    slopes = jnp.asarray(alibi, F32)
    slopes_log2 = jnp.asarray(alibi * LOG2E, F32)
    page_table_flat = page_table.reshape(-1)

    y_p, y_s = x_prompt, x_sample
    outs = {name: [] for name in ("kp", "vp", "cp", "ks", "vs", "cs")}
    for l in range(depth):
        ada = _ada(jnp.concatenate([c_prompt, c_sample], axis=0), w_ada[l], b_ada[l])
        ada_p, ada_s = _split_ada(ada[:batch]), _split_ada(ada[batch:])
        lng = [ln_g[l, i].reshape(1, D_MODEL) for i in range(3)]
        lnb = [ln_b[l, i].reshape(1, D_MODEL) for i in range(3)]
        w1 = (ffn1_wg[l].astype(BF16), ffn1_wu[l].astype(BF16), ffn1_wd[l].astype(BF16))
        w2 = (ffn2_wg[l].astype(BF16), ffn2_wu[l].astype(BF16), ffn2_wd[l].astype(BF16))
        win16, wo16 = w_in[l].astype(BF16), w_out[l].astype(BF16)
        beta_a = beta_attn[l].reshape(1, ATTN_WIDTH)
        beta_c = beta_conv[l].reshape(1, CONV_CH)

        tile_p = dict(alpha=alpha, nseq_blk=1, rows_blk=PROMPT_ROWS_PER_STEP)
        tile_s = dict(alpha=alpha, nseq_blk=dec_batch, rows_blk=dec_seq)
        pool_kt = jnp.swapaxes(cache_k[l], -1, -2)
        pool_vt = jnp.swapaxes(cache_v[l], -1, -2)

        x1_s, q, k, v, glu_s = _stage_a_sample(y_s, ada_s, lng[0], lnb[0], *w1, win16, alpha=alpha)
        q4, k4, v4 = (_heads(t2, dec_batch, dec_seq) for t2 in (q, k, v))

        x1, kmean_t = _stage_a_prompt(y_p, ada_p, lng[0], lnb[0], *w1, page_table_flat, pool_kt, dec_batch,
                                      alpha=alpha, rows_blk=PROMPT_ROWS_PER_STEP)
        qt, k16, kmean, ktp, vtp, vt16, conv, glu_tail = _stage_b(
            x1, ada_p, win16, w_dw[l], b_dw[l], gn_g[l], gn_b[l], rows_blk=PROMPT_ROWS_PER_STEP)
        idx = _topk_blocks(q.reshape(dec_batch, dec_seq, ATTN_WIDTH), kmean_t)[..., :MOBA_TOPK]
        attn = _moba_prompt(qt, k16.reshape(batch, seq, ATTN_WIDTH), vt16, kmean, slopes_log2)
        to_t = lambda a4: a4.transpose(0, 2, 3, 1)
        sample = dict(qt=to_t(q4), knt=to_t(k4), vnt=to_t(v4), pool_kt=pool_kt, pool_vt=pool_vt,
                      idx_flat=idx.reshape(-1), page_table_flat=page_table_flat, slopes=slopes,
                      past_len=past_len, n_pages=n_pages)
        y_p, attn_t = _stage_c(x1, attn.reshape(batch * seq, ATTN_WIDTH), conv, ada_p,
                               lng[1], lnb[1], lng[2], lnb[2], beta_a, beta_c, wo16, *w2,
                               sample=sample, **tile_p)
        outs["kp"].append(jnp.swapaxes(ktp, -1, -2))
        outs["vp"].append(jnp.swapaxes(vtp, -1, -2))
        outs["cp"].append(glu_tail[:, CONV_PAD_ROWS - HIST:, :])

        x1, glu = x1_s, glu_s
        attn = attn_t.transpose(0, 3, 1, 2).reshape(dec_batch * dec_seq, ATTN_WIDTH)
        glu3 = glu.reshape(dec_batch, dec_seq, CONV_CH)
        hist = state_conv[l]
        conv = _conv_branch(hist, glu3, w_dw[l], b_dw[l], gn_g[l], gn_b[l], nb=dec_batch, ch=dec_seq)
        y_s = _stage_c(x1, attn, conv.reshape(dec_batch * dec_seq, CONV_CH),
                       ada_s, lng[1], lnb[1], lng[2], lnb[2], beta_a, beta_c, wo16, *w2, **tile_s)
        outs["ks"].append(k4.transpose(0, 2, 1, 3))
        outs["vs"].append(v4.transpose(0, 2, 1, 3))
        outs["cs"].append(jnp.concatenate([hist, glu3], axis=1)[:, dec_seq:, :])

    stack = lambda name: jnp.stack(outs[name], 0)
    return (y_p, y_s, stack("kp"), stack("vp"), stack("cp"), stack("ks"), stack("vs"), stack("cs"))
```

```python
import functools

import numpy as np
import jax
import jax.numpy as jnp
from jax import lax
from jax.experimental import pallas as pl
from jax.experimental.pallas import tpu as pltpu

F32 = jnp.float32
BF16 = jnp.bfloat16
HIGHEST = lax.Precision.HIGHEST

D_MODEL = 1024
D_FF = 2816
N_HEADS = 8
HEAD_DIM = 64
ATTN_WIDTH = N_HEADS * HEAD_DIM
CONV_CH = 512
W_IN_COLS = 3 * ATTN_WIDTH + 2 * CONV_CH
CONV_K = 31
HIST = CONV_K - 1
GN_GROUPS = 8
MOBA_BLOCK = 256
MOBA_TOPK = 3
PAGE_SIZE = 128
PAGES_PER_BLOCK = MOBA_BLOCK // PAGE_SIZE
LN_EPS = 1e-5
NEG_INF = -1e30
SCALE = HEAD_DIM ** -0.5

LANES = 128
SUBLANES = 8
HEADS_PER_GROUP = LANES // HEAD_DIM
CONV_PAD_ROWS = 32
LOG2E = 1.4426950408889634
VMEM_LIMIT = 60 * 1024 * 1024

FF_CHUNKS = ((0, 1024), (1024, 2048), (2048, D_FF))
PROMPT_ROWS_PER_STEP = 512
CONV_ROWS_PER_STEP = 256


def _silu(x):
    return x / (1.0 + jnp.exp(-x))


def _sigmoid(x):
    return 1.0 / (1.0 + jnp.exp(-x))


def _layernorm(t, g, b):
    mu = jnp.mean(t, axis=-1, keepdims=True)
    d = t - mu
    var = jnp.mean(d * d, axis=-1, keepdims=True)
    return d * lax.rsqrt(var + LN_EPS) * g + b


def _ffn(h, wg_ref, wu_ref, wd_ref, chunks=FF_CHUNKS, before_up=None, before_down=None):
    acc = None
    for ci, (lo, hi) in enumerate(chunks):
        if before_up is not None:
            before_up(ci)
        g = jnp.dot(h, wg_ref[:, lo:hi], preferred_element_type=F32)
        u = jnp.dot(h, wu_ref[:, lo:hi], preferred_element_type=F32)
        a = (_silu(g) * u).astype(BF16)
        if before_down is not None:
            before_down(ci)
        y = jnp.dot(a, wd_ref[lo:hi, :], preferred_element_type=F32)
        acc = y if acc is None else acc + y
    return acc


def _split_bf16(t):
    hi = t.astype(BF16)
    return hi, (t - hi.astype(F32)).astype(BF16)


def _ada_kernel(c_ref, w_ref, b_ref, o_ref):
    a_hi, a_lo = _split_bf16(_silu(c_ref[...]))
    w_hi, w_lo = _split_bf16(w_ref[...])
    o_ref[...] = (jnp.dot(a_hi, w_hi, preferred_element_type=F32)
                  + jnp.dot(a_lo, w_hi, preferred_element_type=F32)
                  + jnp.dot(a_hi, w_lo, preferred_element_type=F32)) + b_ref[...]


def _ada(c_all, w_ada, b_ada):
    n = c_all.shape[0]
    tn = 1024
    return pl.pallas_call(
        _ada_kernel,
        grid=(w_ada.shape[1] // tn,),
        in_specs=[
            pl.BlockSpec((n, D_MODEL), lambda j: (0, 0)),
            pl.BlockSpec((D_MODEL, tn), lambda j: (0, j)),
            pl.BlockSpec((1, tn), lambda j: (0, j)),
        ],
        out_specs=pl.BlockSpec((n, tn), lambda j: (0, j)),
        out_shape=jax.ShapeDtypeStruct((n, w_ada.shape[1]), F32),
        name="ada",
    )(c_all, w_ada, b_ada.reshape(1, -1))


STREAM_CHUNK_PAGES = 16
STREAM_SLOTS = 3
STREAM_FF_CHUNKS = ((0, 512), (512, 1024), (1024, 1536), (1536, 2048), (2048, 2560), (2560, D_FF))
W_IN_PARTS = ((0, ATTN_WIDTH), (ATTN_WIDTH, 2 * ATTN_WIDTH), (2 * ATTN_WIDTH, 3 * ATTN_WIDTH),
              (3 * ATTN_WIDTH, W_IN_COLS))
STREAM_HOOKS = (("up", 0), ("up", 1), ("down", 1), ("up", 2), ("up", 3), ("up", 4), ("down", 4), ("up", 5))
STREAM_CHUNKS_PER_SEQ = len(STREAM_HOOKS)


def _block_mean_stream(step, n_steps, first, count, pt_ref, pool_ref, kmt_ref, ring_ref, sem_ref):
    ahead = STREAM_SLOTS - 1
    blocks_per_chunk = STREAM_CHUNK_PAGES // PAGES_PER_BLOCK
    n_cols = kmt_ref.shape[3]
    assert count * blocks_per_chunk == n_cols and ahead <= count

    def page_copy(page, slot, i):
        return pltpu.make_async_copy(pool_ref.at[page], ring_ref.at[slot, i], sem_ref.at[slot])

    def start_chunk(for_step, k, slot):
        g = for_step * STREAM_CHUNKS_PER_SEQ + first + k
        for i in range(STREAM_CHUNK_PAGES):
            page_copy(pt_ref[g * STREAM_CHUNK_PAGES + i], slot, i).start()

    def ring_step(k):
        seq_no = step * count + k
        slot = lax.rem(seq_no, STREAM_SLOTS)
        if k == 0:
            @pl.when(step == 0)
            def _():
                for a in range(ahead):
                    start_chunk(0, a, a)
        nxt_slot = lax.rem(seq_no + ahead, STREAM_SLOTS)
        if k + ahead < count:
            start_chunk(step, k + ahead, nxt_slot)
        else:
            @pl.when(step + 1 < n_steps)
            def _():
                start_chunk(step + 1, k + ahead - count, nxt_slot)
        for i in range(STREAM_CHUNK_PAGES):
            page_copy(0, slot, i).wait()
        col_id = lax.broadcasted_iota(jnp.int32, (HEAD_DIM, n_cols), 1)
        cols = slice(k * blocks_per_chunk, (k + 1) * blocks_per_chunk)
        for h in range(N_HEADS):
            acc = jnp.zeros((HEAD_DIM, n_cols), F32)
            for jb in range(blocks_per_chunk):
                tot = ring_ref[slot, jb * PAGES_PER_BLOCK, h]
                for i in range(1, PAGES_PER_BLOCK):
                    tot = tot + ring_ref[slot, jb * PAGES_PER_BLOCK + i, h]
                mean = jnp.sum(tot, axis=1, keepdims=True) * (1.0 / MOBA_BLOCK)
                acc = jnp.where(col_id == k * blocks_per_chunk + jb, mean, acc)
            kmt_ref[0, h, :, cols] = acc[:, cols]

    return ring_step


def _first_ffn(alpha, x_ref, sh0_ref, sc0_ref, g0_ref, lng_ref, lnb_ref, wg_ref, wu_ref, wd_ref,
               ff_chunks, before_up=None, before_down=None):
    nseq, rows, d = x_ref.shape
    x = x_ref[...]
    h0 = (x * (1.0 + sc0_ref[...]) + sh0_ref[...]).reshape(nseq * rows, d).astype(BF16)
    y = _ffn(h0, wg_ref, wu_ref, wd_ref, ff_chunks, before_up, before_down).reshape(nseq, rows, d)
    return _layernorm(alpha * x + g0_ref[...] * (0.5 * y), lng_ref[...], lnb_ref[...])


def _project(x1, sh1_ref, sc1_ref, win_ref, on_glu=None):
    nseq, rows, d = x1.shape
    h1 = (x1 * (1.0 + sc1_ref[...]) + sh1_ref[...]).reshape(nseq * rows, d).astype(BF16)
    part = lambda lo, hi: jnp.dot(h1, win_ref[:, lo:hi], preferred_element_type=F32)
    ug = part(*W_IN_PARTS[3])
    glu = ug[:, 0:CONV_CH] * _sigmoid(ug[:, CONV_CH:])
    qkv = []
    for pi, (lo, hi) in enumerate(W_IN_PARTS[:3]):
        if on_glu is not None:
            on_glu(glu, pi)
        qkv.append(part(lo, hi))
    q, k, v = qkv
    return q, k, v, glu


def _stage_a_sample_kernel(alpha, x_ref, sh0_ref, sc0_ref, g0_ref, sh1_ref, sc1_ref, lng_ref, lnb_ref,
                           wg_ref, wu_ref, wd_ref, win_ref, x1_ref, q_ref, k_ref, v_ref, glu_ref):
    x1 = _first_ffn(alpha, x_ref, sh0_ref, sc0_ref, g0_ref, lng_ref, lnb_ref, wg_ref, wu_ref, wd_ref,
                    FF_CHUNKS)
    x1_ref[...] = x1
    q_ref[...], k_ref[...], v_ref[...], glu_ref[...] = _project(x1, sh1_ref, sc1_ref, win_ref)


def _stage_a_prompt_kernel(alpha, x_ref, sh0_ref, sc0_ref, g0_ref, lng_ref, lnb_ref, wg_ref, wu_ref, wd_ref,
                           pt_ref, pool_ref, x1_ref, kmt_ref, ring_ref, sem_ref):
    step = pl.program_id(0) * pl.num_programs(1) + pl.program_id(1)
    ring_step = _block_mean_stream(step, pl.num_programs(0) * pl.num_programs(1), 0, STREAM_CHUNKS_PER_SEQ,
                                   pt_ref, pool_ref, kmt_ref, ring_ref, sem_ref)

    def hook(kind):
        return lambda ci: ring_step(STREAM_HOOKS.index((kind, ci))) if (kind, ci) in STREAM_HOOKS else None

    x1_ref[...] = _first_ffn(alpha, x_ref, sh0_ref, sc0_ref, g0_ref, lng_ref, lnb_ref, wg_ref, wu_ref, wd_ref,
                             STREAM_FF_CHUNKS, hook("up"), hook("down"))


def _stage_b_kernel(x1_ref, sh1_ref, sc1_ref, win_ref, w_ref, bdw_ref, gng_ref, gnb_ref, gavg_ref,
                    qt_ref, k16_ref, kmean_ref, ktp_ref, vtp_ref, vt16_ref, conv_ref, tail_ref,
                    halo_ref, cbuf_ref):
    m = x1_ref.shape[1]

    n_conv = m // CONV_ROWS_PER_STEP

    def conv_branch(glu, ci):
        if ci >= n_conv:
            return
        lo = ci * CONV_ROWS_PER_STEP
        if ci == 0:
            halo = jnp.where(pl.program_id(1) == 0, 0.0, halo_ref[:, 0:HIST, :])
        else:
            halo = glu[lo - HIST:lo, :].reshape(1, HIST, CONV_CH)
        cur = glu[lo:lo + CONV_ROWS_PER_STEP, :].reshape(1, CONV_ROWS_PER_STEP, CONV_CH)
        conv_ref[lo:lo + CONV_ROWS_PER_STEP, :] = _conv_rows(cbuf_ref, halo, cur, w_ref, bdw_ref,
                                                             gng_ref, gnb_ref, gavg_ref)
        if ci == n_conv - 1:
            halo_ref[:, 0:HIST, :] = glu[m - HIST:m, :].reshape(1, HIST, CONV_CH)
            tail_ref[...] = glu[m - CONV_PAD_ROWS:m, :].reshape(1, CONV_PAD_ROWS, CONV_CH)

    q, k, v, _ = _project(x1_ref[...], sh1_ref, sc1_ref, win_ref, conv_branch)
    qt_ref[0] = q.T
    k16_ref[...] = k.astype(BF16)
    blocks_per_tile = m // MOBA_BLOCK
    means = [jnp.sum(k[n * MOBA_BLOCK:(n + 1) * MOBA_BLOCK, :], axis=0, keepdims=True)
             * (1.0 / MOBA_BLOCK) for n in range(blocks_per_tile)]
    tile = pl.program_id(1)
    for jj in range(kmean_ref.shape[1] // blocks_per_tile):
        @pl.when(tile == jj)
        def _():
            for n in range(blocks_per_tile):
                kmean_ref[0, jj * blocks_per_tile + n:jj * blocks_per_tile + n + 1, :] = means[n]
    kt = k.T
    vt = v.T
    for p in range(m // PAGE_SIZE):
        for h in range(N_HEADS):
            rs = slice(h * HEAD_DIM, (h + 1) * HEAD_DIM)
            cs = slice(p * PAGE_SIZE, (p + 1) * PAGE_SIZE)
            ktp_ref[0, p, h] = kt[rs, cs]
            vtp_ref[0, p, h] = vt[rs, cs]
            vt16_ref[0, p, h] = vt[rs, cs].astype(BF16)


def _const_spec(shape):
    return pl.BlockSpec(shape, lambda i, j: (0,) * len(shape), pipeline_mode=pl.Buffered(1))


def _stage_a_sample(x, ada, ln_g, ln_b, wg, wu, wd, win, *, alpha):
    n_seq, s, _ = x.shape
    n_tok = n_seq * s
    x_spec = pl.BlockSpec((n_seq, s, D_MODEL), lambda i, j: (0, 0, 0))
    a_spec = pl.BlockSpec((n_seq, 1, D_MODEL), lambda i, j: (0, 0, 0))
    flat_spec = pl.BlockSpec((n_tok, ATTN_WIDTH), lambda i, j: (0, 0))
    flat_shape = jax.ShapeDtypeStruct((n_tok, ATTN_WIDTH), F32)
    return pl.pallas_call(
        functools.partial(_stage_a_sample_kernel, alpha),
        grid=(1, 1),
        in_specs=[x_spec, a_spec, a_spec, a_spec, a_spec, a_spec,
                  _const_spec((1, D_MODEL)), _const_spec((1, D_MODEL)),
                  _const_spec((D_MODEL, D_FF)), _const_spec((D_MODEL, D_FF)),
                  _const_spec((D_FF, D_MODEL)), _const_spec((D_MODEL, W_IN_COLS))],
        out_specs=[x_spec, flat_spec, flat_spec, flat_spec, flat_spec],
        out_shape=[jax.ShapeDtypeStruct((n_seq, s, D_MODEL), F32)] + [flat_shape] * 4,
        compiler_params=pltpu.CompilerParams(
            dimension_semantics=("arbitrary", "arbitrary"), vmem_limit_bytes=VMEM_LIMIT),
        name="stage_a_sample",
    )(x, ada["sh0"], ada["sc0"], ada["g0"], ada["sh1"], ada["sc1"], ln_g, ln_b, wg, wu, wd, win)


def _stage_a_prompt(x, ada, ln_g, ln_b, wg, wu, wd, page_table_flat, pool_kt, cache_seqs, *, alpha, rows_blk):
    n_seq, s, _ = x.shape
    tiles_per_seq = s // rows_blk
    x_spec = pl.BlockSpec((1, rows_blk, D_MODEL), lambda i, j: (i, j, 0))
    a_spec = pl.BlockSpec((1, 1, D_MODEL), lambda i, j: (i, 0, 0))
    n_steps = n_seq * tiles_per_seq
    pages_per_seq = page_table_flat.shape[0] // cache_seqs
    assert cache_seqs == n_steps and pages_per_seq == STREAM_CHUNKS_PER_SEQ * STREAM_CHUNK_PAGES
    blocks_per_seq = pages_per_seq // PAGES_PER_BLOCK
    return pl.pallas_call(
        functools.partial(_stage_a_prompt_kernel, alpha),
        grid=(n_seq, tiles_per_seq),
        in_specs=[x_spec, a_spec, a_spec, a_spec,
                  _const_spec((1, D_MODEL)), _const_spec((1, D_MODEL)),
                  _const_spec((D_MODEL, D_FF)), _const_spec((D_MODEL, D_FF)), _const_spec((D_FF, D_MODEL)),
                  pl.BlockSpec(memory_space=pltpu.SMEM), pl.BlockSpec(memory_space=pl.ANY)],
        out_specs=[x_spec, pl.BlockSpec((1, N_HEADS, HEAD_DIM, blocks_per_seq),
                                        lambda i, j: (i * tiles_per_seq + j, 0, 0, 0))],
        out_shape=[jax.ShapeDtypeStruct((n_seq, s, D_MODEL), F32),
                   jax.ShapeDtypeStruct((cache_seqs, N_HEADS, HEAD_DIM, blocks_per_seq), F32)],
        scratch_shapes=[pltpu.VMEM((STREAM_SLOTS, STREAM_CHUNK_PAGES, N_HEADS, HEAD_DIM, PAGE_SIZE), F32),
                        pltpu.SemaphoreType.DMA((STREAM_SLOTS,))],
        compiler_params=pltpu.CompilerParams(
            dimension_semantics=("arbitrary", "arbitrary"), vmem_limit_bytes=VMEM_LIMIT),
        name="stage_a_prompt",
    )(x, ada["sh0"], ada["sc0"], ada["g0"], ln_g, ln_b, wg, wu, wd, page_table_flat, pool_kt)


def _stage_b(x1, ada, win, w_dw, b_dw, gn_g, gn_b, *, rows_blk):
    n_seq, s, _ = x1.shape
    tiles_per_seq = s // rows_blk
    assert rows_blk % MOBA_BLOCK == 0 and MOBA_BLOCK % PAGE_SIZE == 0
    assert rows_blk % CONV_ROWS_PER_STEP == 0 and CONV_ROWS_PER_STEP >= CONV_PAD_ROWS
    n_tok = n_seq * s
    x_spec = pl.BlockSpec((1, rows_blk, D_MODEL), lambda i, j: (i, j, 0))
    a_spec = pl.BlockSpec((1, 1, D_MODEL), lambda i, j: (i, 0, 0))
    flat_spec = pl.BlockSpec((rows_blk, ATTN_WIDTH), lambda i, j: (i * tiles_per_seq + j, 0))
    page_dims = (n_seq, s // PAGE_SIZE, N_HEADS, HEAD_DIM, PAGE_SIZE)
    page_spec = pl.BlockSpec((1, rows_blk // PAGE_SIZE, N_HEADS, HEAD_DIM, PAGE_SIZE),
                             lambda i, j: (i, j, 0, 0, 0))
    return pl.pallas_call(
        _stage_b_kernel,
        grid=(n_seq, tiles_per_seq),
        in_specs=[x_spec, a_spec, a_spec, _const_spec((D_MODEL, W_IN_COLS)),
                  _const_spec((CONV_K, CONV_CH)), _const_spec((1, CONV_CH)), _const_spec((1, CONV_CH)),
                  _const_spec((1, CONV_CH)), _const_spec((CONV_CH, CONV_CH))],
        out_specs=[pl.BlockSpec((1, ATTN_WIDTH, rows_blk), lambda i, j: (i, 0, j)),
                   flat_spec,
                   pl.BlockSpec((1, s // MOBA_BLOCK, ATTN_WIDTH), lambda i, j: (i, 0, 0)),
                   page_spec, page_spec, page_spec, flat_spec,
                   pl.BlockSpec((1, CONV_PAD_ROWS, CONV_CH), lambda i, j: (i, 0, 0))],
        out_shape=[jax.ShapeDtypeStruct((n_seq, ATTN_WIDTH, s), F32),
                   jax.ShapeDtypeStruct((n_tok, ATTN_WIDTH), BF16),
                   jax.ShapeDtypeStruct((n_seq, s // MOBA_BLOCK, ATTN_WIDTH), F32),
                   jax.ShapeDtypeStruct(page_dims, F32), jax.ShapeDtypeStruct(page_dims, F32),
                   jax.ShapeDtypeStruct(page_dims, BF16),
                   jax.ShapeDtypeStruct((n_tok, CONV_CH), F32),
                   jax.ShapeDtypeStruct((n_seq, CONV_PAD_ROWS, CONV_CH), F32)],
        scratch_shapes=[pltpu.VMEM((1, CONV_PAD_ROWS, CONV_CH), F32),
                        pltpu.VMEM((1, CONV_PAD_ROWS + CONV_ROWS_PER_STEP + SUBLANES, CONV_CH), F32)],
        compiler_params=pltpu.CompilerParams(
            dimension_semantics=("arbitrary", "arbitrary"), vmem_limit_bytes=VMEM_LIMIT),
        name="stage_b",
    )(x1, ada["sh1"], ada["sc1"], win, w_dw, b_dw.reshape(1, CONV_CH), gn_g.reshape(1, CONV_CH),
      gn_b.reshape(1, CONV_CH), _group_average_matrix(CONV_CH))


SAMPLE_GATHER_SLOTS = 3


def _stage_c_kernel(alpha, sample_cfg, *refs):
    (x1_ref, attn_ref, conv_ref, g1_ref, sh2_ref, sc2_ref, g2_ref, lng1_ref, lnb1_ref, lng2_ref, lnb2_ref,
     ba_ref, bc_ref, wo_ref, wg_ref, wu_ref, wd_ref) = refs[:17]
    if sample_cfg is not None:
        past_len, n_pages = sample_cfg
        y_ref, so_ref = refs[25:27]
        step = pl.program_id(0) * pl.num_programs(1) + pl.program_id(1)
        ring_step = _sample_attention_stream(step, pl.num_programs(0) * pl.num_programs(1), past_len,
                                             n_pages, *refs[17:25], so_ref, *refs[27:30])
        ff_chunks = STREAM_FF_CHUNKS
        last = len(ff_chunks) - 1
        assert len(ff_chunks) + 2 == N_HEADS
        before_up = lambda ci: ring_step(ci + 1)
        before_down = lambda ci: ring_step(N_HEADS - 1) if ci == last else None
        ring_step(0)
    else:
        y_ref = refs[17]
        before_up, before_down, ff_chunks = None, None, FF_CHUNKS
    nseq, rows, d = x1_ref.shape
    m = nseq * rows
    a = (attn_ref[...] * ba_ref[...]).astype(BF16)
    c = (conv_ref[...] * bc_ref[...]).astype(BF16)
    mix = (jnp.dot(a, wo_ref[0:ATTN_WIDTH, :], preferred_element_type=F32)
           + jnp.dot(c, wo_ref[ATTN_WIDTH:, :], preferred_element_type=F32)).reshape(nseq, rows, d)
    x2 = _layernorm(alpha * x1_ref[...] + g1_ref[...] * mix, lng1_ref[...], lnb1_ref[...])
    h2 = (x2 * (1.0 + sc2_ref[...]) + sh2_ref[...]).reshape(m, d).astype(BF16)
    y = _ffn(h2, wg_ref, wu_ref, wd_ref, ff_chunks, before_up, before_down).reshape(nseq, rows, d)
    y_ref[...] = _layernorm(alpha * x2 + g2_ref[...] * (0.5 * y), lng2_ref[...], lnb2_ref[...])


def _stage_c(x1, attn, conv, ada, ln_g1, ln_b1, ln_g2, ln_b2, beta_a, beta_c, wo, wg, wu, wd,
             *, alpha, nseq_blk, rows_blk, sample=None):
    n_seq, s, _ = x1.shape
    tiles_per_seq = s // rows_blk
    assert nseq_blk == 1 or (nseq_blk == n_seq and tiles_per_seq == 1)
    m_blk = nseq_blk * rows_blk
    x_spec = pl.BlockSpec((nseq_blk, rows_blk, D_MODEL), lambda i, j: (i, j, 0))
    a_spec = pl.BlockSpec((nseq_blk, 1, D_MODEL), lambda i, j: (i, 0, 0))
    flat = pl.BlockSpec((m_blk, ATTN_WIDTH), lambda i, j: (i * tiles_per_seq + j, 0))
    out_specs = [x_spec]
    out_shape = [jax.ShapeDtypeStruct((n_seq, s, D_MODEL), F32)]
    extra_in, extra_in_specs, scratch, cfg = [], [], [], None
    if sample is not None:
        seqs, _, _, t = sample["qt"].shape
        assert seqs == n_seq * tiles_per_seq
        n_k = t * MOBA_TOPK * PAGES_PER_BLOCK
        cfg = (sample["past_len"], sample["n_pages"])
        smem = pl.BlockSpec(memory_space=pltpu.SMEM)
        hbm = pl.BlockSpec(memory_space=pl.ANY)
        seq_spec = pl.BlockSpec((1, N_HEADS, HEAD_DIM, t), lambda i, j: (i * tiles_per_seq + j, 0, 0, 0))
        extra_in = [sample["idx_flat"], sample["page_table_flat"], sample["slopes"],
                    sample["qt"], sample["knt"], sample["vnt"], sample["pool_kt"], sample["pool_vt"]]
        extra_in_specs = [smem, smem, smem, seq_spec, seq_spec, seq_spec, hbm, hbm]
        out_specs.append(seq_spec)
        out_shape.append(jax.ShapeDtypeStruct((seqs, N_HEADS, HEAD_DIM, t), F32))
        scratch = [pltpu.VMEM((SAMPLE_GATHER_SLOTS, n_k, HEAD_DIM, PAGE_SIZE), F32),
                   pltpu.VMEM((SAMPLE_GATHER_SLOTS, n_k, HEAD_DIM, PAGE_SIZE), F32),
                   pltpu.SemaphoreType.DMA((SAMPLE_GATHER_SLOTS,))]
    res = pl.pallas_call(
        functools.partial(_stage_c_kernel, alpha, cfg),
        grid=(n_seq // nseq_blk, tiles_per_seq),
        in_specs=[x_spec, flat, flat, a_spec, a_spec, a_spec, a_spec,
                  _const_spec((1, D_MODEL)), _const_spec((1, D_MODEL)),
                  _const_spec((1, D_MODEL)), _const_spec((1, D_MODEL)),
                  _const_spec((1, ATTN_WIDTH)), _const_spec((1, CONV_CH)),
                  _const_spec((D_MODEL, D_MODEL)),
                  _const_spec((D_MODEL, D_FF)), _const_spec((D_MODEL, D_FF)),
                  _const_spec((D_FF, D_MODEL))] + extra_in_specs,
        out_specs=out_specs,
        out_shape=out_shape,
        scratch_shapes=scratch,
        compiler_params=pltpu.CompilerParams(
            dimension_semantics=("arbitrary", "arbitrary"), vmem_limit_bytes=VMEM_LIMIT),
        name="stage_c",
    )(x1, attn, conv, ada["g1"], ada["sh2"], ada["sc2"], ada["g2"],
      ln_g1, ln_b1, ln_g2, ln_b2, beta_a, beta_c, wo, wg, wu, wd, *extra_in)
    return res if sample is not None else res[0]


def _conv_rows(buf_ref, halo, cur, w_ref, bdw_ref, gng_ref, gnb_ref, gavg_ref):
    nb, ch, c = cur.shape
    pad = CONV_PAD_ROWS
    buf_ref[:, 0:pad - HIST, :] = jnp.zeros((nb, pad - HIST, c), F32)
    buf_ref[:, pad - HIST:pad, :] = halo
    buf_ref[:, pad:pad + ch, :] = cur
    buf_ref[:, pad + ch:, :] = jnp.zeros((nb, SUBLANES, c), F32)
    acc = None
    for r in range(SUBLANES):
        part = None
        for a in range((pad + SUBLANES) // SUBLANES):
            j = SUBLANES * a + r - (pad - HIST)
            if 0 <= j < CONV_K:
                term = buf_ref[:, SUBLANES * a:SUBLANES * a + ch + SUBLANES, :] * w_ref[j:j + 1, :]
                part = term if part is None else part + term
        shifted = part[:, r:r + ch, :]
        acc = shifted if acc is None else acc + shifted
    y = (acc + bdw_ref[...]).reshape(nb * ch, c)
    gavg = gavg_ref[...]

    def group_mean(t):
        hi, lo = _split_bf16(t)
        return (jnp.dot(hi, gavg, preferred_element_type=F32)
                + jnp.dot(lo, gavg, preferred_element_type=F32))

    mu = group_mean(y)
    dlt = y - mu
    var = group_mean(dlt * dlt)
    z = dlt * lax.rsqrt(var + LN_EPS) * gng_ref[...] + gnb_ref[...]
    return _silu(z)


def _conv_kernel(n_chunks, hist_ref, prev_ref, cur_ref, w_ref, bdw_ref, gng_ref, gnb_ref, gavg_ref,
                 o_ref, buf_ref):
    nb, ch, c = cur_ref.shape
    if n_chunks == 1:
        halo = hist_ref[...]
    else:
        halo = jnp.where(pl.program_id(1) == 0, hist_ref[...], prev_ref[:, ch - HIST:, :])
    out = _conv_rows(buf_ref, halo, cur_ref[...], w_ref, bdw_ref, gng_ref, gnb_ref, gavg_ref)
    o_ref[...] = out.reshape(nb, ch, c)


def _group_average_matrix(c):
    grp = np.arange(c) // (c // GN_GROUPS)
    return jnp.asarray((grp[:, None] == grp[None, :]).astype(np.float32) / (c // GN_GROUPS), BF16)


def _conv_branch(hist, glu3, w_dw, b_dw, gn_g, gn_b, *, nb, ch):
    n_seq, s, c = glu3.shape
    n_chunks = s // ch
    assert ch >= HIST or n_chunks == 1
    gavg = _group_average_matrix(c)
    cur_spec = pl.BlockSpec((nb, ch, c), lambda i, j: (i, j, 0))
    prev_spec = pl.BlockSpec((nb, ch, c), lambda i, j: (i, jnp.maximum(j - 1, 0), 0))
    return pl.pallas_call(
        functools.partial(_conv_kernel, n_chunks),
        grid=(n_seq // nb, n_chunks),
        in_specs=[pl.BlockSpec((nb, HIST, c), lambda i, j: (i, 0, 0)), prev_spec, cur_spec,
                  pl.BlockSpec((CONV_K, c), lambda i, j: (0, 0)),
                  pl.BlockSpec((1, c), lambda i, j: (0, 0)),
                  pl.BlockSpec((1, c), lambda i, j: (0, 0)),
                  pl.BlockSpec((1, c), lambda i, j: (0, 0)),
                  pl.BlockSpec((c, c), lambda i, j: (0, 0))],
        out_specs=cur_spec,
        out_shape=jax.ShapeDtypeStruct((n_seq, s, c), F32),
        scratch_shapes=[pltpu.VMEM((nb, CONV_PAD_ROWS + ch + SUBLANES, c), F32)],
        compiler_params=pltpu.CompilerParams(dimension_semantics=("arbitrary", "arbitrary")),
        name="conv_branch",
    )(hist, glu3, glu3, w_dw, b_dw.reshape(1, c), gn_g.reshape(1, c), gn_b.reshape(1, c), gavg)


MOBA_HEADS_PER_STEP = 8


def _moba_prompt_kernel(slopes_ref, qt_ref, k16_ref, vt16_ref, kmean_ref, o_ref,
                        q16_ref, bias_ref, krs_ref, s_ref, m_ref, l_ref, acc_ref):
    grp0 = pl.program_id(1)
    qb = pl.program_id(2)
    blk = MOBA_BLOCK
    nb = kmean_ref.shape[1]
    hps = MOBA_HEADS_PER_STEP
    row_head = lax.broadcasted_iota(jnp.int32, (LANES, blk), 0) // HEAD_DIM
    key_i = lax.broadcasted_iota(jnp.int32, (blk, blk), 0)
    qry_i = lax.broadcasted_iota(jnp.int32, (blk, blk), 1)
    key_f = key_i.astype(F32)
    blk_id = lax.broadcasted_iota(jnp.int32, (nb, blk), 0)
    past = blk_id < qb
    zero_row = jnp.zeros((1, blk), jnp.int32)

    def lanes_of(hh):
        g = hh // HEADS_PER_GROUP
        return slice(g * LANES, (g + 1) * LANES)

    def scores(hh, n):
        kb = k16_ref[0, pl.ds(pl.multiple_of(n * blk, blk), blk), lanes_of(hh)]
        return jnp.dot(kb, q16_ref[hh], preferred_element_type=F32) + krs_ref[hh]

    def pv(hh, n, p):
        p16 = p.astype(BF16)
        acc = None
        for i in range(PAGES_PER_BLOCK):
            vt = vt16_ref[0, n * PAGES_PER_BLOCK + i, hh]
            part = jnp.dot(vt, p16[i * PAGE_SIZE:(i + 1) * PAGE_SIZE, :], preferred_element_type=F32)
            acc = part if acc is None else acc + part
        return acc

    @pl.when(qb == 0)
    def _():
        for hh in range(hps):
            krs_ref[hh] = key_f * slopes_ref[grp0 * hps + hh]

    for hh in range(hps):
        sub = hh % HEADS_PER_GROUP
        qh = jnp.where(row_head == sub, qt_ref[0, lanes_of(hh), :], 0.0)
        q16_ref[hh] = (qh * (SCALE * LOG2E)).astype(BF16)

        gate = jnp.dot(kmean_ref[0, :, lanes_of(hh)], qh, preferred_element_type=F32, precision=HIGHEST)
        cnt = jnp.zeros((nb, blk), F32)
        for mth in range(nb):
            other = gate[mth:mth + 1, :]
            ahead = (other > gate) | ((other == gate) & (blk_id > mth))
            cnt = cnt + jnp.where(ahead & (mth < qb), 1.0, 0.0)
        bias_ref[hh] = jnp.where((cnt < MOBA_TOPK) & past, 0.0, NEG_INF)

        s = jnp.where(key_i <= qry_i, scores(hh, qb), NEG_INF)
        s_ref[hh, qb] = s
        m_ref[hh] = jnp.max(s, axis=0, keepdims=True)

    def block_const(hh, n):
        rel = ((n - qb) * blk + zero_row).astype(F32)
        return slopes_ref[grp0 * hps + hh] * rel + bias_ref[hh, pl.ds(n, 1), :]

    def pass1(n, carry):
        for hh in range(hps):
            s = scores(hh, n)
            s_ref[hh, n] = s
            m_ref[hh] = jnp.maximum(m_ref[hh], jnp.max(s, axis=0, keepdims=True) + block_const(hh, n))
        return carry

    lax.fori_loop(0, qb, pass1, 0)

    def block_out(hh, n, c):
        p = jnp.exp2(s_ref[hh, n] - (m_ref[hh] - c))
        return jnp.sum(p, axis=0, keepdims=True), pv(hh, n, p)

    for hh in range(hps):
        l_ref[hh], acc_ref[hh] = block_out(hh, qb, 0.0)

    def pass2(n, carry):
        for hh in range(hps):
            l_part, acc_part = block_out(hh, n, block_const(hh, n))
            l_ref[hh] = l_ref[hh] + l_part
            acc_ref[hh] = acc_ref[hh] + acc_part
        return carry

    lax.fori_loop(0, qb, pass2, 0)
    out_t = jnp.concatenate([acc_ref[hh] / l_ref[hh] for hh in range(hps)], axis=0)
    o_ref[0] = out_t.T


def _moba_prompt(qt3, k16, vt16, kmean, slopes):
    b, s, _ = k16.shape
    hps = MOBA_HEADS_PER_STEP
    width = hps * HEAD_DIM
    n_blocks = s // MOBA_BLOCK
    return pl.pallas_call(
        _moba_prompt_kernel,
        grid=(b, N_HEADS // hps, n_blocks),
        in_specs=[pl.BlockSpec(memory_space=pltpu.SMEM),
                  pl.BlockSpec((1, width, MOBA_BLOCK), lambda i, g, j: (i, g, j)),
                  pl.BlockSpec((1, s, width), lambda i, g, j: (i, 0, g)),
                  pl.BlockSpec((1, s // PAGE_SIZE, hps, HEAD_DIM, PAGE_SIZE),
                               lambda i, g, j: (i, 0, g, 0, 0)),
                  pl.BlockSpec((1, n_blocks, width), lambda i, g, j: (i, 0, g))],
        out_specs=pl.BlockSpec((1, MOBA_BLOCK, width), lambda i, g, j: (i, j, g)),
        out_shape=jax.ShapeDtypeStruct((b, s, ATTN_WIDTH), F32),
        scratch_shapes=[pltpu.VMEM((hps, LANES, MOBA_BLOCK), BF16),
                        pltpu.VMEM((hps, n_blocks, MOBA_BLOCK), F32),
                        pltpu.VMEM((hps, MOBA_BLOCK, MOBA_BLOCK), F32),
                        pltpu.VMEM((hps, n_blocks, MOBA_BLOCK, MOBA_BLOCK), F32),
                        pltpu.VMEM((hps, 1, MOBA_BLOCK), F32),
                        pltpu.VMEM((hps, 1, MOBA_BLOCK), F32),
                        pltpu.VMEM((hps, HEAD_DIM, MOBA_BLOCK), F32)],
        compiler_params=pltpu.CompilerParams(
            dimension_semantics=("arbitrary", "arbitrary", "arbitrary"), vmem_limit_bytes=VMEM_LIMIT),
        name="moba_prompt",
    )(slopes, qt3, k16, vt16, kmean)


TOPK_SEQS_PER_STEP = 4


def _topk_kernel(q_ref, kmt_ref, o_ref):
    ns, t, nbp = q_ref.shape[0], q_ref.shape[1], kmt_ref.shape[3]
    rows = ns * N_HEADS * t
    gates = []
    for s in range(ns):
        q = q_ref[s]
        for h in range(N_HEADS):
            gates.append(jnp.dot(q[:, h * HEAD_DIM:(h + 1) * HEAD_DIM], kmt_ref[s, h],
                                 preferred_element_type=F32, precision=HIGHEST))
    g = jnp.concatenate(gates, axis=0)
    lane = lax.broadcasted_iota(jnp.int32, (rows, nbp), 1).astype(F32)
    out_lane = lax.broadcasted_iota(jnp.int32, (rows, LANES), 1)
    res = jnp.zeros((rows, LANES), F32)
    for r in range(MOBA_TOPK):
        mx = jnp.max(g, axis=1, keepdims=True)
        idx = jnp.min(jnp.where(g == mx, lane, float(nbp)), axis=1, keepdims=True)
        res = jnp.where(out_lane == r, idx, res)
        g = jnp.where(lane == idx, -jnp.inf, g)
    o_ref[...] = res.astype(jnp.int32).reshape(ns, N_HEADS, t, LANES)


def _topk_blocks(q3, kmean_t):
    n_seq, t, _ = q3.shape
    nbp = kmean_t.shape[3]
    ns = TOPK_SEQS_PER_STEP
    return pl.pallas_call(
        _topk_kernel,
        grid=(n_seq // ns,),
        in_specs=[pl.BlockSpec((ns, t, ATTN_WIDTH), lambda b: (b, 0, 0)),
                  pl.BlockSpec((ns, N_HEADS, HEAD_DIM, nbp), lambda b: (b, 0, 0, 0))],
        out_specs=pl.BlockSpec((ns, N_HEADS, t, LANES), lambda b: (b, 0, 0, 0)),
        out_shape=jax.ShapeDtypeStruct((n_seq, N_HEADS, t, LANES), jnp.int32),
        name="topk_blocks",
    )(q3, kmean_t)


def _sample_attention_stream(step, n_steps, past_len, n_pages, idx_ref, pt_ref, slopes_ref,
                             qt_ref, knt_ref, vnt_ref, pool_k_ref, pool_v_ref, o_ref,
                             kbuf_ref, vbuf_ref, sem_ref):
    t = qt_ref.shape[3]
    slabs = MOBA_TOPK * PAGES_PER_BLOCK
    n_k = t * slabs

    def slab_copies(page, head, slot, n):
        return (pltpu.make_async_copy(pool_k_ref.at[page, head], kbuf_ref.at[slot, n], sem_ref.at[slot]),
                pltpu.make_async_copy(pool_v_ref.at[page, head], vbuf_ref.at[slot, n], sem_ref.at[slot]))

    def start_gather(seq, head, slot):
        for n in range(n_k):
            tq, rem = divmod(n, slabs)
            j, i = divmod(rem, PAGES_PER_BLOCK)
            blk_idx = idx_ref[((seq * N_HEADS + head) * t + tq) * MOBA_TOPK + j]
            page = pt_ref[seq * n_pages + blk_idx * PAGES_PER_BLOCK + i]
            for cp in slab_copies(page, head, slot, n):
                cp.start()

    n_slots = kbuf_ref.shape[0]
    ahead = n_slots - 1
    assert ahead < N_HEADS

    def ring_step(h):
        unit = step * N_HEADS + h
        slot = lax.rem(unit, n_slots)
        if h == 0:
            @pl.when(step == 0)
            def _():
                for a in range(ahead):
                    start_gather(0, a, a)
        nxt_slot = lax.rem(unit + ahead, n_slots)
        if h + ahead < N_HEADS:
            start_gather(step, h + ahead, nxt_slot)
        else:
            @pl.when(step + 1 < n_steps)
            def _():
                start_gather(step + 1, h + ahead - N_HEADS, nxt_slot)
        for n in range(n_k):
            for cp in slab_copies(0, 0, slot, n):
                cp.wait()

        slope = slopes_ref[h]
        qt = qt_ref[0, h] * SCALE
        knt = knt_ref[0, h]
        vnt = vnt_ref[0, h]
        srow = lax.broadcasted_iota(jnp.int32, (slabs, PAGE_SIZE), 0)
        lane = lax.broadcasted_iota(jnp.int32, (slabs, PAGE_SIZE), 1)
        own_pos = lax.broadcasted_iota(jnp.int32, (1, t), 1)
        out_lane = lax.broadcasted_iota(jnp.int32, (HEAD_DIM, t), 1)
        out = jnp.zeros((HEAD_DIM, t), F32)
        for tq in range(t):
            q_col = qt[:, tq:tq + 1]
            base = ((step * N_HEADS + h) * t + tq) * MOBA_TOPK
            s_rows = []
            blk_of_row = jnp.zeros((slabs, PAGE_SIZE), jnp.int32)
            for j in range(MOBA_TOPK):
                blk_of_row = jnp.where(srow // PAGES_PER_BLOCK == j, idx_ref[base + j], blk_of_row)
                for i in range(PAGES_PER_BLOCK):
                    kt = kbuf_ref[slot, (tq * MOBA_TOPK + j) * PAGES_PER_BLOCK + i]
                    s_rows.append(jnp.sum(kt * q_col, axis=0, keepdims=True))
            s_sel = jnp.concatenate(s_rows, axis=0)
            k_pos = blk_of_row * MOBA_BLOCK + (srow % PAGES_PER_BLOCK) * PAGE_SIZE + lane
            s_sel = s_sel - slope * ((past_len + tq) - k_pos).astype(F32)
            rel_own = tq - own_pos
            s_own = jnp.sum(knt * q_col, axis=0, keepdims=True)
            s_own = jnp.where(rel_own >= 0, s_own - slope * rel_own.astype(F32), NEG_INF)
            m = jnp.maximum(jnp.max(jnp.max(s_sel, axis=1, keepdims=True), axis=0, keepdims=True),
                            jnp.max(s_own, axis=1, keepdims=True))
            p_sel = jnp.exp(s_sel - m)
            p_own = jnp.exp(s_own - m)
            l = (jnp.sum(jnp.sum(p_sel, axis=1, keepdims=True), axis=0, keepdims=True)
                 + jnp.sum(p_own, axis=1, keepdims=True))
            acc = jnp.zeros((HEAD_DIM, PAGE_SIZE), F32)
            for r in range(slabs):
                acc = acc + vbuf_ref[slot, tq * slabs + r] * p_sel[r:r + 1, :]
            o_col = (jnp.sum(acc, axis=1, keepdims=True) + jnp.sum(vnt * p_own, axis=1, keepdims=True)) / l
            out = jnp.where(out_lane == tq, o_col, out)
        o_ref[0, h] = out

    return ring_step


def _split_ada(ada_rows):
    a = ada_rows.reshape(ada_rows.shape[0], 3, 3, 1, D_MODEL)
    return {"sh0": a[:, 0, 0], "sc0": a[:, 0, 1], "g0": a[:, 0, 2],
            "sh1": a[:, 1, 0], "sc1": a[:, 1, 1], "g1": a[:, 1, 2],
            "sh2": a[:, 2, 0], "sc2": a[:, 2, 1], "g2": a[:, 2, 2]}


def _heads(t2, n_seq, s):
    return t2.reshape(n_seq, s, N_HEADS, HEAD_DIM)


def kernel(x_prompt, x_sample, cache_k, cache_v, state_conv, page_table, c_prompt, c_sample,
           w_ada, b_ada, ffn1_wg, ffn1_wu, ffn1_wd, w_in, w_dw, b_dw, gn_g, gn_b,
           beta_attn, beta_conv, w_out, ffn2_wg, ffn2_wu, ffn2_wd, ln_g, ln_b):
    depth = w_ada.shape[0]
    alpha = (2.0 * depth) ** 0.25
    batch, seq, _ = x_prompt.shape
    dec_batch, dec_seq, _ = x_sample.shape
    n_pages = page_table.shape[1]
    past_len = n_pages * PAGE_SIZE
    assert past_len % MOBA_BLOCK == 0 and dec_seq <= MOBA_BLOCK
    alibi = 2.0 ** (-8.0 * np.arange(1, N_HEADS + 1) / N_HEADS)
    slopes = jnp.asarray(alibi, F32)
    slopes_log2 = jnp.asarray(alibi * LOG2E, F32)
    page_table_flat = page_table.reshape(-1)

    y_p, y_s = x_prompt, x_sample
    outs = {name: [] for name in ("kp", "vp", "cp", "ks", "vs", "cs")}
    for l in range(depth):
        ada = _ada(jnp.concatenate([c_prompt, c_sample], axis=0), w_ada[l], b_ada[l])
        ada_p, ada_s = _split_ada(ada[:batch]), _split_ada(ada[batch:])
        lng = [ln_g[l, i].reshape(1, D_MODEL) for i in range(3)]
        lnb = [ln_b[l, i].reshape(1, D_MODEL) for i in range(3)]
        w1 = (ffn1_wg[l].astype(BF16), ffn1_wu[l].astype(BF16), ffn1_wd[l].astype(BF16))
        w2 = (ffn2_wg[l].astype(BF16), ffn2_wu[l].astype(BF16), ffn2_wd[l].astype(BF16))
        win16, wo16 = w_in[l].astype(BF16), w_out[l].astype(BF16)
        beta_a = beta_attn[l].reshape(1, ATTN_WIDTH)
        beta_c = beta_conv[l].reshape(1, CONV_CH)

        tile_p = dict(alpha=alpha, nseq_blk=1, rows_blk=PROMPT_ROWS_PER_STEP)
        tile_s = dict(alpha=alpha, nseq_blk=dec_batch, rows_blk=dec_seq)
        pool_kt = jnp.swapaxes(cache_k[l], -1, -2)
        pool_vt = jnp.swapaxes(cache_v[l], -1, -2)

        x1_s, q, k, v, glu_s = _stage_a_sample(y_s, ada_s, lng[0], lnb[0], *w1, win16, alpha=alpha)
        q4, k4, v4 = (_heads(t2, dec_batch, dec_seq) for t2 in (q, k, v))

        x1, kmean_t = _stage_a_prompt(y_p, ada_p, lng[0], lnb[0], *w1, page_table_flat, pool_kt, dec_batch,
                                      alpha=alpha, rows_blk=PROMPT_ROWS_PER_STEP)
        qt, k16, kmean, ktp, vtp, vt16, conv, glu_tail = _stage_b(
            x1, ada_p, win16, w_dw[l], b_dw[l], gn_g[l], gn_b[l], rows_blk=PROMPT_ROWS_PER_STEP)
        idx = _topk_blocks(q.reshape(dec_batch, dec_seq, ATTN_WIDTH), kmean_t)[..., :MOBA_TOPK]
        attn = _moba_prompt(qt, k16.reshape(batch, seq, ATTN_WIDTH), vt16, kmean, slopes_log2)
        to_t = lambda a4: a4.transpose(0, 2, 3, 1)
        sample = dict(qt=to_t(q4), knt=to_t(k4), vnt=to_t(v4), pool_kt=pool_kt, pool_vt=pool_vt,
                      idx_flat=idx.reshape(-1), page_table_flat=page_table_flat, slopes=slopes,
                      past_len=past_len, n_pages=n_pages)
        y_p, attn_t = _stage_c(x1, attn.reshape(batch * seq, ATTN_WIDTH), conv, ada_p,
                               lng[1], lnb[1], lng[2], lnb[2], beta_a, beta_c, wo16, *w2,
                               sample=sample, **tile_p)
        outs["kp"].append(jnp.swapaxes(ktp, -1, -2))
        outs["vp"].append(jnp.swapaxes(vtp, -1, -2))
        outs["cp"].append(glu_tail[:, CONV_PAD_ROWS - HIST:, :])

        x1, glu = x1_s, glu_s
        attn = attn_t.transpose(0, 3, 1, 2).reshape(dec_batch * dec_seq, ATTN_WIDTH)
        glu3 = glu.reshape(dec_batch, dec_seq, CONV_CH)
        hist = state_conv[l]
        conv = _conv_branch(hist, glu3, w_dw[l], b_dw[l], gn_g[l], gn_b[l], nb=dec_batch, ch=dec_seq)
        y_s = _stage_c(x1, attn, conv.reshape(dec_batch * dec_seq, CONV_CH),
                       ada_s, lng[1], lnb[1], lng[2], lnb[2], beta_a, beta_c, wo16, *w2, **tile_s)
        outs["ks"].append(k4.transpose(0, 2, 1, 3))
        outs["vs"].append(v4.transpose(0, 2, 1, 3))
        outs["cs"].append(jnp.concatenate([hist, glu3], axis=1)[:, dec_seq:, :])

    stack = lambda name: jnp.stack(outs[name], 0)
    return (y_p, y_s, stack("kp"), stack("vp"), stack("cp"), stack("ks"), stack("vs"), stack("cs"))
```

```python
import functools

import numpy as np
import jax
import jax.numpy as jnp
from jax import lax
from jax.experimental import pallas as pl
from jax.experimental.pallas import tpu as pltpu

F32 = jnp.float32
BF16 = jnp.bfloat16
HIGHEST = lax.Precision.HIGHEST

D_MODEL = 1024
D_FF = 2816
N_HEADS = 8
HEAD_DIM = 64
ATTN_WIDTH = N_HEADS * HEAD_DIM
CONV_CH = 512
W_IN_COLS = 3 * ATTN_WIDTH + 2 * CONV_CH
CONV_K = 31
HIST = CONV_K - 1
GN_GROUPS = 8
MOBA_BLOCK = 256
MOBA_TOPK = 3
PAGE_SIZE = 128
PAGES_PER_BLOCK = MOBA_BLOCK // PAGE_SIZE
LN_EPS = 1e-5
NEG_INF = -1e30
SCALE = HEAD_DIM ** -0.5

LANES = 128
SUBLANES = 8
HEADS_PER_GROUP = LANES // HEAD_DIM
CONV_PAD_ROWS = 32
LOG2E = 1.4426950408889634
VMEM_LIMIT = 60 * 1024 * 1024

FF_CHUNKS = ((0, 1024), (1024, 2048), (2048, D_FF))
PROMPT_ROWS_PER_STEP = 512
CONV_ROWS_PER_STEP = 256


def _silu(x):
    return x / (1.0 + jnp.exp(-x))


def _sigmoid(x):
    return 1.0 / (1.0 + jnp.exp(-x))


def _layernorm(t, g, b):
    mu = jnp.mean(t, axis=-1, keepdims=True)
    d = t - mu
    var = jnp.mean(d * d, axis=-1, keepdims=True)
    return d * lax.rsqrt(var + LN_EPS) * g + b


def _ffn(h, wg_ref, wu_ref, wd_ref, chunks=FF_CHUNKS, before_up=None, before_down=None):
    acc = None
    for ci, (lo, hi) in enumerate(chunks):
        if before_up is not None:
            before_up(ci)
        g = jnp.dot(h, wg_ref[:, lo:hi], preferred_element_type=F32)
        u = jnp.dot(h, wu_ref[:, lo:hi], preferred_element_type=F32)
        a = (_silu(g) * u).astype(BF16)
        if before_down is not None:
            before_down(ci)
        y = jnp.dot(a, wd_ref[lo:hi, :], preferred_element_type=F32)
        acc = y if acc is None else acc + y
    return acc


def _split_bf16(t):
    hi = t.astype(BF16)
    return hi, (t - hi.astype(F32)).astype(BF16)


def _ada_kernel(c_ref, w_ref, b_ref, o_ref):
    a_hi, a_lo = _split_bf16(_silu(c_ref[...]))
    w_hi, w_lo = _split_bf16(w_ref[...])
    o_ref[...] = (jnp.dot(a_hi, w_hi, preferred_element_type=F32)
                  + jnp.dot(a_lo, w_hi, preferred_element_type=F32)
                  + jnp.dot(a_hi, w_lo, preferred_element_type=F32)) + b_ref[...]


def _ada(c_all, w_ada, b_ada):
    n = c_all.shape[0]
    tn = 1024
    return pl.pallas_call(
        _ada_kernel,
        grid=(w_ada.shape[1] // tn,),
        in_specs=[
            pl.BlockSpec((n, D_MODEL), lambda j: (0, 0)),
            pl.BlockSpec((D_MODEL, tn), lambda j: (0, j)),
            pl.BlockSpec((1, tn), lambda j: (0, j)),
        ],
        out_specs=pl.BlockSpec((n, tn), lambda j: (0, j)),
        out_shape=jax.ShapeDtypeStruct((n, w_ada.shape[1]), F32),
        name="ada",
    )(c_all, w_ada, b_ada.reshape(1, -1))


STREAM_CHUNK_PAGES = 16
STREAM_SLOTS = 3
STREAM_FF_CHUNKS = ((0, 512), (512, 1024), (1024, 1536), (1536, 2048), (2048, 2560), (2560, D_FF))
W_IN_PARTS = ((0, ATTN_WIDTH), (ATTN_WIDTH, 2 * ATTN_WIDTH), (2 * ATTN_WIDTH, 3 * ATTN_WIDTH),
              (3 * ATTN_WIDTH, W_IN_COLS))
STREAM_HOOKS = (("up", 0), ("up", 1), ("down", 1), ("up", 2), ("up", 3), ("up", 4), ("down", 4), ("up", 5))
STREAM_CHUNKS_PER_SEQ = len(STREAM_HOOKS)


def _block_mean_stream(step, n_steps, first, count, pt_ref, pool_ref, kmt_ref, ring_ref, sem_ref):
    ahead = STREAM_SLOTS - 1
    blocks_per_chunk = STREAM_CHUNK_PAGES // PAGES_PER_BLOCK
    n_cols = kmt_ref.shape[3]
    assert count * blocks_per_chunk == n_cols and ahead <= count

    def page_copy(page, slot, i):
        return pltpu.make_async_copy(pool_ref.at[page], ring_ref.at[slot, i], sem_ref.at[slot])

    def start_chunk(for_step, k, slot):
        g = for_step * STREAM_CHUNKS_PER_SEQ + first + k
        for i in range(STREAM_CHUNK_PAGES):
            page_copy(pt_ref[g * STREAM_CHUNK_PAGES + i], slot, i).start()

    def ring_step(k):
        seq_no = step * count + k
        slot = lax.rem(seq_no, STREAM_SLOTS)
        if k == 0:
            @pl.when(step == 0)
            def _():
                for a in range(ahead):
                    start_chunk(0, a, a)
        nxt_slot = lax.rem(seq_no + ahead, STREAM_SLOTS)
        if k + ahead < count:
            start_chunk(step, k + ahead, nxt_slot)
        else:
            @pl.when(step + 1 < n_steps)
            def _():
                start_chunk(step + 1, k + ahead - count, nxt_slot)
        for i in range(STREAM_CHUNK_PAGES):
            page_copy(0, slot, i).wait()
        col_id = lax.broadcasted_iota(jnp.int32, (HEAD_DIM, n_cols), 1)
        cols = slice(k * blocks_per_chunk, (k + 1) * blocks_per_chunk)
        for h in range(N_HEADS):
            acc = jnp.zeros((HEAD_DIM, n_cols), F32)
            for jb in range(blocks_per_chunk):
                tot = ring_ref[slot, jb * PAGES_PER_BLOCK, h]
                for i in range(1, PAGES_PER_BLOCK):
                    tot = tot + ring_ref[slot, jb * PAGES_PER_BLOCK + i, h]
                mean = jnp.sum(tot, axis=1, keepdims=True) * (1.0 / MOBA_BLOCK)
                acc = jnp.where(col_id == k * blocks_per_chunk + jb, mean, acc)
            kmt_ref[0, h, :, cols] = acc[:, cols]

    return ring_step


def _first_ffn(alpha, x_ref, sh0_ref, sc0_ref, g0_ref, lng_ref, lnb_ref, wg_ref, wu_ref, wd_ref,
               ff_chunks, before_up=None, before_down=None):
    nseq, rows, d = x_ref.shape
    x = x_ref[...]
    h0 = (x * (1.0 + sc0_ref[...]) + sh0_ref[...]).reshape(nseq * rows, d).astype(BF16)
    y = _ffn(h0, wg_ref, wu_ref, wd_ref, ff_chunks, before_up, before_down).reshape(nseq, rows, d)
    return _layernorm(alpha * x + g0_ref[...] * (0.5 * y), lng_ref[...], lnb_ref[...])


def _project(x1, sh1_ref, sc1_ref, win_ref, on_glu=None):
    nseq, rows, d = x1.shape
    h1 = (x1 * (1.0 + sc1_ref[...]) + sh1_ref[...]).reshape(nseq * rows, d).astype(BF16)
    part = lambda lo, hi: jnp.dot(h1, win_ref[:, lo:hi], preferred_element_type=F32)
    ug = part(*W_IN_PARTS[3])
    glu = ug[:, 0:CONV_CH] * _sigmoid(ug[:, CONV_CH:])
    qkv = []
    for pi, (lo, hi) in enumerate(W_IN_PARTS[:3]):
        if on_glu is not None:
            on_glu(glu, pi)
        qkv.append(part(lo, hi))
    q, k, v = qkv
    return q, k, v, glu


def _stage_a_sample_kernel(alpha, x_ref, sh0_ref, sc0_ref, g0_ref, sh1_ref, sc1_ref, lng_ref, lnb_ref,
                           wg_ref, wu_ref, wd_ref, win_ref, x1_ref, q_ref, k_ref, v_ref, glu_ref):
    x1 = _first_ffn(alpha, x_ref, sh0_ref, sc0_ref, g0_ref, lng_ref, lnb_ref, wg_ref, wu_ref, wd_ref,
                    FF_CHUNKS)
    x1_ref[...] = x1
    q_ref[...], k_ref[...], v_ref[...], glu_ref[...] = _project(x1, sh1_ref, sc1_ref, win_ref)


def _stage_a_prompt_kernel(alpha, x_ref, sh0_ref, sc0_ref, g0_ref, lng_ref, lnb_ref, wg_ref, wu_ref, wd_ref,
                           pt_ref, pool_ref, x1_ref, kmt_ref, ring_ref, sem_ref):
    step = pl.program_id(0) * pl.num_programs(1) + pl.program_id(1)
    ring_step = _block_mean_stream(step, pl.num_programs(0) * pl.num_programs(1), 0, STREAM_CHUNKS_PER_SEQ,
                                   pt_ref, pool_ref, kmt_ref, ring_ref, sem_ref)

    def hook(kind):
        return lambda ci: ring_step(STREAM_HOOKS.index((kind, ci))) if (kind, ci) in STREAM_HOOKS else None

    x1_ref[...] = _first_ffn(alpha, x_ref, sh0_ref, sc0_ref, g0_ref, lng_ref, lnb_ref, wg_ref, wu_ref, wd_ref,
                             STREAM_FF_CHUNKS, hook("up"), hook("down"))


def _stage_b_kernel(x1_ref, sh1_ref, sc1_ref, win_ref, w_ref, bdw_ref, gng_ref, gnb_ref, gavg_ref,
                    qt_ref, k16_ref, kmean_ref, ktp_ref, vtp_ref, vt16_ref, conv_ref, tail_ref,
                    halo_ref, cbuf_ref):
    m = x1_ref.shape[1]

    n_conv = m // CONV_ROWS_PER_STEP

    def conv_branch(glu, ci):
        if ci >= n_conv:
            return
        lo = ci * CONV_ROWS_PER_STEP
        if ci == 0:
            halo = jnp.where(pl.program_id(1) == 0, 0.0, halo_ref[:, 0:HIST, :])
        else:
            halo = glu[lo - HIST:lo, :].reshape(1, HIST, CONV_CH)
        cur = glu[lo:lo + CONV_ROWS_PER_STEP, :].reshape(1, CONV_ROWS_PER_STEP, CONV_CH)
        conv_ref[lo:lo + CONV_ROWS_PER_STEP, :] = _conv_rows(cbuf_ref, halo, cur, w_ref, bdw_ref,
                                                             gng_ref, gnb_ref, gavg_ref)
        if ci == n_conv - 1:
            halo_ref[:, 0:HIST, :] = glu[m - HIST:m, :].reshape(1, HIST, CONV_CH)
            tail_ref[...] = glu[m - CONV_PAD_ROWS:m, :].reshape(1, CONV_PAD_ROWS, CONV_CH)

    q, k, v, _ = _project(x1_ref[...], sh1_ref, sc1_ref, win_ref, conv_branch)
    qt_ref[0] = q.T
    k16_ref[...] = k.astype(BF16)
    blocks_per_tile = m // MOBA_BLOCK
    means = [jnp.sum(k[n * MOBA_BLOCK:(n + 1) * MOBA_BLOCK, :], axis=0, keepdims=True)
             * (1.0 / MOBA_BLOCK) for n in range(blocks_per_tile)]
    tile = pl.program_id(1)
    for jj in range(kmean_ref.shape[1] // blocks_per_tile):
        @pl.when(tile == jj)
        def _():
            for n in range(blocks_per_tile):
                kmean_ref[0, jj * blocks_per_tile + n:jj * blocks_per_tile + n + 1, :] = means[n]
    kt = k.T
    vt = v.T
    for p in range(m // PAGE_SIZE):
        for h in range(N_HEADS):
            rs = slice(h * HEAD_DIM, (h + 1) * HEAD_DIM)
            cs = slice(p * PAGE_SIZE, (p + 1) * PAGE_SIZE)
            ktp_ref[0, p, h] = kt[rs, cs]
            vtp_ref[0, p, h] = vt[rs, cs]
            vt16_ref[0, p, h] = vt[rs, cs].astype(BF16)


def _const_spec(shape):
    return pl.BlockSpec(shape, lambda i, j: (0,) * len(shape), pipeline_mode=pl.Buffered(1))


def _stage_a_sample(x, ada, ln_g, ln_b, wg, wu, wd, win, *, alpha):
    n_seq, s, _ = x.shape
    n_tok = n_seq * s
    x_spec = pl.BlockSpec((n_seq, s, D_MODEL), lambda i, j: (0, 0, 0))
    a_spec = pl.BlockSpec((n_seq, 1, D_MODEL), lambda i, j: (0, 0, 0))
    flat_spec = pl.BlockSpec((n_tok, ATTN_WIDTH), lambda i, j: (0, 0))
    flat_shape = jax.ShapeDtypeStruct((n_tok, ATTN_WIDTH), F32)
    return pl.pallas_call(
        functools.partial(_stage_a_sample_kernel, alpha),
        grid=(1, 1),
        in_specs=[x_spec, a_spec, a_spec, a_spec, a_spec, a_spec,
                  _const_spec((1, D_MODEL)), _const_spec((1, D_MODEL)),
                  _const_spec((D_MODEL, D_FF)), _const_spec((D_MODEL, D_FF)),
                  _const_spec((D_FF, D_MODEL)), _const_spec((D_MODEL, W_IN_COLS))],
        out_specs=[x_spec, flat_spec, flat_spec, flat_spec, flat_spec],
        out_shape=[jax.ShapeDtypeStruct((n_seq, s, D_MODEL), F32)] + [flat_shape] * 4,
        compiler_params=pltpu.CompilerParams(
            dimension_semantics=("arbitrary", "arbitrary"), vmem_limit_bytes=VMEM_LIMIT),
        name="stage_a_sample",
    )(x, ada["sh0"], ada["sc0"], ada["g0"], ada["sh1"], ada["sc1"], ln_g, ln_b, wg, wu, wd, win)


def _stage_a_prompt(x, ada, ln_g, ln_b, wg, wu, wd, page_table_flat, pool_kt, cache_seqs, *, alpha, rows_blk):
    n_seq, s, _ = x.shape
    tiles_per_seq = s // rows_blk
    x_spec = pl.BlockSpec((1, rows_blk, D_MODEL), lambda i, j: (i, j, 0))
    a_spec = pl.BlockSpec((1, 1, D_MODEL), lambda i, j: (i, 0, 0))
    n_steps = n_seq * tiles_per_seq
    pages_per_seq = page_table_flat.shape[0] // cache_seqs
    assert cache_seqs == n_steps and pages_per_seq == STREAM_CHUNKS_PER_SEQ * STREAM_CHUNK_PAGES
    blocks_per_seq = pages_per_seq // PAGES_PER_BLOCK
    return pl.pallas_call(
        functools.partial(_stage_a_prompt_kernel, alpha),
        grid=(n_seq, tiles_per_seq),
        in_specs=[x_spec, a_spec, a_spec, a_spec,
                  _const_spec((1, D_MODEL)), _const_spec((1, D_MODEL)),
                  _const_spec((D_MODEL, D_FF)), _const_spec((D_MODEL, D_FF)), _const_spec((D_FF, D_MODEL)),
                  pl.BlockSpec(memory_space=pltpu.SMEM), pl.BlockSpec(memory_space=pl.ANY)],
        out_specs=[x_spec, pl.BlockSpec((1, N_HEADS, HEAD_DIM, blocks_per_seq),
                                        lambda i, j: (i * tiles_per_seq + j, 0, 0, 0))],
        out_shape=[jax.ShapeDtypeStruct((n_seq, s, D_MODEL), F32),
                   jax.ShapeDtypeStruct((cache_seqs, N_HEADS, HEAD_DIM, blocks_per_seq), F32)],
        scratch_shapes=[pltpu.VMEM((STREAM_SLOTS, STREAM_CHUNK_PAGES, N_HEADS, HEAD_DIM, PAGE_SIZE), F32),
                        pltpu.SemaphoreType.DMA((STREAM_SLOTS,))],
        compiler_params=pltpu.CompilerParams(
            dimension_semantics=("arbitrary", "arbitrary"), vmem_limit_bytes=VMEM_LIMIT),
        name="stage_a_prompt",
    )(x, ada["sh0"], ada["sc0"], ada["g0"], ln_g, ln_b, wg, wu, wd, page_table_flat, pool_kt)


def _stage_b(x1, ada, win, w_dw, b_dw, gn_g, gn_b, *, rows_blk):
    n_seq, s, _ = x1.shape
    tiles_per_seq = s // rows_blk
    assert rows_blk % MOBA_BLOCK == 0 and MOBA_BLOCK % PAGE_SIZE == 0
    assert rows_blk % CONV_ROWS_PER_STEP == 0 and CONV_ROWS_PER_STEP >= CONV_PAD_ROWS
    n_tok = n_seq * s
    x_spec = pl.BlockSpec((1, rows_blk, D_MODEL), lambda i, j: (i, j, 0))
    a_spec = pl.BlockSpec((1, 1, D_MODEL), lambda i, j: (i, 0, 0))
    flat_spec = pl.BlockSpec((rows_blk, ATTN_WIDTH), lambda i, j: (i * tiles_per_seq + j, 0))
    page_dims = (n_seq, s // PAGE_SIZE, N_HEADS, HEAD_DIM, PAGE_SIZE)
    page_spec = pl.BlockSpec((1, rows_blk // PAGE_SIZE, N_HEADS, HEAD_DIM, PAGE_SIZE),
                             lambda i, j: (i, j, 0, 0, 0))
    return pl.pallas_call(
        _stage_b_kernel,
        grid=(n_seq, tiles_per_seq),
        in_specs=[x_spec, a_spec, a_spec, _const_spec((D_MODEL, W_IN_COLS)),
                  _const_spec((CONV_K, CONV_CH)), _const_spec((1, CONV_CH)), _const_spec((1, CONV_CH)),
                  _const_spec((1, CONV_CH)), _const_spec((CONV_CH, CONV_CH))],
        out_specs=[pl.BlockSpec((1, ATTN_WIDTH, rows_blk), lambda i, j: (i, 0, j)),
                   flat_spec,
                   pl.BlockSpec((1, s // MOBA_BLOCK, ATTN_WIDTH), lambda i, j: (i, 0, 0)),
                   page_spec, page_spec, page_spec, flat_spec,
                   pl.BlockSpec((1, CONV_PAD_ROWS, CONV_CH), lambda i, j: (i, 0, 0))],
        out_shape=[jax.ShapeDtypeStruct((n_seq, ATTN_WIDTH, s), F32),
                   jax.ShapeDtypeStruct((n_tok, ATTN_WIDTH), BF16),
                   jax.ShapeDtypeStruct((n_seq, s // MOBA_BLOCK, ATTN_WIDTH), F32),
                   jax.ShapeDtypeStruct(page_dims, F32), jax.ShapeDtypeStruct(page_dims, F32),
                   jax.ShapeDtypeStruct(page_dims, BF16),
                   jax.ShapeDtypeStruct((n_tok, CONV_CH), F32),
                   jax.ShapeDtypeStruct((n_seq, CONV_PAD_ROWS, CONV_CH), F32)],
        scratch_shapes=[pltpu.VMEM((1, CONV_PAD_ROWS, CONV_CH), F32),
                        pltpu.VMEM((1, CONV_PAD_ROWS + CONV_ROWS_PER_STEP + SUBLANES, CONV_CH), F32)],
        compiler_params=pltpu.CompilerParams(
            dimension_semantics=("arbitrary", "arbitrary"), vmem_limit_bytes=VMEM_LIMIT),
        name="stage_b",
    )(x1, ada["sh1"], ada["sc1"], win, w_dw, b_dw.reshape(1, CONV_CH), gn_g.reshape(1, CONV_CH),
      gn_b.reshape(1, CONV_CH), _group_average_matrix(CONV_CH))


SAMPLE_GATHER_SLOTS = 3


def _stage_c_kernel(alpha, sample_cfg, *refs):
    (x1_ref, attn_ref, conv_ref, g1_ref, sh2_ref, sc2_ref, g2_ref, lng1_ref, lnb1_ref, lng2_ref, lnb2_ref,
     ba_ref, bc_ref, wo_ref, wg_ref, wu_ref, wd_ref) = refs[:17]
    if sample_cfg is not None:
        past_len, n_pages = sample_cfg
        y_ref, so_ref = refs[25:27]
        step = pl.program_id(0) * pl.num_programs(1) + pl.program_id(1)
        ring_step = _sample_attention_stream(step, pl.num_programs(0) * pl.num_programs(1), past_len,
                                             n_pages, *refs[17:25], so_ref, *refs[27:30])
        ff_chunks = STREAM_FF_CHUNKS
        last = len(ff_chunks) - 1
        assert len(ff_chunks) + 2 == N_HEADS
        before_up = lambda ci: ring_step(ci + 1)
        before_down = lambda ci: ring_step(N_HEADS - 1) if ci == last else None
        ring_step(0)
    else:
        y_ref = refs[17]
        before_up, before_down, ff_chunks = None, None, FF_CHUNKS
    nseq, rows, d = x1_ref.shape
    m = nseq * rows
    a = (attn_ref[...] * ba_ref[...]).astype(BF16)
    c = (conv_ref[...] * bc_ref[...]).astype(BF16)
    mix = (jnp.dot(a, wo_ref[0:ATTN_WIDTH, :], preferred_element_type=F32)
           + jnp.dot(c, wo_ref[ATTN_WIDTH:, :], preferred_element_type=F32)).reshape(nseq, rows, d)
    x2 = _layernorm(alpha * x1_ref[...] + g1_ref[...] * mix, lng1_ref[...], lnb1_ref[...])
    h2 = (x2 * (1.0 + sc2_ref[...]) + sh2_ref[...]).reshape(m, d).astype(BF16)
    y = _ffn(h2, wg_ref, wu_ref, wd_ref, ff_chunks, before_up, before_down).reshape(nseq, rows, d)
    y_ref[...] = _layernorm(alpha * x2 + g2_ref[...] * (0.5 * y), lng2_ref[...], lnb2_ref[...])


def _stage_c(x1, attn, conv, ada, ln_g1, ln_b1, ln_g2, ln_b2, beta_a, beta_c, wo, wg, wu, wd,
             *, alpha, nseq_blk, rows_blk, sample=None):
    n_seq, s, _ = x1.shape
    tiles_per_seq = s // rows_blk
    assert nseq_blk == 1 or (nseq_blk == n_seq and tiles_per_seq == 1)
    m_blk = nseq_blk * rows_blk
    x_spec = pl.BlockSpec((nseq_blk, rows_blk, D_MODEL), lambda i, j: (i, j, 0))
    a_spec = pl.BlockSpec((nseq_blk, 1, D_MODEL), lambda i, j: (i, 0, 0))
    flat = pl.BlockSpec((m_blk, ATTN_WIDTH), lambda i, j: (i * tiles_per_seq + j, 0))
    out_specs = [x_spec]
    out_shape = [jax.ShapeDtypeStruct((n_seq, s, D_MODEL), F32)]
    extra_in, extra_in_specs, scratch, cfg = [], [], [], None
    if sample is not None:
        seqs, _, _, t = sample["qt"].shape
        assert seqs == n_seq * tiles_per_seq
        n_k = t * MOBA_TOPK * PAGES_PER_BLOCK
        cfg = (sample["past_len"], sample["n_pages"])
        smem = pl.BlockSpec(memory_space=pltpu.SMEM)
        hbm = pl.BlockSpec(memory_space=pl.ANY)
        seq_spec = pl.BlockSpec((1, N_HEADS, HEAD_DIM, t), lambda i, j: (i * tiles_per_seq + j, 0, 0, 0))
        extra_in = [sample["idx_flat"], sample["page_table_flat"], sample["slopes"],
                    sample["qt"], sample["knt"], sample["vnt"], sample["pool_kt"], sample["pool_vt"]]
        extra_in_specs = [smem, smem, smem, seq_spec, seq_spec, seq_spec, hbm, hbm]
        out_specs.append(seq_spec)
        out_shape.append(jax.ShapeDtypeStruct((seqs, N_HEADS, HEAD_DIM, t), F32))
        scratch = [pltpu.VMEM((SAMPLE_GATHER_SLOTS, n_k, HEAD_DIM, PAGE_SIZE), F32),
                   pltpu.VMEM((SAMPLE_GATHER_SLOTS, n_k, HEAD_DIM, PAGE_SIZE), F32),
                   pltpu.SemaphoreType.DMA((SAMPLE_GATHER_SLOTS,))]
    res = pl.pallas_call(
        functools.partial(_stage_c_kernel, alpha, cfg),
        grid=(n_seq // nseq_blk, tiles_per_seq),
        in_specs=[x_spec, flat, flat, a_spec, a_spec, a_spec, a_spec,
                  _const_spec((1, D_MODEL)), _const_spec((1, D_MODEL)),
                  _const_spec((1, D_MODEL)), _const_spec((1, D_MODEL)),
                  _const_spec((1, ATTN_WIDTH)), _const_spec((1, CONV_CH)),
                  _const_spec((D_MODEL, D_MODEL)),
                  _const_spec((D_MODEL, D_FF)), _const_spec((D_MODEL, D_FF)),
                  _const_spec((D_FF, D_MODEL))] + extra_in_specs,
        out_specs=out_specs,
        out_shape=out_shape,
        scratch_shapes=scratch,
        compiler_params=pltpu.CompilerParams(
            dimension_semantics=("arbitrary", "arbitrary"), vmem_limit_bytes=VMEM_LIMIT),
        name="stage_c",
    )(x1, attn, conv, ada["g1"], ada["sh2"], ada["sc2"], ada["g2"],
      ln_g1, ln_b1, ln_g2, ln_b2, beta_a, beta_c, wo, wg, wu, wd, *extra_in)
    return res if sample is not None else res[0]


def _conv_rows(buf_ref, halo, cur, w_ref, bdw_ref, gng_ref, gnb_ref, gavg_ref):
    nb, ch, c = cur.shape
    pad = CONV_PAD_ROWS
    buf_ref[:, 0:pad - HIST, :] = jnp.zeros((nb, pad - HIST, c), F32)
    buf_ref[:, pad - HIST:pad, :] = halo
    buf_ref[:, pad:pad + ch, :] = cur
    buf_ref[:, pad + ch:, :] = jnp.zeros((nb, SUBLANES, c), F32)
    acc = None
    for r in range(SUBLANES):
        part = None
        for a in range((pad + SUBLANES) // SUBLANES):
            j = SUBLANES * a + r - (pad - HIST)
            if 0 <= j < CONV_K:
                term = buf_ref[:, SUBLANES * a:SUBLANES * a + ch + SUBLANES, :] * w_ref[j:j + 1, :]
                part = term if part is None else part + term
        shifted = part[:, r:r + ch, :]
        acc = shifted if acc is None else acc + shifted
    y = (acc + bdw_ref[...]).reshape(nb * ch, c)
    gavg = gavg_ref[...]

    def group_mean(t):
        hi, lo = _split_bf16(t)
        return (jnp.dot(hi, gavg, preferred_element_type=F32)
                + jnp.dot(lo, gavg, preferred_element_type=F32))

    mu = group_mean(y)
    dlt = y - mu
    var = group_mean(dlt * dlt)
    z = dlt * lax.rsqrt(var + LN_EPS) * gng_ref[...] + gnb_ref[...]
    return _silu(z)


def _conv_kernel(n_chunks, hist_ref, prev_ref, cur_ref, w_ref, bdw_ref, gng_ref, gnb_ref, gavg_ref,
                 o_ref, buf_ref):
    nb, ch, c = cur_ref.shape
    if n_chunks == 1:
        halo = hist_ref[...]
    else:
        halo = jnp.where(pl.program_id(1) == 0, hist_ref[...], prev_ref[:, ch - HIST:, :])
    out = _conv_rows(buf_ref, halo, cur_ref[...], w_ref, bdw_ref, gng_ref, gnb_ref, gavg_ref)
    o_ref[...] = out.reshape(nb, ch, c)


def _group_average_matrix(c):
    grp = np.arange(c) // (c // GN_GROUPS)
    return jnp.asarray((grp[:, None] == grp[None, :]).astype(np.float32) / (c // GN_GROUPS), BF16)


def _conv_branch(hist, glu3, w_dw, b_dw, gn_g, gn_b, *, nb, ch):
    n_seq, s, c = glu3.shape
    n_chunks = s // ch
    assert ch >= HIST or n_chunks == 1
    gavg = _group_average_matrix(c)
    cur_spec = pl.BlockSpec((nb, ch, c), lambda i, j: (i, j, 0))
    prev_spec = pl.BlockSpec((nb, ch, c), lambda i, j: (i, jnp.maximum(j - 1, 0), 0))
    return pl.pallas_call(
        functools.partial(_conv_kernel, n_chunks),
        grid=(n_seq // nb, n_chunks),
        in_specs=[pl.BlockSpec((nb, HIST, c), lambda i, j: (i, 0, 0)), prev_spec, cur_spec,
                  pl.BlockSpec((CONV_K, c), lambda i, j: (0, 0)),
                  pl.BlockSpec((1, c), lambda i, j: (0, 0)),
                  pl.BlockSpec((1, c), lambda i, j: (0, 0)),
                  pl.BlockSpec((1, c), lambda i, j: (0, 0)),
                  pl.BlockSpec((c, c), lambda i, j: (0, 0))],
        out_specs=cur_spec,
        out_shape=jax.ShapeDtypeStruct((n_seq, s, c), F32),
        scratch_shapes=[pltpu.VMEM((nb, CONV_PAD_ROWS + ch + SUBLANES, c), F32)],
        compiler_params=pltpu.CompilerParams(dimension_semantics=("arbitrary", "arbitrary")),
        name="conv_branch",
    )(hist, glu3, glu3, w_dw, b_dw.reshape(1, c), gn_g.reshape(1, c), gn_b.reshape(1, c), gavg)


MOBA_HEADS_PER_STEP = 8


def _moba_prompt_kernel(slopes_ref, qt_ref, k16_ref, vt16_ref, kmean_ref, o_ref,
                        q16_ref, bias_ref, krs_ref, s_ref, m_ref, l_ref, acc_ref):
    grp0 = pl.program_id(1)
    qb = pl.program_id(2)
    blk = MOBA_BLOCK
    nb = kmean_ref.shape[1]
    hps = MOBA_HEADS_PER_STEP
    row_head = lax.broadcasted_iota(jnp.int32, (LANES, blk), 0) // HEAD_DIM
    key_i = lax.broadcasted_iota(jnp.int32, (blk, blk), 0)
    qry_i = lax.broadcasted_iota(jnp.int32, (blk, blk), 1)
    key_f = key_i.astype(F32)
    blk_id = lax.broadcasted_iota(jnp.int32, (nb, blk), 0)
    past = blk_id < qb
    zero_row = jnp.zeros((1, blk), jnp.int32)

    def lanes_of(hh):
        g = hh // HEADS_PER_GROUP
        return slice(g * LANES, (g + 1) * LANES)

    def scores(hh, n):
        kb = k16_ref[0, pl.ds(pl.multiple_of(n * blk, blk), blk), lanes_of(hh)]
        return jnp.dot(kb, q16_ref[hh], preferred_element_type=F32) + krs_ref[hh]

    def pv(hh, n, p):
        p16 = p.astype(BF16)
        acc = None
        for i in range(PAGES_PER_BLOCK):
            vt = vt16_ref[0, n * PAGES_PER_BLOCK + i, hh]
            part = jnp.dot(vt, p16[i * PAGE_SIZE:(i + 1) * PAGE_SIZE, :], preferred_element_type=F32)
            acc = part if acc is None else acc + part
        return acc

    @pl.when(qb == 0)
    def _():
        for hh in range(hps):
            krs_ref[hh] = key_f * slopes_ref[grp0 * hps + hh]

    for hh in range(hps):
        sub = hh % HEADS_PER_GROUP
        qh = jnp.where(row_head == sub, qt_ref[0, lanes_of(hh), :], 0.0)
        q16_ref[hh] = (qh * (SCALE * LOG2E)).astype(BF16)

        gate = jnp.dot(kmean_ref[0, :, lanes_of(hh)], qh, preferred_element_type=F32, precision=HIGHEST)
        cnt = jnp.zeros((nb, blk), F32)
        for mth in range(nb):
            other = gate[mth:mth + 1, :]
            ahead = (other > gate) | ((other == gate) & (blk_id > mth))
            cnt = cnt + jnp.where(ahead & (mth < qb), 1.0, 0.0)
        bias_ref[hh] = jnp.where((cnt < MOBA_TOPK) & past, 0.0, NEG_INF)

        s = jnp.where(key_i <= qry_i, scores(hh, qb), NEG_INF)
        s_ref[hh, qb] = s
        m_ref[hh] = jnp.max(s, axis=0, keepdims=True)

    def block_const(hh, n):
        rel = ((n - qb) * blk + zero_row).astype(F32)
        return slopes_ref[grp0 * hps + hh] * rel + bias_ref[hh, pl.ds(n, 1), :]

    def pass1(n, carry):
        for hh in range(hps):
            s = scores(hh, n)
            s_ref[hh, n] = s
            m_ref[hh] = jnp.maximum(m_ref[hh], jnp.max(s, axis=0, keepdims=True) + block_const(hh, n))
        return carry

    lax.fori_loop(0, qb, pass1, 0)

    def block_out(hh, n, c):
        p = jnp.exp2(s_ref[hh, n] - (m_ref[hh] - c))
        return jnp.sum(p, axis=0, keepdims=True), pv(hh, n, p)

    for hh in range(hps):
        l_ref[hh], acc_ref[hh] = block_out(hh, qb, 0.0)

    def pass2(n, carry):
        for hh in range(hps):
            l_part, acc_part = block_out(hh, n, block_const(hh, n))
            l_ref[hh] = l_ref[hh] + l_part
            acc_ref[hh] = acc_ref[hh] + acc_part
        return carry

    lax.fori_loop(0, qb, pass2, 0)
    out_t = jnp.concatenate([acc_ref[hh] / l_ref[hh] for hh in range(hps)], axis=0)
    o_ref[0] = out_t.T


def _moba_prompt(qt3, k16, vt16, kmean, slopes):
    b, s, _ = k16.shape
    hps = MOBA_HEADS_PER_STEP
    width = hps * HEAD_DIM
    n_blocks = s // MOBA_BLOCK
    return pl.pallas_call(
        _moba_prompt_kernel,
        grid=(b, N_HEADS // hps, n_blocks),
        in_specs=[pl.BlockSpec(memory_space=pltpu.SMEM),
                  pl.BlockSpec((1, width, MOBA_BLOCK), lambda i, g, j: (i, g, j)),
                  pl.BlockSpec((1, s, width), lambda i, g, j: (i, 0, g)),
                  pl.BlockSpec((1, s // PAGE_SIZE, hps, HEAD_DIM, PAGE_SIZE),
                               lambda i, g, j: (i, 0, g, 0, 0)),
                  pl.BlockSpec((1, n_blocks, width), lambda i, g, j: (i, 0, g))],
        out_specs=pl.BlockSpec((1, MOBA_BLOCK, width), lambda i, g, j: (i, j, g)),
        out_shape=jax.ShapeDtypeStruct((b, s, ATTN_WIDTH), F32),
        scratch_shapes=[pltpu.VMEM((hps, LANES, MOBA_BLOCK), BF16),
                        pltpu.VMEM((hps, n_blocks, MOBA_BLOCK), F32),
                        pltpu.VMEM((hps, MOBA_BLOCK, MOBA_BLOCK), F32),
                        pltpu.VMEM((hps, n_blocks, MOBA_BLOCK, MOBA_BLOCK), F32),
                        pltpu.VMEM((hps, 1, MOBA_BLOCK), F32),
                        pltpu.VMEM((hps, 1, MOBA_BLOCK), F32),
                        pltpu.VMEM((hps, HEAD_DIM, MOBA_BLOCK), F32)],
        compiler_params=pltpu.CompilerParams(
            dimension_semantics=("arbitrary", "arbitrary", "arbitrary"), vmem_limit_bytes=VMEM_LIMIT),
        name="moba_prompt",
    )(slopes, qt3, k16, vt16, kmean)


TOPK_SEQS_PER_STEP = 4


def _topk_kernel(q_ref, kmt_ref, o_ref):
    ns, t, nbp = q_ref.shape[0], q_ref.shape[1], kmt_ref.shape[3]
    rows = ns * N_HEADS * t
    gates = []
    for s in range(ns):
        q = q_ref[s]
        for h in range(N_HEADS):
            gates.append(jnp.dot(q[:, h * HEAD_DIM:(h + 1) * HEAD_DIM], kmt_ref[s, h],
                                 preferred_element_type=F32, precision=HIGHEST))
    g = jnp.concatenate(gates, axis=0)
    lane = lax.broadcasted_iota(jnp.int32, (rows, nbp), 1).astype(F32)
    out_lane = lax.broadcasted_iota(jnp.int32, (rows, LANES), 1)
    res = jnp.zeros((rows, LANES), F32)
    for r in range(MOBA_TOPK):
        mx = jnp.max(g, axis=1, keepdims=True)
        idx = jnp.min(jnp.where(g == mx, lane, float(nbp)), axis=1, keepdims=True)
        res = jnp.where(out_lane == r, idx, res)
        g = jnp.where(lane == idx, -jnp.inf, g)
    o_ref[...] = res.astype(jnp.int32).reshape(ns, N_HEADS, t, LANES)


def _topk_blocks(q3, kmean_t):
    n_seq, t, _ = q3.shape
    nbp = kmean_t.shape[3]
    ns = TOPK_SEQS_PER_STEP
    return pl.pallas_call(
        _topk_kernel,
        grid=(n_seq // ns,),
        in_specs=[pl.BlockSpec((ns, t, ATTN_WIDTH), lambda b: (b, 0, 0)),
                  pl.BlockSpec((ns, N_HEADS, HEAD_DIM, nbp), lambda b: (b, 0, 0, 0))],
        out_specs=pl.BlockSpec((ns, N_HEADS, t, LANES), lambda b: (b, 0, 0, 0)),
        out_shape=jax.ShapeDtypeStruct((n_seq, N_HEADS, t, LANES), jnp.int32),
        name="topk_blocks",
    )(q3, kmean_t)


def _sample_attention_stream(step, n_steps, past_len, n_pages, idx_ref, pt_ref, slopes_ref,
                             qt_ref, knt_ref, vnt_ref, pool_k_ref, pool_v_ref, o_ref,
                             kbuf_ref, vbuf_ref, sem_ref):
    t = qt_ref.shape[3]
    slabs = MOBA_TOPK * PAGES_PER_BLOCK
    n_k = t * slabs

    def slab_copies(page, head, slot, n):
        return (pltpu.make_async_copy(pool_k_ref.at[page, head], kbuf_ref.at[slot, n], sem_ref.at[slot]),
                pltpu.make_async_copy(pool_v_ref.at[page, head], vbuf_ref.at[slot, n], sem_ref.at[slot]))

    def start_gather(seq, head, slot):
        for n in range(n_k):
            tq, rem = divmod(n, slabs)
            j, i = divmod(rem, PAGES_PER_BLOCK)
            blk_idx = idx_ref[((seq * N_HEADS + head) * t + tq) * MOBA_TOPK + j]
            page = pt_ref[seq * n_pages + blk_idx * PAGES_PER_BLOCK + i]
            for prio, cp in enumerate(slab_copies(page, head, slot, n)):
                cp.start(priority=prio)

    n_slots = kbuf_ref.shape[0]
    ahead = n_slots - 1
    assert ahead < N_HEADS

    def ring_step(h):
        unit = step * N_HEADS + h
        slot = lax.rem(unit, n_slots)
        if h == 0:
            @pl.when(step == 0)
            def _():
                for a in range(ahead):
                    start_gather(0, a, a)
        nxt_slot = lax.rem(unit + ahead, n_slots)
        if h + ahead < N_HEADS:
            start_gather(step, h + ahead, nxt_slot)
        else:
            @pl.when(step + 1 < n_steps)
            def _():
                start_gather(step + 1, h + ahead - N_HEADS, nxt_slot)
        for n in range(n_k):
            for cp in slab_copies(0, 0, slot, n):
                cp.wait()

        slope = slopes_ref[h]
        qt = qt_ref[0, h] * SCALE
        knt = knt_ref[0, h]
        vnt = vnt_ref[0, h]
        srow = lax.broadcasted_iota(jnp.int32, (slabs, PAGE_SIZE), 0)
        lane = lax.broadcasted_iota(jnp.int32, (slabs, PAGE_SIZE), 1)
        own_pos = lax.broadcasted_iota(jnp.int32, (1, t), 1)
        out_lane = lax.broadcasted_iota(jnp.int32, (HEAD_DIM, t), 1)
        out = jnp.zeros((HEAD_DIM, t), F32)
        for tq in range(t):
            q_col = qt[:, tq:tq + 1]
            base = ((step * N_HEADS + h) * t + tq) * MOBA_TOPK
            s_rows = []
            blk_of_row = jnp.zeros((slabs, PAGE_SIZE), jnp.int32)
            for j in range(MOBA_TOPK):
                blk_of_row = jnp.where(srow // PAGES_PER_BLOCK == j, idx_ref[base + j], blk_of_row)
                for i in range(PAGES_PER_BLOCK):
                    kt = kbuf_ref[slot, (tq * MOBA_TOPK + j) * PAGES_PER_BLOCK + i]
                    s_rows.append(jnp.sum(kt * q_col, axis=0, keepdims=True))
            s_sel = jnp.concatenate(s_rows, axis=0)
            k_pos = blk_of_row * MOBA_BLOCK + (srow % PAGES_PER_BLOCK) * PAGE_SIZE + lane
            s_sel = s_sel - slope * ((past_len + tq) - k_pos).astype(F32)
            rel_own = tq - own_pos
            s_own = jnp.sum(knt * q_col, axis=0, keepdims=True)
            s_own = jnp.where(rel_own >= 0, s_own - slope * rel_own.astype(F32), NEG_INF)
            m = jnp.maximum(jnp.max(jnp.max(s_sel, axis=1, keepdims=True), axis=0, keepdims=True),
                            jnp.max(s_own, axis=1, keepdims=True))
            p_sel = jnp.exp(s_sel - m)
            p_own = jnp.exp(s_own - m)
            l = (jnp.sum(jnp.sum(p_sel, axis=1, keepdims=True), axis=0, keepdims=True)
                 + jnp.sum(p_own, axis=1, keepdims=True))
            acc = jnp.zeros((HEAD_DIM, PAGE_SIZE), F32)
            for r in range(slabs):
                acc = acc + vbuf_ref[slot, tq * slabs + r] * p_sel[r:r + 1, :]
            o_col = (jnp.sum(acc, axis=1, keepdims=True) + jnp.sum(vnt * p_own, axis=1, keepdims=True)) / l
            out = jnp.where(out_lane == tq, o_col, out)
        o_ref[0, h] = out

    return ring_step


def _split_ada(ada_rows):
    a = ada_rows.reshape(ada_rows.shape[0], 3, 3, 1, D_MODEL)
    return {"sh0": a[:, 0, 0], "sc0": a[:, 0, 1], "g0": a[:, 0, 2],
            "sh1": a[:, 1, 0], "sc1": a[:, 1, 1], "g1": a[:, 1, 2],
            "sh2": a[:, 2, 0], "sc2": a[:, 2, 1], "g2": a[:, 2, 2]}


def _heads(t2, n_seq, s):
    return t2.reshape(n_seq, s, N_HEADS, HEAD_DIM)


def kernel(x_prompt, x_sample, cache_k, cache_v, state_conv, page_table, c_prompt, c_sample,
           w_ada, b_ada, ffn1_wg, ffn1_wu, ffn1_wd, w_in, w_dw, b_dw, gn_g, gn_b,
           beta_attn, beta_conv, w_out, ffn2_wg, ffn2_wu, ffn2_wd, ln_g, ln_b):
    depth = w_ada.shape[0]
    alpha = (2.0 * depth) ** 0.25
    batch, seq, _ = x_prompt.shape
    dec_batch, dec_seq, _ = x_sample.shape
    n_pages = page_table.shape[1]
    past_len = n_pages * PAGE_SIZE
    assert past_len % MOBA_BLOCK == 0 and dec_seq <= MOBA_BLOCK
    alibi = 2.0 ** (-8.0 * np.arange(1, N_HEADS + 1) / N_HEADS)
    slopes = jnp.asarray(alibi, F32)
    slopes_log2 = jnp.asarray(alibi * LOG2E, F32)
    page_table_flat = page_table.reshape(-1)

    y_p, y_s = x_prompt, x_sample
    outs = {name: [] for name in ("kp", "vp", "cp", "ks", "vs", "cs")}
    for l in range(depth):
        ada = _ada(jnp.concatenate([c_prompt, c_sample], axis=0), w_ada[l], b_ada[l])
        ada_p, ada_s = _split_ada(ada[:batch]), _split_ada(ada[batch:])
        lng = [ln_g[l, i].reshape(1, D_MODEL) for i in range(3)]
        lnb = [ln_b[l, i].reshape(1, D_MODEL) for i in range(3)]
        w1 = (ffn1_wg[l].astype(BF16), ffn1_wu[l].astype(BF16), ffn1_wd[l].astype(BF16))
        w2 = (ffn2_wg[l].astype(BF16), ffn2_wu[l].astype(BF16), ffn2_wd[l].astype(BF16))
        win16, wo16 = w_in[l].astype(BF16), w_out[l].astype(BF16)
        beta_a = beta_attn[l].reshape(1, ATTN_WIDTH)
        beta_c = beta_conv[l].reshape(1, CONV_CH)

        tile_p = dict(alpha=alpha, nseq_blk=1, rows_blk=PROMPT_ROWS_PER_STEP)
        tile_s = dict(alpha=alpha, nseq_blk=dec_batch, rows_blk=dec_seq)
        pool_kt = jnp.swapaxes(cache_k[l], -1, -2)
        pool_vt = jnp.swapaxes(cache_v[l], -1, -2)

        x1_s, q, k, v, glu_s = _stage_a_sample(y_s, ada_s, lng[0], lnb[0], *w1, win16, alpha=alpha)
        q4, k4, v4 = (_heads(t2, dec_batch, dec_seq) for t2 in (q, k, v))

        x1, kmean_t = _stage_a_prompt(y_p, ada_p, lng[0], lnb[0], *w1, page_table_flat, pool_kt, dec_batch,
                                      alpha=alpha, rows_blk=PROMPT_ROWS_PER_STEP)
        qt, k16, kmean, ktp, vtp, vt16, conv, glu_tail = _stage_b(
            x1, ada_p, win16, w_dw[l], b_dw[l], gn_g[l], gn_b[l], rows_blk=PROMPT_ROWS_PER_STEP)
        idx = _topk_blocks(q.reshape(dec_batch, dec_seq, ATTN_WIDTH), kmean_t)[..., :MOBA_TOPK]
        attn = _moba_prompt(qt, k16.reshape(batch, seq, ATTN_WIDTH), vt16, kmean, slopes_log2)
        to_t = lambda a4: a4.transpose(0, 2, 3, 1)
        sample = dict(qt=to_t(q4), knt=to_t(k4), vnt=to_t(v4), pool_kt=pool_kt, pool_vt=pool_vt,
                      idx_flat=idx.reshape(-1), page_table_flat=page_table_flat, slopes=slopes,
                      past_len=past_len, n_pages=n_pages)
        y_p, attn_t = _stage_c(x1, attn.reshape(batch * seq, ATTN_WIDTH), conv, ada_p,
                               lng[1], lnb[1], lng[2], lnb[2], beta_a, beta_c, wo16, *w2,
                               sample=sample, **tile_p)
        outs["kp"].append(jnp.swapaxes(ktp, -1, -2))
        outs["vp"].append(jnp.swapaxes(vtp, -1, -2))
        outs["cp"].append(glu_tail[:, CONV_PAD_ROWS - HIST:, :])

        x1, glu = x1_s, glu_s
        attn = attn_t.transpose(0, 3, 1, 2).reshape(dec_batch * dec_seq, ATTN_WIDTH)
        glu3 = glu.reshape(dec_batch, dec_seq, CONV_CH)
        hist = state_conv[l]
        conv = _conv_branch(hist, glu3, w_dw[l], b_dw[l], gn_g[l], gn_b[l], nb=dec_batch, ch=dec_seq)
        y_s = _stage_c(x1, attn, conv.reshape(dec_batch * dec_seq, CONV_CH),
                       ada_s, lng[1], lnb[1], lng[2], lnb[2], beta_a, beta_c, wo16, *w2, **tile_s)
        outs["ks"].append(k4.transpose(0, 2, 1, 3))
        outs["vs"].append(v4.transpose(0, 2, 1, 3))
        outs["cs"].append(jnp.concatenate([hist, glu3], axis=1)[:, dec_seq:, :])

    stack = lambda name: jnp.stack(outs[name], 0)
    return (y_p, y_s, stack("kp"), stack("vp"), stack("cp"), stack("ks"), stack("vs"), stack("cs"))
```
